```python
import math
import jax, jax.numpy as jnp
from jax import lax
import numpy as np

D_MODEL = 1024
BATCH = 16
SEQ = 256
DEPTH = 4
DEC_BATCH = 8
DEC_SEQ = 2048
PAST_LEN = 512

GRID_W = 64
EXPAND = 2
D_INNER = EXPAND * D_MODEL
EPS = 1e-6
S5_WIDTH = D_INNER // 2
S5_GROUP = 16
S5_GROUPS = S5_WIDTH // S5_GROUP
S5_STATE = 64
HY_WIDTH = D_INNER - S5_WIDTH
HY_ORDER = 2
HY_SHORT = 3
HY_BANDS = 16
HY_EMB = 1 + 2 * HY_BANDS
HY_HID = 64
HY_DECAY_TARGET = 1e-2
HY_FAST = 0.3
HY_SLOW = 1.5
EVEN_PROJ = 2 * S5_WIDTH + 4 * HY_WIDTH
SSD_HEADDIM = 64
SSD_HEADS = D_INNER // SSD_HEADDIM
SSD_STATE = 128
SSD_GROUPS = 4
SSD_CONV = 3
SSD_CHUNK = 128
SSD_CONV_DIM = D_INNER + 2 * SSD_GROUPS * SSD_STATE
ODD_PROJ = D_INNER + SSD_CONV_DIM + 2 * SSD_HEADS
N_S5_LAYERS = (DEPTH + 1) // 2
N_SSD_LAYERS = DEPTH // 2

kernel_name = 'hybrid_s5_hyena_ssd_diffusion_step'

F32 = jnp.float32


def rmsnorm(x, w):
    xf = x.astype(F32)
    return xf * lax.rsqrt(jnp.mean(xf * xf, axis=-1, keepdims=True) + EPS) * w.astype(F32)


def dwconv(x, w, b):
    k = w.shape[0]
    y = lax.conv_general_dilated(x, w.astype(x.dtype)[:, None, :], window_strides=(1,),
                                 padding=[(k // 2, k // 2)], dimension_numbers=('NWC', 'WIO', 'NWC'),
                                 feature_group_count=x.shape[-1])
    return y + b


def s5_scan(ug, lam_bar, b_bar, c_mat, h0, reverse):
    bu = jnp.einsum('blgh,gph->blgp', ug.astype(jnp.complex64), b_bar)
    if reverse:
        bu = jnp.flip(bu, 1)
    bu = bu.at[:, 0].add(lam_bar * h0)
    a = jnp.broadcast_to(lam_bar, bu.shape)

    def combine(left, right):
        a_l, b_l = left
        a_r, b_r = right
        return a_r * a_l, a_r * b_l + b_r

    _, h = lax.associative_scan(combine, (a, bu), axis=1)
    h_final = h[:, -1]
    if reverse:
        h = jnp.flip(h, 1)
    y = jnp.einsum('blgp,ghp->blgh', h, c_mat).real
    return y, h_final


def s5_mixer(u, a_re, a_im, log_dt, b_re, b_im, c_re, c_im, d, glu_w, glu_b, h0):
    bsz, L, _ = u.shape
    ug = u.reshape(bsz, L, S5_GROUPS, S5_GROUP)
    lam = lax.complex(a_re.astype(F32), a_im.astype(F32))
    dt = jnp.exp(log_dt.astype(F32))[..., None]
    lam_bar = jnp.exp(lam * dt)
    b_bar = ((lam_bar - 1.0) / lam)[..., None] * lax.complex(b_re.astype(F32), b_im.astype(F32))
    c_mat = lax.complex(c_re.astype(F32), c_im.astype(F32))
    y_f, h_f = s5_scan(ug, lam_bar[0], b_bar[0], c_mat[0], h0[:, 0], reverse=False)
    y_b, h_b = s5_scan(ug, lam_bar[1], b_bar[1], c_mat[1], h0[:, 1], reverse=True)
    y = (y_f + y_b).reshape(bsz, L, S5_WIDTH) + u * d
    y = jax.nn.gelu(y)
    y = y * jax.nn.sigmoid(y @ glu_w + glu_b)
    return y, jnp.stack([h_f, h_b], axis=1)


def hyena_filters(L, w1, b1, freq1, w2, b2, freq2, w3):
    t = jnp.arange(L, dtype=F32)[:, None]
    t_norm = t / (L - 1)
    bands = jnp.linspace(1e-4, HY_BANDS - 1, HY_BANDS, dtype=F32)
    ang = (2.0 * math.pi / L) * t * bands
    emb = jnp.concatenate([t_norm, jnp.cos(ang), jnp.sin(ang)], axis=-1)
    f = jnp.sin(freq1 * (emb @ w1 + b1))
    f = jnp.sin(freq2 * (f @ w2 + b2))
    f = (f @ w3).reshape(L, HY_ORDER, 2, HY_WIDTH)
    deltas = jnp.abs(jnp.linspace(math.log(HY_DECAY_TARGET) / HY_SLOW, math.log(HY_DECAY_TARGET) / HY_FAST,
                                  HY_WIDTH, dtype=F32))
    window = jnp.exp(-t_norm * deltas)
    return f * window[:, None, None, :]


def long_conv(v, h_fwd, h_bwd, bias):
    L = v.shape[1]
    k = jnp.concatenate([h_fwd, jnp.zeros_like(h_fwd[:1]), jnp.flip(h_bwd[1:], 0)], axis=0)
    k_f = jnp.fft.rfft(k, n=2 * L, axis=0)
    v_f = jnp.fft.rfft(v, n=2 * L, axis=1)
    y = jnp.fft.irfft(v_f * k_f[None], n=2 * L, axis=1)[:, :L]
    return y + v * bias


def hyena_mixer(vx, short_w, short_b, w1, b1, freq1, w2, b2, freq2, w3, bias):
    L = vx.shape[1]
    vx = dwconv(vx, short_w, short_b)
    v, x1, x2 = jnp.split(vx, 3, axis=-1)
    filt = hyena_filters(L, w1, b1, freq1, w2, b2, freq2, w3)
    z = x1 * long_conv(v, filt[:, 0, 0], filt[:, 0, 1], bias[0])
    z = x2 * long_conv(z, filt[:, 1, 0], filt[:, 1, 1], bias[1])
    return z


def even_mixer(h, w_in, w_out, s5p, hyp, s5_h0):
    proj = h @ w_in
    u, g_a, vx, g_b = jnp.split(proj, [S5_WIDTH, 2 * S5_WIDTH, 2 * S5_WIDTH + 3 * HY_WIDTH], axis=-1)
    y_a, s5_final = s5_mixer(u, *s5p, s5_h0)
    y_b = hyena_mixer(vx, *hyp)
    y = jnp.concatenate([y_a * jax.nn.silu(g_a), y_b * jax.nn.silu(g_b)], axis=-1)
    return y @ w_out, s5_final


def ssd_scan(x, dt, a, b, c, h0):
    bsz, L, H, P = x.shape
    G, N = b.shape[2], b.shape[3]
    R, Q = H // G, SSD_CHUNK
    nc = L // Q
    x = x.reshape(bsz, nc, Q, G, R, P)
    dt = dt.reshape(bsz, nc, Q, G, R)
    b = b.reshape(bsz, nc, Q, G, N)
    c = c.reshape(bsz, nc, Q, G, N)
    da_cum = jnp.cumsum(dt * a.reshape(G, R), axis=2)
    xdt = x * dt[..., None]
    seg = da_cum[:, :, :, None] - da_cum[:, :, None, :]
    lower = jnp.tril(jnp.ones((Q, Q), bool))[:, :, None, None]
    decay = jnp.exp(jnp.where(lower, seg, -jnp.inf))
    cb = jnp.einsum('bcign,bcjgn->bcijg', c, b)
    scores = cb[..., None] * decay
    y_diag = jnp.einsum('bcijgr,bcjgrp->bcigrp', scores, xdt)
    xw = xdt * jnp.exp(da_cum[:, :, -1:] - da_cum)[..., None]
    states = jnp.einsum('bcjgn,bcjgrp->bcgrpn', b, xw)
    chunk_decay = jnp.exp(da_cum[:, :, -1])

    def step(hc, inp):
        s, dcy = inp
        return hc * dcy[..., None, None] + s, hc

    h_final, h_prev = lax.scan(step, h0.reshape(bsz, G, R, P, N),
                               (jnp.moveaxis(states, 1, 0), jnp.moveaxis(chunk_decay, 1, 0)))
    h_prev = jnp.moveaxis(h_prev, 0, 1)
    y_off = jnp.einsum('bcign,bcgrpn->bcigrp', c, h_prev) * jnp.exp(da_cum)[..., None]
    y = (y_diag + y_off).reshape(bsz, L, H, P)
    return y, h_final.reshape(bsz, H, P, N)


def ssd_mixer(h, w_in, conv_w, conv_b, a_log, dt_bias, d, norm_w, w_out, h0):
    bsz, L, _ = h.shape
    proj = h @ w_in
    z, xbc, dt_raw = jnp.split(proj, [D_INNER, D_INNER + SSD_CONV_DIM], axis=-1)
    xbc = jax.nn.silu(dwconv(xbc, conv_w, conv_b))
    x, b, c = jnp.split(xbc, [D_INNER, D_INNER + SSD_GROUPS * SSD_STATE], axis=-1)
    x = x.reshape(bsz, L, SSD_HEADS, SSD_HEADDIM)
    b = b.reshape(bsz, L, SSD_GROUPS, SSD_STATE)
    c = c.reshape(bsz, L, SSD_GROUPS, SSD_STATE)
    dt = jax.nn.softplus(dt_raw.reshape(bsz, L, 2, SSD_HEADS) + dt_bias)
    a = -jnp.exp(a_log.astype(F32))
    y_f, h_f = ssd_scan(x, dt[:, :, 0], a[0], b, c, h0[:, 0])
    y_b, h_b = ssd_scan(jnp.flip(x, 1), jnp.flip(dt[:, :, 1], 1), a[1], jnp.flip(b, 1), jnp.flip(c, 1), h0[:, 1])
    y = y_f + jnp.flip(y_b, 1) + x * d[:, None]
    y = rmsnorm(y.reshape(bsz, L, D_INNER) * jax.nn.silu(z), norm_w)
    return y @ w_out, jnp.stack([h_f, h_b], axis=1)


def setup_inputs(seed: int = 0) -> dict:
    key = jax.random.key(seed)
    ks = iter(jax.random.split(key, 64))

    def nrm(shape, scale):
        return scale * jax.random.normal(next(ks), shape, F32)

    def unif(shape, lo, hi):
        return jax.random.uniform(next(ks), shape, F32, lo, hi)

    n5, nd = N_S5_LAYERS, N_SSD_LAYERS
    G, P, GS, W = S5_GROUPS, S5_STATE, S5_GROUP, S5_WIDTH
    dt_ssd = jnp.exp(unif((nd, 2, SSD_HEADS), math.log(1e-3), math.log(1e-1)))
    return {
        'x_prompt': nrm((BATCH, SEQ, D_MODEL), 1.0),
        'x_sample': nrm((DEC_BATCH, DEC_SEQ, D_MODEL), 1.0),
        'state_s5_re': nrm((DEC_BATCH, n5, 2, G, P), 0.5),
        'state_s5_im': nrm((DEC_BATCH, n5, 2, G, P), 0.5),
        'state_ssd': nrm((DEC_BATCH, nd, 2, SSD_HEADS, SSD_HEADDIM, SSD_STATE), 0.5),
        'c': nrm((DEC_BATCH, D_MODEL), 1.0),
        'c_ctx': nrm((D_MODEL,), 1.0),
        'mod_w': nrm((DEPTH, D_MODEL, 3 * D_MODEL), 0.5 * D_MODEL ** -0.5),
        'mod_b': nrm((DEPTH, 3 * D_MODEL), 0.02),
        'norm_w': 1.0 + nrm((DEPTH, D_MODEL), 0.02),
        'final_norm_w': 1.0 + nrm((D_MODEL,), 0.02),
        'ev_w_in': nrm((n5, D_MODEL, EVEN_PROJ), D_MODEL ** -0.5),
        'ev_w_out': nrm((n5, D_INNER, D_MODEL), D_INNER ** -0.5),
        's5_a_re': -0.5 + nrm((n5, 2, G, P), 0.01),
        's5_a_im': math.pi * jnp.arange(P, dtype=F32) + nrm((n5, 2, G, P), 0.01),
        's5_log_dt': unif((n5, 2, G), math.log(1e-3), math.log(1e-1)),
        's5_b_re': nrm((n5, 2, G, P, GS), (2 * GS) ** -0.5),
        's5_b_im': nrm((n5, 2, G, P, GS), (2 * GS) ** -0.5),
        's5_c_re': nrm((n5, 2, G, GS, P), (2 * P) ** -0.5),
        's5_c_im': nrm((n5, 2, G, GS, P), (2 * P) ** -0.5),
        's5_d': 1.0 + nrm((n5, W), 0.1),
        's5_glu_w': nrm((n5, W, W), W ** -0.5),
        's5_glu_b': nrm((n5, W), 0.02),
        'hy_short_w': nrm((n5, HY_SHORT, 3 * HY_WIDTH), HY_SHORT ** -0.5),
        'hy_short_b': nrm((n5, 3 * HY_WIDTH), 0.02),
        'hy_f_w1': nrm((n5, HY_EMB, HY_HID), HY_EMB ** -0.5),
        'hy_f_b1': nrm((n5, HY_HID), 0.02),
        'hy_f_freq1': 1.0 + nrm((n5, HY_HID), 0.01),
        'hy_f_w2': nrm((n5, HY_HID, HY_HID), HY_HID ** -0.5),
        'hy_f_b2': nrm((n5, HY_HID), 0.02),
        'hy_f_freq2': 1.0 + nrm((n5, HY_HID), 0.01),
        'hy_f_w3': nrm((n5, HY_HID, HY_ORDER * 2 * HY_WIDTH), 0.1 * HY_HID ** -0.5),
        'hy_bias': nrm((n5, HY_ORDER, HY_WIDTH), 0.5),
        'ssd_w_in': nrm((nd, D_MODEL, ODD_PROJ), D_MODEL ** -0.5),
        'ssd_conv_w': nrm((nd, SSD_CONV, SSD_CONV_DIM), SSD_CONV ** -0.5),
        'ssd_conv_b': nrm((nd, SSD_CONV_DIM), 0.02),
        'ssd_a_log': jnp.log(unif((nd, 2, SSD_HEADS), 1.0, 16.0)),
        'ssd_dt_bias': dt_ssd + jnp.log(-jnp.expm1(-dt_ssd)),
        'ssd_d': 1.0 + nrm((nd, SSD_HEADS), 0.1),
        'ssd_norm_w': 1.0 + nrm((nd, D_INNER), 0.02),
        'ssd_w_out': nrm((nd, D_INNER, D_MODEL), D_INNER ** -0.5),
    }


def reference(x_prompt, x_sample, state_s5_re, state_s5_im, state_ssd, c, c_ctx, mod_w, mod_b, norm_w,
              final_norm_w, ev_w_in, ev_w_out, s5_a_re, s5_a_im, s5_log_dt, s5_b_re, s5_b_im, s5_c_re, s5_c_im,
              s5_d, s5_glu_w, s5_glu_b, hy_short_w, hy_short_b, hy_f_w1, hy_f_b1, hy_f_freq1, hy_f_w2, hy_f_b2,
              hy_f_freq2, hy_f_w3, hy_bias, ssd_w_in, ssd_conv_w, ssd_conv_b, ssd_a_log, ssd_dt_bias, ssd_d,
              ssd_norm_w, ssd_w_out):
    silu_ctx = jax.nn.silu(c_ctx.astype(F32))[None]
    silu_lat = jax.nn.silu(c.astype(F32))
    xp, xs = x_prompt, x_sample
    new_re, new_im, new_ssd = [], [], []
    for i in range(DEPTH):
        j = i // 2
        sh_p, sc_p, g_p = jnp.split(silu_ctx @ mod_w[i] + mod_b[i], 3, axis=-1)
        sh_s, sc_s, g_s = jnp.split(silu_lat @ mod_w[i] + mod_b[i], 3, axis=-1)
        hp = rmsnorm(xp, norm_w[i]) * (1.0 + sc_p[:, None]) + sh_p[:, None]
        hs = rmsnorm(xs, norm_w[i]) * (1.0 + sc_s[:, None]) + sh_s[:, None]
        if i % 2 == 0:
            s5p = (s5_a_re[j], s5_a_im[j], s5_log_dt[j], s5_b_re[j], s5_b_im[j], s5_c_re[j], s5_c_im[j],
                   s5_d[j], s5_glu_w[j], s5_glu_b[j])
            hyp = (hy_short_w[j], hy_short_b[j], hy_f_w1[j], hy_f_b1[j], hy_f_freq1[j], hy_f_w2[j], hy_f_b2[j],
                   hy_f_freq2[j], hy_f_w3[j], hy_bias[j])
            h0_ctx = jnp.zeros((xp.shape[0], 2, S5_GROUPS, S5_STATE), jnp.complex64)
            h0_lat = lax.complex(state_s5_re[:, j].astype(F32), state_s5_im[:, j].astype(F32))
            out_p, st = even_mixer(hp, ev_w_in[j], ev_w_out[j], s5p, hyp, h0_ctx)
            out_s, _ = even_mixer(hs, ev_w_in[j], ev_w_out[j], s5p, hyp, h0_lat)
            new_re.append(st.real)
            new_im.append(st.imag)
        else:
            sdp = (ssd_w_in[j], ssd_conv_w[j], ssd_conv_b[j], ssd_a_log[j], ssd_dt_bias[j], ssd_d[j],
                   ssd_norm_w[j], ssd_w_out[j])
            h0_ctx = jnp.zeros((xp.shape[0], 2, SSD_HEADS, SSD_HEADDIM, SSD_STATE), F32)
            out_p, st = ssd_mixer(hp, *sdp, h0_ctx)
            out_s, _ = ssd_mixer(hs, *sdp, state_ssd[:, j].astype(F32))
            new_ssd.append(st)
        xp = xp + (g_p[:, None] * out_p).astype(xp.dtype)
        xs = xs + (g_s[:, None] * out_s).astype(xs.dtype)
    y_prompt = rmsnorm(xp, final_norm_w).astype(x_prompt.dtype)
    y_sample = rmsnorm(xs, final_norm_w).astype(x_sample.dtype)
    return (y_prompt, y_sample, jnp.stack(new_re, axis=1), jnp.stack(new_im, axis=1), jnp.stack(new_ssd, axis=1))
```

```python
import functools
import math

import jax
import jax.numpy as jnp
from jax import lax
from jax.experimental import pallas as pl
from jax.experimental.pallas import tpu as pltpu

F32 = jnp.float32
BF16 = jnp.bfloat16
HIGHEST = lax.Precision.HIGHEST

D_MODEL = 1024
DEPTH = 4
D_INNER = 2048
EPS = 1e-6
S5_WIDTH = 1024
S5_GROUP = 16
S5_GROUPS = 64
S5_STATE = 64
HY_WIDTH = 1024
HY_BANDS = 16
HY_HID = 64
HY_DECAY_TARGET = 1e-2
HY_FAST = 0.3
HY_SLOW = 1.5
SSD_HEADDIM = 64
SSD_HEADS = 32
SSD_STATE = 128
SSD_GROUPS = 4
SSD_CHUNK = 128
SSD_CONV_DIM = D_INNER + 2 * SSD_GROUPS * SSD_STATE

LANES = 128
SUBLANES = 8
VMEM_LIMIT = 56 * 1024 * 1024

S5_LANE_CHUNKS = S5_WIDTH // LANES
S5_CHUNK_STATE = (LANES // S5_GROUP) * S5_STATE
S5_STATE_LANES = S5_LANE_CHUNKS * 2 * S5_CHUNK_STATE


def _params(*sem):
    return pltpu.CompilerParams(dimension_semantics=sem, vmem_limit_bytes=VMEM_LIMIT)


def _silu(x):
    return x * jax.nn.sigmoid(x)


def _dot(a, b):
    return jnp.dot(a, b, preferred_element_type=F32)


def _dot_hi(a, b):
    return jnp.dot(a, b, preferred_element_type=F32, precision=HIGHEST)


def _mod_kernel(c_ref, w_ref, b_ref, o_ref):
    s = _silu(c_ref[...])
    o_ref[0] = _dot_hi(s, w_ref[0]) + b_ref[0]


def _modulation(cvec, mod_w, mod_b):
    rows = cvec.shape[0]
    tn = D_MODEL
    return pl.pallas_call(
        _mod_kernel,
        grid=(DEPTH, 3 * D_MODEL // tn),
        in_specs=[
            pl.BlockSpec((rows, D_MODEL), lambda i, n: (0, 0)),
            pl.BlockSpec((1, D_MODEL, tn), lambda i, n: (i, 0, n)),
            pl.BlockSpec((1, 1, tn), lambda i, n: (i, 0, n)),
        ],
        out_specs=pl.BlockSpec((1, rows, tn), lambda i, n: (i, 0, n)),
        out_shape=jax.ShapeDtypeStruct((DEPTH, rows, 3 * D_MODEL), F32),
        compiler_params=_params("arbitrary", "arbitrary"),
        name="modulation",
    )(cvec, mod_w, mod_b.reshape(DEPTH, 1, 3 * D_MODEL))


def _inproj_kernel(x_ref, mod_ref, nw_ref, w_ref, *out_refs, splits):
    x = x_ref[0]
    mod = mod_ref[0]
    shift = mod[:, :D_MODEL]
    scale = mod[:, D_MODEL:2 * D_MODEL]
    ms = jnp.mean(x * x, axis=-1, keepdims=True)
    h = x * lax.rsqrt(ms + EPS) * nw_ref[...] * (1.0 + scale) + shift
    hb = h.astype(BF16)
    for o_ref, (lo, hi) in zip(out_refs, splits):
        o_ref[0] = _dot(hb, w_ref[:, lo:hi])


def _mod_spec(mod, ngrid):
    per_batch = mod.shape[0] > 1
    width = mod.shape[-1]
    if ngrid == 2:
        return pl.BlockSpec((1, 1, width), (lambda b, m: (b, 0, 0)) if per_batch else (lambda b, m: (0, 0, 0)))
    raise ValueError(ngrid)


def _inproj(x, mod, norm_w, w_bf16, widths, tm=256):
    bsz, seq, _ = x.shape
    n_total = w_bf16.shape[1]
    splits, lo = [], 0
    for w in widths:
        splits.append((lo, lo + w))
        lo += w
    assert lo == n_total
    return pl.pallas_call(
        functools.partial(_inproj_kernel, splits=tuple(splits)),
        grid=(bsz, seq // tm),
        in_specs=[
            pl.BlockSpec((1, tm, D_MODEL), lambda b, m: (b, m, 0)),
            _mod_spec(mod, 2),
            pl.BlockSpec((1, D_MODEL), lambda b, m: (0, 0)),
            pl.BlockSpec((D_MODEL, n_total), lambda b, m: (0, 0)),
        ],
        out_specs=[pl.BlockSpec((1, tm, w), lambda b, m: (b, m, 0)) for w in widths],
        out_shape=[jax.ShapeDtypeStruct((bsz, seq, w), F32) for w in widths],
        compiler_params=_params("arbitrary", "arbitrary"),
        name="inproj",
    )(x, mod, norm_w.reshape(1, D_MODEL), w_bf16)


def _s5_kernel(u_ref, bw_ref, cw_ref, lr_ref, li_ref, h0_ref, ys_ref, hf_ref, bu_ref, hst_ref, *, bsz, tq):
    d = pl.program_id(0)
    blk = pl.program_id(1)
    nblk = pl.num_programs(1)
    cs = S5_CHUNK_STATE

    @pl.when(blk == 0)
    def _():
        hst_ref[...] = h0_ref[0]

    u2 = u_ref[...].reshape(bsz * tq, S5_WIDTH).astype(BF16)
    spc = 2 * cs // LANES
    half = spc // 2
    for j in range(S5_LANE_CHUNKS):
        drive = _dot(u2[:, LANES * j:LANES * (j + 1)], bw_ref[0, j])
        for c in range(spc):
            bu_ref[spc * j + c] = drive[:, LANES * c:LANES * (c + 1)]

    for j in range(S5_LANE_CHUNKS):
        lam_re = [jnp.broadcast_to(lr_ref[0, j, :, LANES * c:LANES * (c + 1)], (bsz, LANES)) for c in range(half)]
        lam_im = [jnp.broadcast_to(li_ref[0, j, :, LANES * c:LANES * (c + 1)], (bsz, LANES)) for c in range(half)]
        base = 2 * cs * j

        def body(s, carry, j=j, lam_re=lam_re, lam_im=lam_im):
            t = jnp.where(d == 0, s, tq - 1 - s)
            rows = pl.ds(t, bsz, stride=tq)
            out = []
            for c in range(half):
                hr, hi = carry[c], carry[half + c]
                nr = lam_re[c] * hr - lam_im[c] * hi + bu_ref[spc * j + c, rows, :]
                ni = lam_re[c] * hi + lam_im[c] * hr + bu_ref[spc * j + half + c, rows, :]
                bu_ref[spc * j + c, rows, :] = nr
                bu_ref[spc * j + half + c, rows, :] = ni
                out.append((nr, ni))
            return tuple(o[0] for o in out) + tuple(o[1] for o in out)

        init = tuple(hst_ref[:, base + LANES * c:base + LANES * (c + 1)] for c in range(spc))
        fin = lax.fori_loop(0, tq, body, init)
        for c in range(spc):
            hst_ref[:, base + LANES * c:base + LANES * (c + 1)] = fin[c]

    for j in range(S5_LANE_CHUNKS):
        hb = jnp.concatenate([bu_ref[spc * j + c] for c in range(spc)], axis=-1).astype(BF16)
        ys_ref[0, :, :, LANES * j:LANES * (j + 1)] = _dot(hb, cw_ref[0, j]).reshape(bsz, tq, LANES)

    @pl.when(blk == nblk - 1)
    def _():
        hf_ref[0] = hst_ref[...]


def _s5_scan(u, bw, cw, lr, li, h0, tq):
    bsz, seq, _ = u.shape
    nblk = seq // tq

    def tblk(d, i):
        return i + d * (nblk - 1 - 2 * i)

    nj, cs = S5_LANE_CHUNKS, S5_CHUNK_STATE
    return pl.pallas_call(
        functools.partial(_s5_kernel, bsz=bsz, tq=tq),
        grid=(2, nblk),
        in_specs=[
            pl.BlockSpec((bsz, tq, S5_WIDTH), lambda d, i: (0, tblk(d, i), 0)),
            pl.BlockSpec((1, nj, LANES, 2 * cs), lambda d, i: (d, 0, 0, 0)),
            pl.BlockSpec((1, nj, 2 * cs, LANES), lambda d, i: (d, 0, 0, 0)),
            pl.BlockSpec((1, nj, 1, cs), lambda d, i: (d, 0, 0, 0)),
            pl.BlockSpec((1, nj, 1, cs), lambda d, i: (d, 0, 0, 0)),
            pl.BlockSpec((1, bsz, S5_STATE_LANES), lambda d, i: (d, 0, 0)),
        ],
        out_specs=[
            pl.BlockSpec((1, bsz, tq, S5_WIDTH), lambda d, i: (d, 0, tblk(d, i), 0)),
            pl.BlockSpec((1, bsz, S5_STATE_LANES), lambda d, i: (d, 0, 0)),
        ],
        out_shape=[
            jax.ShapeDtypeStruct((2, bsz, seq, S5_WIDTH), F32),
            jax.ShapeDtypeStruct((2, bsz, S5_STATE_LANES), F32),
        ],
        scratch_shapes=[
            pltpu.VMEM((S5_STATE_LANES // LANES, bsz * tq, LANES), F32),
            pltpu.VMEM((bsz, S5_STATE_LANES), F32),
        ],
        compiler_params=_params("arbitrary", "arbitrary"),
        name="s5_scan",
    )(u, bw, cw, lr, li, h0)


def _s5_prepare(a_re, a_im, log_dt, b_re, b_im, c_re, c_im):
    dt = jnp.exp(log_dt.astype(F32))[..., None]
    a_re, a_im = a_re.astype(F32), a_im.astype(F32)
    mag = jnp.exp(a_re * dt)
    lbr, lbi = mag * jnp.cos(a_im * dt), mag * jnp.sin(a_im * dt)
    den = a_re * a_re + a_im * a_im
    cr = ((lbr - 1.0) * a_re + lbi * a_im) / den
    ci = (lbi * a_re - (lbr - 1.0) * a_im) / den
    bbr = cr[..., None] * b_re - ci[..., None] * b_im
    bbi = cr[..., None] * b_im + ci[..., None] * b_re
    nj, gpc = S5_LANE_CHUNKS, LANES // S5_GROUP
    eye = jnp.eye(gpc, dtype=F32)

    def pack_b(m):
        m = m.reshape(2, nj, gpc, S5_STATE, S5_GROUP)
        return jnp.einsum('djgph,gk->djghkp', m, eye).reshape(2, nj, LANES, S5_CHUNK_STATE)

    def pack_c(m):
        m = m.reshape(2, nj, gpc, S5_GROUP, S5_STATE)
        return jnp.einsum('djghp,gk->djgpkh', m, eye).reshape(2, nj, S5_CHUNK_STATE, LANES)

    bw = jnp.concatenate([pack_b(bbr), pack_b(bbi)], axis=-1).astype(BF16)
    cw = jnp.concatenate([pack_c(c_re.astype(F32)), pack_c(-c_im.astype(F32))], axis=-2).astype(BF16)
    lr = lbr.reshape(2, nj, 1, S5_CHUNK_STATE)
    li = lbi.reshape(2, nj, 1, S5_CHUNK_STATE)
    return bw, cw, lr, li


def _s5_state_to_lanes(re, im):
    bsz = re.shape[0]
    st = jnp.stack([re.reshape(bsz, 2, S5_LANE_CHUNKS, S5_CHUNK_STATE),
                    im.reshape(bsz, 2, S5_LANE_CHUNKS, S5_CHUNK_STATE)], axis=3)
    return jnp.transpose(st.reshape(bsz, 2, S5_STATE_LANES), (1, 0, 2))


def _s5_state_from_lanes(hf):
    bsz = hf.shape[1]
    st = jnp.transpose(hf, (1, 0, 2)).reshape(bsz, 2, S5_LANE_CHUNKS, 2, S5_CHUNK_STATE)
    re = st[:, :, :, 0].reshape(bsz, 2, S5_GROUPS, S5_STATE)
    im = st[:, :, :, 1].reshape(bsz, 2, S5_GROUPS, S5_STATE)
    return re, im


def _conv3(x, prev_row, next_row, w, b):
    rows = x.shape[0]
    ridx = lax.broadcasted_iota(jnp.int32, x.shape, 0)
    xp = jnp.where(ridx == 0, prev_row, pltpu.roll(x, 1, 0))
    xn = jnp.where(ridx == rows - 1, next_row, pltpu.roll(x, rows - 1, 0))
    return w[0:1] * xp + w[1:2] * x + w[2:3] * xn + b


def _halo_specs(width, tl, seq, nargs, blk_of):
    r = tl // SUBLANES
    last = seq // SUBLANES - 1

    def prev_map(*idx):
        return (idx[0], jnp.maximum(blk_of(*idx) * r - 1, 0), 0)

    def next_map(*idx):
        return (idx[0], jnp.minimum((blk_of(*idx) + 1) * r, last), 0)

    return (pl.BlockSpec((1, SUBLANES, width), prev_map), pl.BlockSpec((1, SUBLANES, width), next_map))


def _hy_short_kernel(vx_ref, pv_ref, nx_ref, gb_ref, w_ref, b_ref, v_ref, x1_ref, x2_ref):
    m = pl.program_id(1)
    nm = pl.num_programs(1)
    prev_row = pv_ref[0, SUBLANES - 1:SUBLANES, :] * (m > 0).astype(F32)
    next_row = nx_ref[0, 0:1, :] * (m < nm - 1).astype(F32)
    y = _conv3(vx_ref[0], prev_row, next_row, w_ref[...], b_ref[...])
    v_ref[0] = y[:, :HY_WIDTH]
    x1_ref[0] = y[:, HY_WIDTH:2 * HY_WIDTH]
    x2_ref[0] = y[:, 2 * HY_WIDTH:] * _silu(gb_ref[0])


def _hy_short(vx, g_b, short_w, short_b, tl=256):
    bsz, seq, width = vx.shape
    pv, nx = _halo_specs(width, tl, seq, 2, lambda b, m: m)
    out = jax.ShapeDtypeStruct((bsz, seq, HY_WIDTH), F32)
    ospec = pl.BlockSpec((1, tl, HY_WIDTH), lambda b, m: (b, m, 0))
    return pl.pallas_call(
        _hy_short_kernel,
        grid=(bsz, seq // tl),
        in_specs=[
            pl.BlockSpec((1, tl, width), lambda b, m: (b, m, 0)), pv, nx,
            pl.BlockSpec((1, tl, HY_WIDTH), lambda b, m: (b, m, 0)),
            pl.BlockSpec((3, width), lambda b, m: (0, 0)),
            pl.BlockSpec((1, width), lambda b, m: (0, 0)),
        ],
        out_specs=[ospec, ospec, ospec],
        out_shape=[out, out, out],
        compiler_params=_params("arbitrary", "arbitrary"),
        name="hy_short",
    )(vx, vx, vx, g_b, short_w, short_b.reshape(1, width))


def _hy_filter_kernel(bands_ref, deltas_ref, w1t_ref, w1c_ref, w1s_ref, b1_ref, f1_ref, w2_ref, b2_ref, f2_ref,
                      w3_ref, o_ref, *, seq, tl):
    m = pl.program_id(0)
    t = (lax.broadcasted_iota(jnp.int32, (tl, 1), 0) + m * tl).astype(F32)
    t_norm = t / (seq - 1)
    ang = (2.0 * math.pi / seq) * t * bands_ref[...]
    pre = t_norm * w1t_ref[...] + _dot_hi(jnp.cos(ang), w1c_ref[...]) + _dot_hi(jnp.sin(ang), w1s_ref[...])
    f = jnp.sin(f1_ref[...] * (pre + b1_ref[...]))
    f = jnp.sin(f2_ref[...] * (_dot_hi(f, w2_ref[...]) + b2_ref[...]))
    f = _dot_hi(f, w3_ref[...])
    window = jnp.exp(-t_norm * deltas_ref[...])
    not_lag0 = (t > 0.0).astype(F32)
    for o in range(2):
        h_fwd = f[:, (2 * o) * HY_WIDTH:(2 * o + 1) * HY_WIDTH] * window
        h_bwd = f[:, (2 * o + 1) * HY_WIDTH:(2 * o + 2) * HY_WIDTH] * window * not_lag0
        o_ref[o, 0] = (h_fwd + h_bwd).astype(BF16)
        o_ref[o, 1] = (h_fwd - h_bwd).astype(BF16)


def _hy_filters(seq, w1, b1, freq1, w2, b2, freq2, w3, tl=256):
    bands = jnp.linspace(1e-4, HY_BANDS - 1, HY_BANDS, dtype=F32).reshape(1, HY_BANDS)
    deltas = jnp.abs(jnp.linspace(math.log(HY_DECAY_TARGET) / HY_SLOW, math.log(HY_DECAY_TARGET) / HY_FAST,
                                  HY_WIDTH, dtype=F32)).reshape(1, HY_WIDTH)
    small = [bands, deltas, w1[0:1], w1[1:1 + HY_BANDS], w1[1 + HY_BANDS:], b1.reshape(1, -1), freq1.reshape(1, -1),
             w2, b2.reshape(1, -1), freq2.reshape(1, -1), w3]
    return pl.pallas_call(
        functools.partial(_hy_filter_kernel, seq=seq, tl=tl),
        grid=(seq // tl,),
        in_specs=[pl.BlockSpec(a.shape, lambda m: (0, 0)) for a in small],
        out_specs=pl.BlockSpec((2, 2, tl, HY_WIDTH), lambda m: (0, 0, m, 0)),
        out_shape=jax.ShapeDtypeStruct((2, 2, seq, HY_WIDTH), BF16),
        compiler_params=_params("arbitrary"),
        name="hy_filters",
    )(*small)


def _dft_block(seq):
    return min(256, seq)


def _dft_matrices(seq):
    n = 2 * seq
    kb = _dft_block(seq)
    nb = seq // kb
    k = jnp.arange(seq, dtype=jnp.int32)
    ang = ((k[:, None] * k[None, :]) % n).astype(F32) * (2.0 * math.pi / n)
    cos, sin = jnp.cos(ang), jnp.sin(ang)
    alt = jnp.where(k % 2 == 0, 1.0, -1.0).astype(F32)
    re_rows = cos
    im_rows = jnp.where(k[:, None] == 0, alt[None, :], -sin)
    fwd = jnp.concatenate([re_rows.reshape(nb, kb, seq), im_rows.reshape(nb, kb, seq)], axis=1).reshape(n, seq)
    scale = jnp.where(k == 0, 1.0 / n, 2.0 / n).astype(F32)
    re_cols = cos * scale[None, :]
    im_cols = jnp.where(k[None, :] == 0, alt[:, None], -sin) * scale[None, :]
    inv = jnp.concatenate([re_cols.reshape(seq, nb, kb), im_cols.reshape(seq, nb, kb)], axis=2).reshape(seq, n)
    return fwd.astype(BF16), inv.astype(BF16)


def _hy_spec_kernel(fwd_ref, h_ref, o_ref, *, kb):
    k = pl.program_id(2)
    f = fwd_ref[...]
    s_sum = _dot(f, h_ref[0, 0])
    s_dif = _dot(f, h_ref[0, 1])
    ridx = lax.broadcasted_iota(jnp.int32, s_sum.shape, 0)
    real_row = (ridx < kb) | ((ridx == kb) & (k == 0))
    o_ref[0] = jnp.where(real_row, s_sum, s_dif)


def _hy_spectra(filt, fwd, tc=512):
    _, _, seq, width = filt.shape
    kb = _dft_block(seq)
    nkb = seq // kb
    return pl.pallas_call(
        functools.partial(_hy_spec_kernel, kb=kb),
        grid=(2, width // tc, nkb),
        in_specs=[
            pl.BlockSpec((2 * kb, seq), lambda o, c, k: (k, 0)),
            pl.BlockSpec((1, 2, seq, tc), lambda o, c, k: (o, 0, 0, c)),
        ],
        out_specs=pl.BlockSpec((1, 2 * kb, tc), lambda o, c, k: (o, k, c)),
        out_shape=jax.ShapeDtypeStruct((2, 2 * seq, width), F32),
        compiler_params=_params("arbitrary", "arbitrary", "arbitrary"),
        name="hy_spectra",
    )(fwd, filt)


def _hy_conv_kernel(v_ref, g_ref, bias_ref, fwd_ref, inv_ref, ks_ref, o_ref, vb_ref, acc_ref, *, kb, bb):
    k = pl.program_id(2)
    nk = pl.num_programs(2)

    @pl.when(k == 0)
    def _():
        vb_ref[...] = v_ref[...].astype(BF16)
        acc_ref[...] = jnp.zeros_like(acc_ref)

    f = fwd_ref[...]
    ks = ks_ref[0]
    k_re, k_im = ks[:kb], ks[kb:]
    ridx = lax.broadcasted_iota(jnp.int32, k_re.shape, 0)
    dc_row = (ridx == 0) & (k == 0)
    for i in range(bb):
        s = _dot(f, vb_ref[i])
        s_re, s_im = s[:kb], s[kb:]
        p_re = jnp.where(dc_row, s_re * k_re, s_re * k_re - s_im * k_im)
        p_im = jnp.where(dc_row, s_im * k_im, s_re * k_im + s_im * k_re)
        p = jnp.concatenate([p_re, p_im], axis=0).astype(BF16)
        acc_ref[i] += _dot(inv_ref[...], p)

    @pl.when(k == nk - 1)
    def _():
        o_ref[...] = g_ref[...] * (acc_ref[...] + v_ref[...] * bias_ref[...])


def _hy_conv(v, gate, bias, fwd, inv, kspec, order, bb, tc):
    bsz, seq, width = v.shape
    kb = _dft_block(seq)
    nkb = seq // kb
    blk = pl.BlockSpec((bb, seq, tc), lambda c, b, k: (b, 0, c))
    return pl.pallas_call(
        functools.partial(_hy_conv_kernel, kb=kb, bb=bb),
        grid=(width // tc, bsz // bb, nkb),
        in_specs=[
            blk, blk,
            pl.BlockSpec((1, tc), lambda c, b, k: (0, c)),
            pl.BlockSpec((2 * kb, seq), lambda c, b, k: (k, 0)),
            pl.BlockSpec((seq, 2 * kb), lambda c, b, k: (0, k)),
            pl.BlockSpec((1, 2 * kb, tc), lambda c, b, k: (order, k, c)),
        ],
        out_specs=blk,
        out_shape=jax.ShapeDtypeStruct((bsz, seq, width), F32),
        scratch_shapes=[pltpu.VMEM((bb, seq, tc), BF16), pltpu.VMEM((bb, seq, tc), F32)],
        compiler_params=_params("arbitrary", "arbitrary", "arbitrary"),
        name="hy_conv",
    )(v, gate, bias.reshape(1, width), fwd, inv, kspec)


def _gelu_tanh(x):
    return 0.5 * x * (1.0 + jnp.tanh(math.sqrt(2.0 / math.pi) * (x + 0.044715 * (x * x * x))))


def _rms(x, w):
    return x * lax.rsqrt(jnp.mean(x * x, axis=-1, keepdims=True) + EPS) * w


def _even_tail_kernel(x_ref, mod_ref, yf_ref, yb_ref, u_ref, ga_ref, hy_ref, d_ref, gw_ref, gb_ref, wo_ref,
                      fw_ref, o_ref, *, final):
    y = yf_ref[0, 0] + yb_ref[0, 0] + u_ref[0] * d_ref[...]
    y = _gelu_tanh(y)
    y = y * jax.nn.sigmoid(_dot(y.astype(BF16), gw_ref[...]) + gb_ref[...])
    ya = (y * _silu(ga_ref[0])).astype(BF16)
    out = _dot(ya, wo_ref[:S5_WIDTH, :]) + _dot(hy_ref[0].astype(BF16), wo_ref[S5_WIDTH:, :])
    xn = x_ref[0] + mod_ref[0][:, 2 * D_MODEL:] * out
    o_ref[0] = _rms(xn, fw_ref[...]) if final else xn


def _even_tail(x, mod, ys, u, g_a, hy, s5_d, glu_w, glu_b, w_out, final_w, final, tm=256):
    bsz, seq, _ = x.shape
    tok = lambda w: pl.BlockSpec((1, tm, w), lambda b, m: (b, m, 0))
    const = lambda r, c: pl.BlockSpec((r, c), lambda b, m: (0, 0))
    return pl.pallas_call(
        functools.partial(_even_tail_kernel, final=final),
        grid=(bsz, seq // tm),
        in_specs=[
            tok(D_MODEL), _mod_spec(mod, 2),
            pl.BlockSpec((1, 1, tm, S5_WIDTH), lambda b, m: (0, b, m, 0)),
            pl.BlockSpec((1, 1, tm, S5_WIDTH), lambda b, m: (1, b, m, 0)),
            tok(S5_WIDTH), tok(S5_WIDTH), tok(HY_WIDTH),
            const(1, S5_WIDTH), const(S5_WIDTH, S5_WIDTH), const(1, S5_WIDTH), const(D_INNER, D_MODEL),
            const(1, D_MODEL),
        ],
        out_specs=tok(D_MODEL),
        out_shape=jax.ShapeDtypeStruct((bsz, seq, D_MODEL), F32),
        compiler_params=_params("arbitrary", "arbitrary"),
        name="even_tail",
    )(x, mod, ys, ys, u, g_a, hy, s5_d.reshape(1, -1), glu_w, glu_b.reshape(1, -1), w_out, final_w.reshape(1, -1))


def _softplus(x):
    return jnp.maximum(x, 0.0) + jnp.log1p(jnp.exp(-jnp.abs(x)))


def _ssd_direction(xbc, dt_raw, dtb, a_row, st_ref, direction, d_row):
    q = SSD_CHUNK
    hp = LANES // SSD_HEADDIM
    ii = lax.broadcasted_iota(jnp.int32, (q, q), 0)
    jj = lax.broadcasted_iota(jnp.int32, (q, q), 1)
    if direction == 0:
        tri = (jj <= ii)
        last = q - 1
    else:
        tri = (jj >= ii)
        last = 0
    dt = _softplus(dt_raw + dtb)
    cum = _dot_hi(tri.astype(F32), dt * a_row)
    cum_t = cum.T
    dt_t = dt.T
    last_col = cum_t[:, last:last + 1]
    w_t = dt_t * jnp.exp(last_col - cum_t)
    lane = lax.broadcasted_iota(jnp.int32, (q, LANES), 1)
    first_half = lane < SSD_HEADDIM
    x_all = xbc[:, :D_INNER]
    ys = []
    for g in range(SSD_GROUPS):
        b_g = xbc[:, D_INNER + g * SSD_STATE:D_INNER + (g + 1) * SSD_STATE]
        c_g = xbc[:, D_INNER + (SSD_GROUPS + g) * SSD_STATE:D_INNER + (SSD_GROUPS + g + 1) * SSD_STATE]
        b_bf, c_bf = b_g.astype(BF16), c_g.astype(BF16)
        cb = lax.dot_general(c_bf, b_bf, (((1,), (1,)), ((), ())), preferred_element_type=F32)
        b_t = b_g.T
        heads_per_group = SSD_HEADS // SSD_GROUPS
        for pair in range(heads_per_group // hp):
            h0 = g * heads_per_group + pair * hp
            lo = h0 * SSD_HEADDIM
            x_pair = x_all[:, lo:lo + LANES]
            x_bf = x_pair.astype(BF16)
            yd, new, colb, dec = [], [], [], []
            for h in (h0, h0 + 1):
                hl = direction * SSD_HEADS + h
                col = jnp.broadcast_to(cum[:, hl:hl + 1], (q, q))
                seg = col - cum_t[hl:hl + 1, :]
                scores = cb * jnp.where(tri, jnp.exp(seg), 0.0) * dt_t[hl:hl + 1, :]
                yd.append(_dot(scores.astype(BF16), x_bf))
                new.append(_dot((b_t * w_t[hl:hl + 1, :]).astype(BF16), x_bf))
                colb.append(col)
                dec.append(jnp.broadcast_to(last_col[hl:hl + 1, :], (q, LANES)))
            s_pair = st_ref[:, lo:lo + LANES]
            y_off = _dot(c_bf, s_pair.astype(BF16)) * jnp.exp(jnp.where(first_half, colb[0], colb[1]))
            y = jnp.where(first_half, yd[0], yd[1]) + y_off
            if d_row is not None:
                y = y + x_pair * d_row[:, lo:lo + LANES]
            ys.append(y)
            st_ref[:, lo:lo + LANES] = (s_pair * jnp.exp(jnp.where(first_half, dec[0], dec[1]))
                                        + jnp.where(first_half, new[0], new[1]))
    return jnp.concatenate(ys, axis=-1)


def _ssd_kernel(xf_ref, pf_ref, nf_ref, xb_ref, pb_ref, nb_ref, dtf_ref, dtb_ref, cw_ref, cb_ref, bias_ref,
                alog_ref, d_ref, h0_ref, yf_ref, yb_ref, hf_ref, st_ref):
    c = pl.program_id(1)
    nc = pl.num_programs(1)

    @pl.when(c == 0)
    def _():
        st_ref[...] = h0_ref[0]

    a_row = -jnp.exp(alog_ref[...])
    w, b = cw_ref[...], cb_ref[...]
    has_prev_f = (c > 0).astype(F32)
    has_next_f = (c < nc - 1).astype(F32)
    xf = _silu(_conv3(xf_ref[0], pf_ref[0, SUBLANES - 1:SUBLANES, :] * has_prev_f,
                      nf_ref[0, 0:1, :] * has_next_f, w, b))
    yf_ref[0] = _ssd_direction(xf, dtf_ref[0], bias_ref[...], a_row, st_ref.at[0], 0, d_ref[...])
    xb = _silu(_conv3(xb_ref[0], pb_ref[0, SUBLANES - 1:SUBLANES, :] * has_next_f,
                      nb_ref[0, 0:1, :] * has_prev_f, w, b))
    yb_ref[0] = _ssd_direction(xb, dtb_ref[0], bias_ref[...], a_row, st_ref.at[1], 1, None)

    @pl.when(c == nc - 1)
    def _():
        hf_ref[0] = st_ref[...]


def _ssd_scan(xbc, dt_raw, conv_w, conv_b, dt_bias, a_log, d_exp, h0):
    bsz, seq, width = xbc.shape
    q = SSD_CHUNK
    nc = seq // q
    fwd = lambda b, c: c
    bwd = lambda b, c: nc - 1 - c
    pf, nf = _halo_specs(width, q, seq, 2, fwd)
    pb, nb = _halo_specs(width, q, seq, 2, bwd)
    const = lambda r, cdim: pl.BlockSpec((r, cdim), lambda b, c: (0, 0))
    yspec_f = pl.BlockSpec((1, q, D_INNER), lambda b, c: (b, c, 0))
    yspec_b = pl.BlockSpec((1, q, D_INNER), lambda b, c: (b, nc - 1 - c, 0))
    st_spec = pl.BlockSpec((1, 2, SSD_STATE, D_INNER), lambda b, c: (b, 0, 0, 0))
    return pl.pallas_call(
        _ssd_kernel,
        grid=(bsz, nc),
        in_specs=[
            pl.BlockSpec((1, q, width), lambda b, c: (b, c, 0)), pf, nf,
            pl.BlockSpec((1, q, width), lambda b, c: (b, nc - 1 - c, 0)), pb, nb,
            pl.BlockSpec((1, q, LANES), lambda b, c: (b, c, 0)),
            pl.BlockSpec((1, q, LANES), lambda b, c: (b, nc - 1 - c, 0)),
            const(3, width), const(1, width), const(1, LANES), const(1, LANES), const(1, D_INNER),
            st_spec,
        ],
        out_specs=[yspec_f, yspec_b, st_spec],
        out_shape=[
            jax.ShapeDtypeStruct((bsz, seq, D_INNER), F32),
            jax.ShapeDtypeStruct((bsz, seq, D_INNER), F32),
            jax.ShapeDtypeStruct((bsz, 2, SSD_STATE, D_INNER), F32),
        ],
        scratch_shapes=[pltpu.VMEM((2, SSD_STATE, D_INNER), F32)],
        compiler_params=_params("arbitrary", "arbitrary"),
        name="ssd_scan",
    )(xbc, xbc, xbc, xbc, xbc, xbc, dt_raw, dt_raw, conv_w, conv_b.reshape(1, width), dt_bias, a_log, d_exp, h0)


def _odd_tail_kernel(x_ref, mod_ref, yf_ref, yb_ref, z_ref, nw_ref, wo_ref, fw_ref, o_ref, *, final):
    y = (yf_ref[0] + yb_ref[0]) * _silu(z_ref[0])
    y = _rms(y, nw_ref[...]).astype(BF16)
    xn = x_ref[0] + mod_ref[0][:, 2 * D_MODEL:] * _dot(y, wo_ref[...])
    o_ref[0] = _rms(xn, fw_ref[...]) if final else xn


def _odd_tail(x, mod, yf, yb, z, norm_w, w_out, final_w, final, tm=256):
    bsz, seq, _ = x.shape
    tok = lambda w: pl.BlockSpec((1, tm, w), lambda b, m: (b, m, 0))
    const = lambda r, c: pl.BlockSpec((r, c), lambda b, m: (0, 0))
    return pl.pallas_call(
        functools.partial(_odd_tail_kernel, final=final),
        grid=(bsz, seq // tm),
        in_specs=[tok(D_MODEL), _mod_spec(mod, 2), tok(D_INNER), tok(D_INNER), tok(D_INNER),
                  const(1, D_INNER), const(D_INNER, D_MODEL), const(1, D_MODEL)],
        out_specs=tok(D_MODEL),
        out_shape=jax.ShapeDtypeStruct((bsz, seq, D_MODEL), F32),
        compiler_params=_params("arbitrary", "arbitrary"),
        name="odd_tail",
    )(x, mod, yf, yb, z, norm_w.reshape(1, -1), w_out, final_w.reshape(1, -1))


def _even_layer(x, mod, norm_w, w_in, w_out, s5, s5_d, glu_w, glu_b, short_w, short_b, hy_bias, dft, kspec, h0,
                final_w, final, s5_tq, conv_bb, conv_tc):
    u, g_a, vx, g_b = _inproj(x, mod, norm_w, w_in, (S5_WIDTH, S5_WIDTH, 3 * HY_WIDTH, HY_WIDTH))
    ys, hf = _s5_scan(u, *s5, h0, s5_tq)
    v, x1, x2g = _hy_short(vx, g_b, short_w, short_b)
    fwd, inv = dft
    z1 = _hy_conv(v, x1, hy_bias[0], fwd, inv, kspec, 0, conv_bb, conv_tc)
    hy = _hy_conv(z1, x2g, hy_bias[1], fwd, inv, kspec, 1, conv_bb, conv_tc)
    xn = _even_tail(x, mod, ys, u, g_a, hy, s5_d, glu_w, glu_b, w_out, final_w, final)
    return xn, hf


def _odd_layer(x, mod, norm_w, w_in, conv_w, conv_b, a_log, dt_bias, d_exp, ssd_norm_w, w_out, h0, final_w, final):
    z, xbc, dt_raw = _inproj(x, mod, norm_w, w_in, (D_INNER, SSD_CONV_DIM, LANES))
    yf, yb, hf = _ssd_scan(xbc, dt_raw, conv_w, conv_b, dt_bias, a_log, d_exp, h0)
    xn = _odd_tail(x, mod, yf, yb, z, ssd_norm_w, w_out, final_w, final)
    return xn, hf


def _ssd_state_to_lanes(st):
    bsz = st.shape[0]
    return jnp.transpose(st.astype(F32), (0, 1, 4, 2, 3)).reshape(bsz, 2, SSD_STATE, D_INNER)


def _ssd_state_from_lanes(st):
    bsz = st.shape[0]
    return jnp.transpose(st.reshape(bsz, 2, SSD_STATE, SSD_HEADS, SSD_HEADDIM), (0, 1, 3, 4, 2))


def kernel(x_prompt, x_sample, state_s5_re, state_s5_im, state_ssd, c, c_ctx, mod_w, mod_b, norm_w, final_norm_w,
           ev_w_in, ev_w_out, s5_a_re, s5_a_im, s5_log_dt, s5_b_re, s5_b_im, s5_c_re, s5_c_im, s5_d, s5_glu_w,
           s5_glu_b, hy_short_w, hy_short_b, hy_f_w1, hy_f_b1, hy_f_freq1, hy_f_w2, hy_f_b2, hy_f_freq2, hy_f_w3,
           hy_bias, ssd_w_in, ssd_conv_w, ssd_conv_b, ssd_a_log, ssd_dt_bias, ssd_d, ssd_norm_w, ssd_w_out):
    bp, lp, _ = x_prompt.shape
    bs, ls, _ = x_sample.shape
    rows = 2 * SUBLANES
    cvec = jnp.zeros((rows, D_MODEL), F32).at[0].set(c_ctx.astype(F32)).at[1:1 + bs].set(c.astype(F32))
    mod = _modulation(cvec, mod_w, mod_b)
    dft_p, dft_s = _dft_matrices(lp), _dft_matrices(ls)
    xp, xs = x_prompt, x_sample
    new_re, new_im, new_ssd = [], [], []
    for i in range(DEPTH):
        j = i // 2
        final = i == DEPTH - 1
        mod_p = mod[i, 0:1].reshape(1, 1, 3 * D_MODEL)
        mod_s = mod[i, 1:1 + bs].reshape(bs, 1, 3 * D_MODEL)
        if i % 2 == 0:
            s5 = _s5_prepare(s5_a_re[j], s5_a_im[j], s5_log_dt[j], s5_b_re[j], s5_b_im[j], s5_c_re[j], s5_c_im[j])
            w_in = ev_w_in[j].astype(BF16)
            w_out = ev_w_out[j].astype(BF16)
            glu_w = s5_glu_w[j].astype(BF16)
            fargs = (hy_f_w1[j], hy_f_b1[j], hy_f_freq1[j], hy_f_w2[j], hy_f_b2[j], hy_f_freq2[j], hy_f_w3[j])
            ks_p = _hy_spectra(_hy_filters(lp, *fargs), dft_p[0])
            ks_s = _hy_spectra(_hy_filters(ls, *fargs), dft_s[0])
            h0_p = jnp.zeros((2, bp, S5_STATE_LANES), F32)
            h0_s = _s5_state_to_lanes(state_s5_re[:, j].astype(F32), state_s5_im[:, j].astype(F32))
            common = (s5_d[j], glu_w, s5_glu_b[j], hy_short_w[j], hy_short_b[j], hy_bias[j])
            xp, hf = _even_layer(xp, mod_p, norm_w[i], w_in, w_out, s5, *common, dft_p, ks_p, h0_p,
                                 final_norm_w, final, s5_tq=16, conv_bb=4, conv_tc=256)
            xs, _ = _even_layer(xs, mod_s, norm_w[i], w_in, w_out, s5, *common, dft_s, ks_s, h0_s,
                                final_norm_w, final, s5_tq=32, conv_bb=2, conv_tc=256)
            re, im = _s5_state_from_lanes(hf)
            new_re.append(re)
            new_im.append(im)
        else:
            pad = LANES - 2 * SSD_HEADS
            w_in = jnp.pad(ssd_w_in[j], ((0, 0), (0, pad))).astype(BF16)
            w_out = ssd_w_out[j].astype(BF16)
            dt_bias = jnp.pad(ssd_dt_bias[j].reshape(1, -1).astype(F32), ((0, 0), (0, pad)))
            a_log = jnp.pad(ssd_a_log[j].reshape(1, -1).astype(F32), ((0, 0), (0, pad)))
            d_exp = jnp.repeat(ssd_d[j].astype(F32), SSD_HEADDIM).reshape(1, D_INNER)
            common = (ssd_conv_w[j], ssd_conv_b[j], a_log, dt_bias, d_exp, ssd_norm_w[j], w_out)
            h0_p = jnp.zeros((bp, 2, SSD_STATE, D_INNER), F32)
            h0_s = _ssd_state_to_lanes(state_ssd[:, j])
            xp, hf = _odd_layer(xp, mod_p, norm_w[i], w_in, *common, h0_p, final_norm_w, final)
            xs, _ = _odd_layer(xs, mod_s, norm_w[i], w_in, *common, h0_s, final_norm_w, final)
            new_ssd.append(_ssd_state_from_lanes(hf))
    return (xp, xs, jnp.stack(new_re, axis=1), jnp.stack(new_im, axis=1), jnp.stack(new_ssd, axis=1))
```

```python
import functools
import math

import jax
import jax.numpy as jnp
from jax import lax
from jax.experimental import pallas as pl
from jax.experimental.pallas import tpu as pltpu

F32 = jnp.float32
BF16 = jnp.bfloat16
HIGHEST = lax.Precision.HIGHEST

D_MODEL = 1024
DEPTH = 4
D_INNER = 2048
EPS = 1e-6
S5_WIDTH = 1024
S5_GROUP = 16
S5_GROUPS = 64
S5_STATE = 64
HY_WIDTH = 1024
HY_BANDS = 16
HY_HID = 64
HY_DECAY_TARGET = 1e-2
HY_FAST = 0.3
HY_SLOW = 1.5
SSD_HEADDIM = 64
SSD_HEADS = 32
SSD_STATE = 128
SSD_GROUPS = 4
SSD_CHUNK = 128
SSD_CONV_DIM = D_INNER + 2 * SSD_GROUPS * SSD_STATE

LANES = 128
SUBLANES = 8
VMEM_LIMIT = 56 * 1024 * 1024

S5_LANE_CHUNKS = S5_WIDTH // LANES
S5_CHUNK_STATE = (LANES // S5_GROUP) * S5_STATE
S5_STATE_LANES = S5_LANE_CHUNKS * 2 * S5_CHUNK_STATE


def _params(*sem):
    return pltpu.CompilerParams(dimension_semantics=sem, vmem_limit_bytes=VMEM_LIMIT)


def _silu(x):
    return x * jax.nn.sigmoid(x)


def _dot(a, b):
    return jnp.dot(a, b, preferred_element_type=F32)


def _dot_hi(a, b):
    return jnp.dot(a, b, preferred_element_type=F32, precision=HIGHEST)


def _mod_kernel(c_ref, w_ref, b_ref, o_ref):
    s = _silu(c_ref[...])
    o_ref[0] = _dot_hi(s, w_ref[0]) + b_ref[0]


def _modulation(cvec, mod_w, mod_b):
    rows = cvec.shape[0]
    tn = D_MODEL
    return pl.pallas_call(
        _mod_kernel,
        grid=(DEPTH, 3 * D_MODEL // tn),
        in_specs=[
            pl.BlockSpec((rows, D_MODEL), lambda i, n: (0, 0)),
            pl.BlockSpec((1, D_MODEL, tn), lambda i, n: (i, 0, n)),
            pl.BlockSpec((1, 1, tn), lambda i, n: (i, 0, n)),
        ],
        out_specs=pl.BlockSpec((1, rows, tn), lambda i, n: (i, 0, n)),
        out_shape=jax.ShapeDtypeStruct((DEPTH, rows, 3 * D_MODEL), F32),
        compiler_params=_params("arbitrary", "arbitrary"),
        name="modulation",
    )(cvec, mod_w, mod_b.reshape(DEPTH, 1, 3 * D_MODEL))


def _norm_mod(x, mod, nw):
    shift = mod[:, :D_MODEL]
    scale = mod[:, D_MODEL:2 * D_MODEL]
    ms = jnp.mean(x * x, axis=-1, keepdims=True)
    return (x * lax.rsqrt(ms + EPS) * nw * (1.0 + scale) + shift).astype(BF16)


def _conv3(x, prev_row, next_row, w, b):
    rows = x.shape[0]
    ridx = lax.broadcasted_iota(jnp.int32, x.shape, 0)
    xp = jnp.where(ridx == 0, prev_row, pltpu.roll(x, 1, 0))
    xn = jnp.where(ridx == rows - 1, next_row, pltpu.roll(x, rows - 1, 0))
    return w[0:1] * xp + w[1:2] * x + w[2:3] * xn + b


def _proj_conv3(hb, pv_ref, nx_ref, mod, nw, w, cw, cb):
    m = pl.program_id(1)
    nm = pl.num_programs(1)
    prev = _dot(_norm_mod(pv_ref[0], mod, nw), w)[SUBLANES - 1:SUBLANES] * (m > 0).astype(F32)
    nxt = _dot(_norm_mod(nx_ref[0], mod, nw), w)[0:1] * (m < nm - 1).astype(F32)
    return _conv3(_dot(hb, w), prev, nxt, cw, cb)


def _inproj_even_kernel(x_ref, pv_ref, nx_ref, mod_ref, nw_ref, w_ref, cw_ref, cb_ref,
                        u_ref, ga_ref, v_ref, x1_ref, x2_ref):
    mod, nw = mod_ref[0], nw_ref[...]
    hb = _norm_mod(x_ref[0], mod, nw)
    lo_vx, hi_vx = 2 * S5_WIDTH, 2 * S5_WIDTH + 3 * HY_WIDTH
    u_ref[0] = _dot(hb, w_ref[:, :S5_WIDTH])
    ga_ref[0] = _dot(hb, w_ref[:, S5_WIDTH:lo_vx])
    y = _proj_conv3(hb, pv_ref, nx_ref, mod, nw, w_ref[:, lo_vx:hi_vx], cw_ref[...], cb_ref[...])
    v_ref[0] = y[:, :HY_WIDTH]
    x1_ref[0] = y[:, HY_WIDTH:2 * HY_WIDTH]
    x2_ref[0] = y[:, 2 * HY_WIDTH:] * _silu(_dot(hb, w_ref[:, hi_vx:]))


def _softplus(x):
    return jnp.maximum(x, 0.0) + jnp.log(1.0 + jnp.exp(-jnp.abs(x)))


def _inproj_odd_kernel(x_ref, pv_ref, nx_ref, mod_ref, nw_ref, w_ref, cw_ref, cb_ref, bias_ref, alog_ref,
                       z_ref, xbc_ref, dt_ref, cum_ref, *, tm):
    mod, nw = mod_ref[0], nw_ref[...]
    hb = _norm_mod(x_ref[0], mod, nw)
    lo_dt = D_INNER + SSD_CONV_DIM
    z_ref[0] = _dot(hb, w_ref[:, :D_INNER])
    xbc_ref[0] = _silu(_proj_conv3(hb, pv_ref, nx_ref, mod, nw, w_ref[:, D_INNER:lo_dt], cw_ref[...], cb_ref[...]))
    dt = _softplus(_dot(hb, w_ref[:, lo_dt:]) + bias_ref[...])
    dt_ref[0] = dt
    da = dt * (-jnp.exp(alog_ref[...]))
    q = SSD_CHUNK
    ii = lax.broadcasted_iota(jnp.int32, (q, q), 0)
    jj = lax.broadcasted_iota(jnp.int32, (q, q), 1)
    low, up = (jj <= ii).astype(F32), (jj >= ii).astype(F32)
    fwd_lane = lax.broadcasted_iota(jnp.int32, (q, LANES), 1) < SSD_HEADS
    for c in range(tm // q):
        dac = da[c * q:(c + 1) * q]
        cum_ref[0, c * q:(c + 1) * q, :] = jnp.where(fwd_lane, _dot_hi(low, dac), _dot_hi(up, dac))


def _mod_spec(mod):
    per_batch = mod.shape[0] > 1
    return pl.BlockSpec((1, 1, mod.shape[-1]), (lambda b, m: (b, 0, 0)) if per_batch else (lambda b, m: (0, 0, 0)))


def _halo_specs(width, tl, seq):
    r = tl // SUBLANES
    last = seq // SUBLANES - 1
    return (pl.BlockSpec((1, SUBLANES, width), lambda b, m: (b, jnp.maximum(m * r - 1, 0), 0)),
            pl.BlockSpec((1, SUBLANES, width), lambda b, m: (b, jnp.minimum((m + 1) * r, last), 0)))


def _inproj(kernel_fn, x, mod, norm_w, w_bf16, row_params, widths, tm=256):
    bsz, seq, _ = x.shape
    pv, nx = _halo_specs(D_MODEL, tm, seq)
    const = lambda a: pl.BlockSpec(a.shape, lambda b, m: (0, 0))
    return pl.pallas_call(
        kernel_fn,
        grid=(bsz, seq // tm),
        in_specs=[pl.BlockSpec((1, tm, D_MODEL), lambda b, m: (b, m, 0)), pv, nx, _mod_spec(mod),
                  pl.BlockSpec((1, D_MODEL), lambda b, m: (0, 0)), const(w_bf16)] + [const(a) for a in row_params],
        out_specs=[pl.BlockSpec((1, tm, w), lambda b, m: (b, m, 0)) for w in widths],
        out_shape=[jax.ShapeDtypeStruct((bsz, seq, w), F32) for w in widths],
        compiler_params=_params("arbitrary", "arbitrary"),
        name="inproj",
    )(x, x, x, mod, norm_w.reshape(1, D_MODEL), w_bf16, *row_params)


def _s5_kernel(u_ref, pin_ref, pout_ref, bw_ref, cw_ref, lr_ref, li_ref, h0_ref, ys_ref, hf_ref, bu_ref, hst_ref,
               *, bsz, tq, jgroup):
    d = pl.program_id(0)
    blk = pl.program_id(1)
    nblk = pl.num_programs(1)
    cs = S5_CHUNK_STATE

    @pl.when(blk == 0)
    def _():
        hst_ref[...] = h0_ref[0]

    nsub = cs // LANES
    u2 = u_ref[...].reshape(bsz * tq, S5_WIDTH).astype(BF16)
    u_tm = _dot(pin_ref[...], u2).astype(BF16)
    for j in range(S5_LANE_CHUNKS):
        bu_ref[:, 2 * cs * j:2 * cs * (j + 1)] = _dot(u_tm[:, LANES * j:LANES * (j + 1)], bw_ref[0, j])

    for j0 in range(0, S5_LANE_CHUNKS, jgroup):
        cols = [(2 * cs * j + LANES * c, 2 * cs * j + cs + LANES * c, j, c)
                for j in range(j0, j0 + jgroup) for c in range(nsub)]
        lam = [(jnp.broadcast_to(lr_ref[0, j, :, LANES * c:LANES * (c + 1)], (bsz, LANES)),
                jnp.broadcast_to(li_ref[0, j, :, LANES * c:LANES * (c + 1)], (bsz, LANES))) for _, _, j, c in cols]

        def body(s, carry, cols=cols, lam=lam):
            t = jnp.where(d == 0, s, tq - 1 - s)
            rows = pl.ds(pl.multiple_of(t * bsz, bsz), bsz)
            out = []
            for i, (re_lo, im_lo, _, _) in enumerate(cols):
                hr, hi = carry[2 * i], carry[2 * i + 1]
                lam_re, lam_im = lam[i]
                nr = lam_re * hr - lam_im * hi + bu_ref[rows, re_lo:re_lo + LANES]
                ni = lam_re * hi + lam_im * hr + bu_ref[rows, im_lo:im_lo + LANES]
                bu_ref[rows, re_lo:re_lo + LANES] = nr
                bu_ref[rows, im_lo:im_lo + LANES] = ni
                out += [nr, ni]
            return tuple(out)

        init = []
        for re_lo, im_lo, _, _ in cols:
            init += [hst_ref[:, re_lo:re_lo + LANES], hst_ref[:, im_lo:im_lo + LANES]]
        fin = lax.fori_loop(0, tq, body, tuple(init), unroll=2)
        for i, (re_lo, im_lo, _, _) in enumerate(cols):
            hst_ref[:, re_lo:re_lo + LANES] = fin[2 * i]
            hst_ref[:, im_lo:im_lo + LANES] = fin[2 * i + 1]

    y_tm = jnp.concatenate([_dot(bu_ref[:, 2 * cs * j:2 * cs * (j + 1)].astype(BF16), cw_ref[0, j])
                            for j in range(S5_LANE_CHUNKS)], axis=-1)
    y_hi = y_tm.astype(BF16)
    y_lo = (y_tm - y_hi.astype(F32)).astype(BF16)
    y_bm = _dot(pout_ref[...], y_hi) + _dot(pout_ref[...], y_lo)
    ys_ref[0] = y_bm.reshape(bsz, tq, S5_WIDTH)

    @pl.when(blk == nblk - 1)
    def _():
        hf_ref[0] = hst_ref[...]


def _s5_scan(u, bw, cw, lr, li, h0, tq):
    bsz, seq, _ = u.shape
    nblk = seq // tq
    rows = bsz * tq
    r = jnp.arange(rows, dtype=jnp.int32)
    src = (r % bsz) * tq + r // bsz
    p_in = (src[:, None] == r[None, :]).astype(BF16)
    p_out = p_in.T

    def tblk(d, i):
        return i + d * (nblk - 1 - 2 * i)

    nj, cs = S5_LANE_CHUNKS, S5_CHUNK_STATE
    jgroup = 2 if bsz <= SUBLANES else 1
    return pl.pallas_call(
        functools.partial(_s5_kernel, bsz=bsz, tq=tq, jgroup=jgroup),
        grid=(2, nblk),
        in_specs=[
            pl.BlockSpec((bsz, tq, S5_WIDTH), lambda d, i: (0, tblk(d, i), 0)),
            pl.BlockSpec((rows, rows), lambda d, i: (0, 0)),
            pl.BlockSpec((rows, rows), lambda d, i: (0, 0)),
            pl.BlockSpec((1, nj, LANES, 2 * cs), lambda d, i: (d, 0, 0, 0)),
            pl.BlockSpec((1, nj, 2 * cs, LANES), lambda d, i: (d, 0, 0, 0)),
            pl.BlockSpec((1, nj, 1, cs), lambda d, i: (d, 0, 0, 0)),
            pl.BlockSpec((1, nj, 1, cs), lambda d, i: (d, 0, 0, 0)),
            pl.BlockSpec((1, bsz, S5_STATE_LANES), lambda d, i: (d, 0, 0)),
        ],
        out_specs=[
            pl.BlockSpec((1, bsz, tq, S5_WIDTH), lambda d, i: (d, 0, tblk(d, i), 0)),
            pl.BlockSpec((1, bsz, S5_STATE_LANES), lambda d, i: (d, 0, 0)),
        ],
        out_shape=[
            jax.ShapeDtypeStruct((2, bsz, seq, S5_WIDTH), F32),
            jax.ShapeDtypeStruct((2, bsz, S5_STATE_LANES), F32),
        ],
        scratch_shapes=[
            pltpu.VMEM((rows, S5_STATE_LANES), F32),
            pltpu.VMEM((bsz, S5_STATE_LANES), F32),
        ],
        compiler_params=_params("arbitrary", "arbitrary"),
        name="s5_scan",
    )(u, p_in, p_out, bw, cw, lr, li, h0)


def _s5_prepare(a_re, a_im, log_dt, b_re, b_im, c_re, c_im):
    dt = jnp.exp(log_dt.astype(F32))[..., None]
    a_re, a_im = a_re.astype(F32), a_im.astype(F32)
    mag = jnp.exp(a_re * dt)
    lbr, lbi = mag * jnp.cos(a_im * dt), mag * jnp.sin(a_im * dt)
    den = a_re * a_re + a_im * a_im
    cr = ((lbr - 1.0) * a_re + lbi * a_im) / den
    ci = (lbi * a_re - (lbr - 1.0) * a_im) / den
    bbr = cr[..., None] * b_re - ci[..., None] * b_im
    bbi = cr[..., None] * b_im + ci[..., None] * b_re
    nj, gpc = S5_LANE_CHUNKS, LANES // S5_GROUP
    eye = jnp.eye(gpc, dtype=F32)

    def pack_b(m):
        m = m.reshape(2, nj, gpc, S5_STATE, S5_GROUP)
        return jnp.einsum('djgph,gk->djghkp', m, eye).reshape(2, nj, LANES, S5_CHUNK_STATE)

    def pack_c(m):
        m = m.reshape(2, nj, gpc, S5_GROUP, S5_STATE)
        return jnp.einsum('djghp,gk->djgpkh', m, eye).reshape(2, nj, S5_CHUNK_STATE, LANES)

    bw = jnp.concatenate([pack_b(bbr), pack_b(bbi)], axis=-1).astype(BF16)
    cw = jnp.concatenate([pack_c(c_re.astype(F32)), pack_c(-c_im.astype(F32))], axis=-2).astype(BF16)
    lr = lbr.reshape(2, nj, 1, S5_CHUNK_STATE)
    li = lbi.reshape(2, nj, 1, S5_CHUNK_STATE)
    return bw, cw, lr, li


def _s5_state_to_lanes(re, im):
    bsz = re.shape[0]
    st = jnp.stack([re.reshape(bsz, 2, S5_LANE_CHUNKS, S5_CHUNK_STATE),
                    im.reshape(bsz, 2, S5_LANE_CHUNKS, S5_CHUNK_STATE)], axis=3)
    return jnp.transpose(st.reshape(bsz, 2, S5_STATE_LANES), (1, 0, 2))


def _s5_state_from_lanes(hf):
    bsz = hf.shape[1]
    st = jnp.transpose(hf, (1, 0, 2)).reshape(bsz, 2, S5_LANE_CHUNKS, 2, S5_CHUNK_STATE)
    re = st[:, :, :, 0].reshape(bsz, 2, S5_GROUPS, S5_STATE)
    im = st[:, :, :, 1].reshape(bsz, 2, S5_GROUPS, S5_STATE)
    return re, im


def _hy_filter_kernel(bands_ref, deltas_ref, w1t_ref, w1c_ref, w1s_ref, b1_ref, f1_ref, w2_ref, b2_ref, f2_ref,
                      w3_ref, o_ref, *, seq, tl):
    m = pl.program_id(0)
    t = (lax.broadcasted_iota(jnp.int32, (tl, 1), 0) + m * tl).astype(F32)
    t_norm = t / (seq - 1)
    ang = (2.0 * math.pi / seq) * t * bands_ref[...]
    pre = t_norm * w1t_ref[...] + _dot_hi(jnp.cos(ang), w1c_ref[...]) + _dot_hi(jnp.sin(ang), w1s_ref[...])
    f = jnp.sin(f1_ref[...] * (pre + b1_ref[...]))
    f = jnp.sin(f2_ref[...] * (_dot_hi(f, w2_ref[...]) + b2_ref[...]))
    f = _dot_hi(f, w3_ref[...])
    window = jnp.exp(-t_norm * deltas_ref[...])
    not_lag0 = (t > 0.0).astype(F32)
    for o in range(2):
        h_fwd = f[:, (2 * o) * HY_WIDTH:(2 * o + 1) * HY_WIDTH] * window
        h_bwd = f[:, (2 * o + 1) * HY_WIDTH:(2 * o + 2) * HY_WIDTH] * window * not_lag0
        o_ref[o, 0] = (h_fwd + h_bwd).astype(BF16)
        o_ref[o, 1] = (h_fwd - h_bwd).astype(BF16)


def _hy_filters(seq, w1, b1, freq1, w2, b2, freq2, w3, tl=256):
    bands = jnp.linspace(1e-4, HY_BANDS - 1, HY_BANDS, dtype=F32).reshape(1, HY_BANDS)
    deltas = jnp.abs(jnp.linspace(math.log(HY_DECAY_TARGET) / HY_SLOW, math.log(HY_DECAY_TARGET) / HY_FAST,
                                  HY_WIDTH, dtype=F32)).reshape(1, HY_WIDTH)
    small = [bands, deltas, w1[0:1], w1[1:1 + HY_BANDS], w1[1 + HY_BANDS:], b1.reshape(1, -1), freq1.reshape(1, -1),
             w2, b2.reshape(1, -1), freq2.reshape(1, -1), w3]
    return pl.pallas_call(
        functools.partial(_hy_filter_kernel, seq=seq, tl=tl),
        grid=(seq // tl,),
        in_specs=[pl.BlockSpec(a.shape, lambda m: (0, 0)) for a in small],
        out_specs=pl.BlockSpec((2, 2, tl, HY_WIDTH), lambda m: (0, 0, m, 0)),
        out_shape=jax.ShapeDtypeStruct((2, 2, seq, HY_WIDTH), BF16),
        compiler_params=_params("arbitrary"),
        name="hy_filters",
    )(*small)


def _dft_block(seq):
    return min(256, seq)


def _dft_matrices(seq):
    n = 2 * seq
    kb = _dft_block(seq)
    nb = seq // kb
    k = jnp.arange(seq, dtype=jnp.int32)
    ang = ((k[:, None] * k[None, :]) % n).astype(F32) * (2.0 * math.pi / n)
    cos, sin = jnp.cos(ang), jnp.sin(ang)
    alt = jnp.where(k % 2 == 0, 1.0, -1.0).astype(F32)
    re_rows = cos
    im_rows = jnp.where(k[:, None] == 0, alt[None, :], -sin)
    fwd = jnp.concatenate([re_rows.reshape(nb, kb, seq), im_rows.reshape(nb, kb, seq)], axis=1).reshape(n, seq)
    scale = jnp.where(k == 0, 1.0 / n, 2.0 / n).astype(F32)
    re_cols = cos * scale[None, :]
    im_cols = jnp.where(k[None, :] == 0, alt[:, None], -sin) * scale[None, :]
    inv = jnp.concatenate([re_cols.reshape(seq, nb, kb), im_cols.reshape(seq, nb, kb)], axis=2).reshape(seq, n)
    return fwd.astype(BF16), inv.astype(BF16)


def _hy_spec_kernel(fwd_ref, h_ref, o_ref, *, kb):
    k = pl.program_id(2)
    f = fwd_ref[...]
    s_sum = _dot(f, h_ref[0, 0])
    s_dif = _dot(f, h_ref[0, 1])
    ridx = lax.broadcasted_iota(jnp.int32, s_sum.shape, 0)
    real_row = (ridx < kb) | ((ridx == kb) & (k == 0))
    o_ref[0] = jnp.where(real_row, s_sum, s_dif)


def _hy_spectra(filt, fwd, tc=512):
    _, _, seq, width = filt.shape
    kb = _dft_block(seq)
    nkb = seq // kb
    return pl.pallas_call(
        functools.partial(_hy_spec_kernel, kb=kb),
        grid=(2, width // tc, nkb),
        in_specs=[
            pl.BlockSpec((2 * kb, seq), lambda o, c, k: (k, 0)),
            pl.BlockSpec((1, 2, seq, tc), lambda o, c, k: (o, 0, 0, c)),
        ],
        out_specs=pl.BlockSpec((1, 2 * kb, tc), lambda o, c, k: (o, k, c)),
        out_shape=jax.ShapeDtypeStruct((2, 2 * seq, width), F32),
        compiler_params=_params("arbitrary", "arbitrary", "arbitrary"),
        name="hy_spectra",
    )(fwd, filt)


def _hy_conv_kernel(v_ref, g_ref, bias_ref, fwd_ref, inv_ref, ks_ref, o_ref, vb_ref, acc_ref, *, kb, bb):
    k = pl.program_id(2)
    nk = pl.num_programs(2)

    @pl.when(k == 0)
    def _():
        vb_ref[...] = v_ref[...].astype(BF16)
        acc_ref[...] = jnp.zeros_like(acc_ref)

    f = fwd_ref[...]
    ks = ks_ref[0]
    k_re, k_im = ks[:kb], ks[kb:]
    ridx = lax.broadcasted_iota(jnp.int32, k_re.shape, 0)
    dc_row = (ridx == 0) & (k == 0)
    for i in range(bb):
        s = _dot(f, vb_ref[i])
        s_re, s_im = s[:kb], s[kb:]
        p_re = jnp.where(dc_row, s_re * k_re, s_re * k_re - s_im * k_im)
        p_im = jnp.where(dc_row, s_im * k_im, s_re * k_im + s_im * k_re)
        p = jnp.concatenate([p_re, p_im], axis=0).astype(BF16)
        acc_ref[i] += _dot(inv_ref[...], p)

    @pl.when(k == nk - 1)
    def _():
        o_ref[...] = g_ref[...] * (acc_ref[...] + v_ref[...] * bias_ref[...])


def _hy_conv(v, gate, bias, fwd, inv, kspec, order, bb, tc):
    bsz, seq, width = v.shape
    kb = _dft_block(seq)
    nkb = seq // kb
    blk = pl.BlockSpec((bb, seq, tc), lambda c, b, k: (b, 0, c))
    return pl.pallas_call(
        functools.partial(_hy_conv_kernel, kb=kb, bb=bb),
        grid=(width // tc, bsz // bb, nkb),
        in_specs=[
            blk, blk,
            pl.BlockSpec((1, tc), lambda c, b, k: (0, c)),
            pl.BlockSpec((2 * kb, seq), lambda c, b, k: (k, 0)),
            pl.BlockSpec((seq, 2 * kb), lambda c, b, k: (0, k)),
            pl.BlockSpec((1, 2 * kb, tc), lambda c, b, k: (order, k, c)),
        ],
        out_specs=blk,
        out_shape=jax.ShapeDtypeStruct((bsz, seq, width), F32),
        scratch_shapes=[pltpu.VMEM((bb, seq, tc), BF16), pltpu.VMEM((bb, seq, tc), F32)],
        compiler_params=_params("arbitrary", "arbitrary", "arbitrary"),
        name="hy_conv",
    )(v, gate, bias.reshape(1, width), fwd, inv, kspec)


def _gelu_tanh(x):
    return 0.5 * x * (1.0 + jnp.tanh(math.sqrt(2.0 / math.pi) * (x + 0.044715 * (x * x * x))))


def _rms(x, w):
    return x * lax.rsqrt(jnp.mean(x * x, axis=-1, keepdims=True) + EPS) * w


def _even_tail_kernel(x_ref, mod_ref, yf_ref, yb_ref, u_ref, ga_ref, hy_ref, d_ref, gw_ref, gb_ref, wo_ref,
                      fw_ref, o_ref, *, final):
    y = yf_ref[0, 0] + yb_ref[0, 0] + u_ref[0] * d_ref[...]
    y = _gelu_tanh(y)
    y = y * jax.nn.sigmoid(_dot(y.astype(BF16), gw_ref[...]) + gb_ref[...])
    ya = (y * _silu(ga_ref[0])).astype(BF16)
    out = _dot(ya, wo_ref[:S5_WIDTH, :]) + _dot(hy_ref[0].astype(BF16), wo_ref[S5_WIDTH:, :])
    xn = x_ref[0] + mod_ref[0][:, 2 * D_MODEL:] * out
    o_ref[0] = _rms(xn, fw_ref[...]) if final else xn


def _even_tail(x, mod, ys, u, g_a, hy, s5_d, glu_w, glu_b, w_out, final_w, final, tm=256):
    bsz, seq, _ = x.shape
    tok = lambda w: pl.BlockSpec((1, tm, w), lambda b, m: (b, m, 0))
    const = lambda r, c: pl.BlockSpec((r, c), lambda b, m: (0, 0))
    return pl.pallas_call(
        functools.partial(_even_tail_kernel, final=final),
        grid=(bsz, seq // tm),
        in_specs=[
            tok(D_MODEL), _mod_spec(mod),
            pl.BlockSpec((1, 1, tm, S5_WIDTH), lambda b, m: (0, b, m, 0)),
            pl.BlockSpec((1, 1, tm, S5_WIDTH), lambda b, m: (1, b, m, 0)),
            tok(S5_WIDTH), tok(S5_WIDTH), tok(HY_WIDTH),
            const(1, S5_WIDTH), const(S5_WIDTH, S5_WIDTH), const(1, S5_WIDTH), const(D_INNER, D_MODEL),
            const(1, D_MODEL),
        ],
        out_specs=tok(D_MODEL),
        out_shape=jax.ShapeDtypeStruct((bsz, seq, D_MODEL), F32),
        compiler_params=_params("arbitrary", "arbitrary"),
        name="even_tail",
    )(x, mod, ys, ys, u, g_a, hy, s5_d.reshape(1, -1), glu_w, glu_b.reshape(1, -1), w_out, final_w.reshape(1, -1))


def _ssd_direction(xbc, dt, cum, st_ref, direction, d_row):
    q = SSD_CHUNK
    hp = LANES // SSD_HEADDIM
    ii = lax.broadcasted_iota(jnp.int32, (q, q), 0)
    jj = lax.broadcasted_iota(jnp.int32, (q, q), 1)
    if direction == 0:
        tri = (jj <= ii)
        last = q - 1
    else:
        tri = (jj >= ii)
        last = 0
    cum_t = cum.T
    src_t = cum_t - jnp.log(dt.T)
    w_t = jnp.exp(cum_t[:, last:last + 1] - src_t)
    last_row = cum[last:last + 1, :]
    first_half = lax.broadcasted_iota(jnp.int32, (q, LANES), 1) < SSD_HEADDIM
    first_half_row = lax.broadcasted_iota(jnp.int32, (1, LANES), 1) < SSD_HEADDIM
    x_all = xbc[:, :D_INNER]
    ys = []
    for g in range(SSD_GROUPS):
        b_g = xbc[:, D_INNER + g * SSD_STATE:D_INNER + (g + 1) * SSD_STATE]
        c_g = xbc[:, D_INNER + (SSD_GROUPS + g) * SSD_STATE:D_INNER + (SSD_GROUPS + g + 1) * SSD_STATE]
        b_bf, c_bf = b_g.astype(BF16), c_g.astype(BF16)
        cb = lax.dot_general(c_bf, b_bf, (((1,), (1,)), ((), ())), preferred_element_type=F32)
        b_t = b_g.T
        heads_per_group = SSD_HEADS // SSD_GROUPS
        for pair in range(heads_per_group // hp):
            h0 = g * heads_per_group + pair * hp
            lo = h0 * SSD_HEADDIM
            x_pair = x_all[:, lo:lo + LANES]
            x_blk = jnp.concatenate([jnp.where(first_half, x_pair, 0.0).astype(BF16),
                                     jnp.where(first_half, 0.0, x_pair).astype(BF16)], axis=0)
            scores, b_w, cols, tot = [], [], [], []
            for h in (h0, h0 + 1):
                hl = direction * SSD_HEADS + h
                col = jnp.broadcast_to(cum[:, hl:hl + 1], (q, q))
                arg = jnp.where(tri, col - src_t[hl:hl + 1, :], -jnp.inf)
                scores.append((cb * jnp.exp(arg)).astype(BF16))
                b_w.append((b_t * w_t[hl:hl + 1, :]).astype(BF16))
                cols.append(col)
                tot.append(jnp.broadcast_to(last_row[:, hl:hl + 1], (1, LANES)))
            s_pair = st_ref[:, lo:lo + LANES]
            y = _dot(jnp.concatenate(scores, axis=1), x_blk)
            y = y + _dot(c_bf, s_pair.astype(BF16)) * jnp.exp(jnp.where(first_half, cols[0], cols[1]))
            if d_row is not None:
                y = y + x_pair * d_row[:, lo:lo + LANES]
            ys.append(y)
            decay = jnp.exp(jnp.where(first_half_row, tot[0], tot[1]))
            st_ref[:, lo:lo + LANES] = s_pair * decay + _dot(jnp.concatenate(b_w, axis=1), x_blk)
    return jnp.concatenate(ys, axis=-1)


def _ssd_kernel(xf_ref, xb_ref, dtf_ref, dtb_ref, cumf_ref, cumb_ref, d_ref, h0_ref, yf_ref, yb_ref, hf_ref, st_ref):
    c = pl.program_id(1)
    nc = pl.num_programs(1)

    @pl.when(c == 0)
    def _():
        st_ref[...] = h0_ref[0]

    yf_ref[0] = _ssd_direction(xf_ref[0], dtf_ref[0], cumf_ref[0], st_ref.at[0], 0, d_ref[...])
    yb_ref[0] = _ssd_direction(xb_ref[0], dtb_ref[0], cumb_ref[0], st_ref.at[1], 1, None)

    @pl.when(c == nc - 1)
    def _():
        hf_ref[0] = st_ref[...]


def _ssd_scan(xbc, dt, cum, d_exp, h0):
    bsz, seq, width = xbc.shape
    q = SSD_CHUNK
    nc = seq // q
    fwd = lambda w: pl.BlockSpec((1, q, w), lambda b, c: (b, c, 0))
    bwd = lambda w: pl.BlockSpec((1, q, w), lambda b, c: (b, nc - 1 - c, 0))
    st_spec = pl.BlockSpec((1, 2, SSD_STATE, D_INNER), lambda b, c: (b, 0, 0, 0))
    return pl.pallas_call(
        _ssd_kernel,
        grid=(bsz, nc),
        in_specs=[fwd(width), bwd(width), fwd(LANES), bwd(LANES), fwd(LANES), bwd(LANES),
                  pl.BlockSpec((1, D_INNER), lambda b, c: (0, 0)), st_spec],
        out_specs=[fwd(D_INNER), bwd(D_INNER), st_spec],
        out_shape=[
            jax.ShapeDtypeStruct((bsz, seq, D_INNER), F32),
            jax.ShapeDtypeStruct((bsz, seq, D_INNER), F32),
            jax.ShapeDtypeStruct((bsz, 2, SSD_STATE, D_INNER), F32),
        ],
        scratch_shapes=[pltpu.VMEM((2, SSD_STATE, D_INNER), F32)],
        compiler_params=_params("arbitrary", "arbitrary"),
        name="ssd_scan",
    )(xbc, xbc, dt, dt, cum, cum, d_exp, h0)


def _odd_tail_kernel(x_ref, mod_ref, yf_ref, yb_ref, z_ref, nw_ref, wo_ref, fw_ref, o_ref, *, final):
    y = (yf_ref[0] + yb_ref[0]) * _silu(z_ref[0])
    y = _rms(y, nw_ref[...]).astype(BF16)
    xn = x_ref[0] + mod_ref[0][:, 2 * D_MODEL:] * _dot(y, wo_ref[...])
    o_ref[0] = _rms(xn, fw_ref[...]) if final else xn


def _odd_tail(x, mod, yf, yb, z, norm_w, w_out, final_w, final, tm=256):
    bsz, seq, _ = x.shape
    tok = lambda w: pl.BlockSpec((1, tm, w), lambda b, m: (b, m, 0))
    const = lambda r, c: pl.BlockSpec((r, c), lambda b, m: (0, 0))
    return pl.pallas_call(
        functools.partial(_odd_tail_kernel, final=final),
        grid=(bsz, seq // tm),
        in_specs=[tok(D_MODEL), _mod_spec(mod), tok(D_INNER), tok(D_INNER), tok(D_INNER),
                  const(1, D_INNER), const(D_INNER, D_MODEL), const(1, D_MODEL)],
        out_specs=tok(D_MODEL),
        out_shape=jax.ShapeDtypeStruct((bsz, seq, D_MODEL), F32),
        compiler_params=_params("arbitrary", "arbitrary"),
        name="odd_tail",
    )(x, mod, yf, yb, z, norm_w.reshape(1, -1), w_out, final_w.reshape(1, -1))


def _even_layer(x, mod, norm_w, w_in, w_out, s5, s5_d, glu_w, glu_b, short_w, short_b, hy_bias, dft, kspec, h0,
                final_w, final, s5_tq, conv_bb, conv_tc):
    u, g_a, v, x1, x2g = _inproj(_inproj_even_kernel, x, mod, norm_w, w_in,
                                 (short_w, short_b.reshape(1, -1)), (S5_WIDTH, S5_WIDTH) + (HY_WIDTH,) * 3)
    ys, hf = _s5_scan(u, *s5, h0, s5_tq)
    fwd, inv = dft
    z1 = _hy_conv(v, x1, hy_bias[0], fwd, inv, kspec, 0, conv_bb, conv_tc)
    hy = _hy_conv(z1, x2g, hy_bias[1], fwd, inv, kspec, 1, conv_bb, conv_tc)
    xn = _even_tail(x, mod, ys, u, g_a, hy, s5_d, glu_w, glu_b, w_out, final_w, final)
    return xn, hf


def _odd_layer(x, mod, norm_w, w_in, conv_w, conv_b, a_log, dt_bias, d_exp, ssd_norm_w, w_out, h0, final_w, final):
    tm = 256
    z, xbc, dt, cum = _inproj(functools.partial(_inproj_odd_kernel, tm=tm), x, mod, norm_w, w_in,
                              (conv_w, conv_b.reshape(1, -1), dt_bias, a_log),
                              (D_INNER, SSD_CONV_DIM, LANES, LANES), tm=tm)
    yf, yb, hf = _ssd_scan(xbc, dt, cum, d_exp, h0)
    xn = _odd_tail(x, mod, yf, yb, z, ssd_norm_w, w_out, final_w, final)
    return xn, hf


def _ssd_state_to_lanes(st):
    bsz = st.shape[0]
    return jnp.transpose(st.astype(F32), (0, 1, 4, 2, 3)).reshape(bsz, 2, SSD_STATE, D_INNER)


def _ssd_state_from_lanes(st):
    bsz = st.shape[0]
    return jnp.transpose(st.reshape(bsz, 2, SSD_STATE, SSD_HEADS, SSD_HEADDIM), (0, 1, 3, 4, 2))


def kernel(x_prompt, x_sample, state_s5_re, state_s5_im, state_ssd, c, c_ctx, mod_w, mod_b, norm_w, final_norm_w,
           ev_w_in, ev_w_out, s5_a_re, s5_a_im, s5_log_dt, s5_b_re, s5_b_im, s5_c_re, s5_c_im, s5_d, s5_glu_w,
           s5_glu_b, hy_short_w, hy_short_b, hy_f_w1, hy_f_b1, hy_f_freq1, hy_f_w2, hy_f_b2, hy_f_freq2, hy_f_w3,
           hy_bias, ssd_w_in, ssd_conv_w, ssd_conv_b, ssd_a_log, ssd_dt_bias, ssd_d, ssd_norm_w, ssd_w_out):
    bp, lp, _ = x_prompt.shape
    bs, ls, _ = x_sample.shape
    rows = 2 * SUBLANES
    cvec = jnp.zeros((rows, D_MODEL), F32).at[0].set(c_ctx.astype(F32)).at[1:1 + bs].set(c.astype(F32))
    mod = _modulation(cvec, mod_w, mod_b)
    dft_p, dft_s = _dft_matrices(lp), _dft_matrices(ls)
    xp, xs = x_prompt, x_sample
    new_re, new_im, new_ssd = [], [], []
    for i in range(DEPTH):
        j = i // 2
        final = i == DEPTH - 1
        mod_p = mod[i, 0:1].reshape(1, 1, 3 * D_MODEL)
        mod_s = mod[i, 1:1 + bs].reshape(bs, 1, 3 * D_MODEL)
        if i % 2 == 0:
            s5 = _s5_prepare(s5_a_re[j], s5_a_im[j], s5_log_dt[j], s5_b_re[j], s5_b_im[j], s5_c_re[j], s5_c_im[j])
            w_in = ev_w_in[j].astype(BF16)
            w_out = ev_w_out[j].astype(BF16)
            glu_w = s5_glu_w[j].astype(BF16)
            fargs = (hy_f_w1[j], hy_f_b1[j], hy_f_freq1[j], hy_f_w2[j], hy_f_b2[j], hy_f_freq2[j], hy_f_w3[j])
            ks_p = _hy_spectra(_hy_filters(lp, *fargs), dft_p[0])
            ks_s = _hy_spectra(_hy_filters(ls, *fargs), dft_s[0])
            h0_p = jnp.zeros((2, bp, S5_STATE_LANES), F32)
            h0_s = _s5_state_to_lanes(state_s5_re[:, j].astype(F32), state_s5_im[:, j].astype(F32))
            common = (s5_d[j], glu_w, s5_glu_b[j], hy_short_w[j], hy_short_b[j], hy_bias[j])
            xp, hf = _even_layer(xp, mod_p, norm_w[i], w_in, w_out, s5, *common, dft_p, ks_p, h0_p,
                                 final_norm_w, final, s5_tq=16, conv_bb=4, conv_tc=256)
            xs, _ = _even_layer(xs, mod_s, norm_w[i], w_in, w_out, s5, *common, dft_s, ks_s, h0_s,
                                final_norm_w, final, s5_tq=32, conv_bb=2, conv_tc=256)
            re, im = _s5_state_from_lanes(hf)
            new_re.append(re)
            new_im.append(im)
        else:
            pad = LANES - 2 * SSD_HEADS
            w_in = jnp.pad(ssd_w_in[j], ((0, 0), (0, pad))).astype(BF16)
            w_out = ssd_w_out[j].astype(BF16)
            dt_bias = jnp.pad(ssd_dt_bias[j].reshape(1, -1).astype(F32), ((0, 0), (0, pad)))
            a_log = jnp.pad(ssd_a_log[j].reshape(1, -1).astype(F32), ((0, 0), (0, pad)))
            d_exp = jnp.repeat(ssd_d[j].astype(F32), SSD_HEADDIM).reshape(1, D_INNER)
            common = (ssd_conv_w[j], ssd_conv_b[j], a_log, dt_bias, d_exp, ssd_norm_w[j], w_out)
            h0_p = jnp.zeros((bp, 2, SSD_STATE, D_INNER), F32)
            h0_s = _ssd_state_to_lanes(state_ssd[:, j])
            xp, hf = _odd_layer(xp, mod_p, norm_w[i], w_in, *common, h0_p, final_norm_w, final)
            xs, _ = _odd_layer(xs, mod_s, norm_w[i], w_in, *common, h0_s, final_norm_w, final)
            new_ssd.append(_ssd_state_from_lanes(hf))
    return (xp, xs, jnp.stack(new_re, axis=1), jnp.stack(new_im, axis=1), jnp.stack(new_ssd, axis=1))
```

```python
import functools
import math

import jax
import jax.numpy as jnp
from jax import lax
from jax.experimental import pallas as pl
from jax.experimental.pallas import tpu as pltpu

F32 = jnp.float32
BF16 = jnp.bfloat16
HIGHEST = lax.Precision.HIGHEST

D_MODEL = 1024
DEPTH = 4
D_INNER = 2048
EPS = 1e-6
S5_WIDTH = 1024
S5_GROUP = 16
S5_GROUPS = 64
S5_STATE = 64
HY_WIDTH = 1024
HY_BANDS = 16
HY_HID = 64
HY_DECAY_TARGET = 1e-2
HY_FAST = 0.3
HY_SLOW = 1.5
HY_PHASES = 4
SSD_HEADDIM = 64
SSD_HEADS = 32
SSD_STATE = 128
SSD_GROUPS = 4
SSD_CHUNK = 128
SSD_CONV_DIM = D_INNER + 2 * SSD_GROUPS * SSD_STATE

LANES = 128
SUBLANES = 8
VMEM_LIMIT = 56 * 1024 * 1024

S5_LANE_CHUNKS = S5_WIDTH // LANES
S5_CHUNK_STATE = (LANES // S5_GROUP) * S5_STATE
S5_STATE_LANES = S5_LANE_CHUNKS * 2 * S5_CHUNK_STATE


def _params(*sem):
    return pltpu.CompilerParams(dimension_semantics=sem, vmem_limit_bytes=VMEM_LIMIT)


def _silu(x):
    return x * jax.nn.sigmoid(x)


def _dot(a, b):
    return jnp.dot(a, b, preferred_element_type=F32)


def _dot_hi(a, b):
    return jnp.dot(a, b, preferred_element_type=F32, precision=HIGHEST)


def _mod_kernel(c_ref, w_ref, b_ref, o_ref):
    s = _silu(c_ref[...])
    o_ref[0] = _dot_hi(s, w_ref[0]) + b_ref[0]


def _modulation(cvec, mod_w, mod_b):
    rows = cvec.shape[0]
    tn = D_MODEL
    return pl.pallas_call(
        _mod_kernel,
        grid=(DEPTH, 3 * D_MODEL // tn),
        in_specs=[
            pl.BlockSpec((rows, D_MODEL), lambda i, n: (0, 0)),
            pl.BlockSpec((1, D_MODEL, tn), lambda i, n: (i, 0, n)),
            pl.BlockSpec((1, 1, tn), lambda i, n: (i, 0, n)),
        ],
        out_specs=pl.BlockSpec((1, rows, tn), lambda i, n: (i, 0, n)),
        out_shape=jax.ShapeDtypeStruct((DEPTH, rows, 3 * D_MODEL), F32),
        compiler_params=_params("arbitrary", "arbitrary"),
        name="modulation",
    )(cvec, mod_w, mod_b.reshape(DEPTH, 1, 3 * D_MODEL))


def _norm_mod(x, mod, nw):
    shift = mod[:, :D_MODEL]
    scale = mod[:, D_MODEL:2 * D_MODEL]
    ms = jnp.mean(x * x, axis=-1, keepdims=True)
    return (x * lax.rsqrt(ms + EPS) * nw * (1.0 + scale) + shift).astype(BF16)


def _conv3(x, prev_row, next_row, w, b):
    rows = x.shape[0]
    ridx = lax.broadcasted_iota(jnp.int32, x.shape, 0)
    xp = jnp.where(ridx == 0, prev_row, pltpu.roll(x, 1, 0))
    xn = jnp.where(ridx == rows - 1, next_row, pltpu.roll(x, rows - 1, 0))
    return w[0:1] * xp + w[1:2] * x + w[2:3] * xn + b


def _conv3_phased(x, prev_row, next_row, w, b):
    ph = x.shape[0] // HY_PHASES
    blk = [x[r * ph:(r + 1) * ph] for r in range(HY_PHASES)]
    ridx = lax.broadcasted_iota(jnp.int32, blk[0].shape, 0)
    before = [jnp.where(ridx == 0, prev_row, pltpu.roll(blk[-1], 1, 0))] + blk[:-1]
    after = blk[1:] + [jnp.where(ridx == ph - 1, next_row, pltpu.roll(blk[0], ph - 1, 0))]
    return jnp.concatenate([w[0:1] * before[r] + w[1:2] * blk[r] + w[2:3] * after[r] + b
                            for r in range(HY_PHASES)], axis=0)


CONV_COLS = 512


def _proj_conv3(conv, hb, pv_ref, nx_ref, mod, nw, w_ref, lo, width, cw_ref, cb_ref):
    m = pl.program_id(1)
    nm = pl.num_programs(1)
    h_prev = _norm_mod(pv_ref[0], mod, nw)
    h_next = _norm_mod(nx_ref[0], mod, nw)
    has_prev, has_next = (m > 0).astype(F32), (m < nm - 1).astype(F32)
    for off in range(0, width, CONV_COLS):
        w = w_ref[:, lo + off:lo + off + CONV_COLS]
        prev = _dot(h_prev, w)[SUBLANES - 1:SUBLANES] * has_prev
        nxt = _dot(h_next, w)[0:1] * has_next
        yield off, conv(_dot(hb, w), prev, nxt, cw_ref[:, off:off + CONV_COLS], cb_ref[:, off:off + CONV_COLS])


def _inproj_even_kernel(x_ref, pv_ref, nx_ref, mod_ref, nw_ref, w_ref, perm_ref, cw_ref, cb_ref,
                        u_ref, ga_ref, v_ref, x1_ref, x2_ref):
    mod, nw = mod_ref[0], nw_ref[...]
    hb = _norm_mod(x_ref[0], mod, nw)
    lo_vx, hi_vx = 2 * S5_WIDTH, 2 * S5_WIDTH + 3 * HY_WIDTH
    u_ref[0] = _dot(hb, w_ref[:, :S5_WIDTH])
    ga_ref[0] = _dot(hb, w_ref[:, S5_WIDTH:lo_vx])
    hbp = _dot(perm_ref[...], hb).astype(BF16)
    ph = hb.shape[0] // HY_PHASES
    outs = (v_ref, x1_ref, x2_ref)
    for off, y in _proj_conv3(_conv3_phased, hbp, pv_ref, nx_ref, mod, nw, w_ref, lo_vx, 3 * HY_WIDTH,
                              cw_ref, cb_ref):
        which, col = off // HY_WIDTH, off % HY_WIDTH
        if which == 2:
            y = y * _silu(_dot(hbp, w_ref[:, hi_vx + col:hi_vx + col + CONV_COLS]))
        o_ref = outs[which]
        o_ref[0, :, :, col:col + CONV_COLS] = y.reshape(HY_PHASES, ph, CONV_COLS).astype(o_ref.dtype)


def _softplus(x):
    return jnp.maximum(x, 0.0) + jnp.log(1.0 + jnp.exp(-jnp.abs(x)))


def _inproj_odd_kernel(x_ref, pv_ref, nx_ref, mod_ref, nw_ref, w_ref, cw_ref, cb_ref, bias_ref, alog_ref,
                       z_ref, xbc_ref, dt_ref, cum_ref, *, tm):
    mod, nw = mod_ref[0], nw_ref[...]
    hb = _norm_mod(x_ref[0], mod, nw)
    lo_dt = D_INNER + SSD_CONV_DIM
    z_ref[0] = _dot(hb, w_ref[:, :D_INNER])
    for off, y in _proj_conv3(_conv3, hb, pv_ref, nx_ref, mod, nw, w_ref, D_INNER, SSD_CONV_DIM, cw_ref, cb_ref):
        xbc_ref[0, :, off:off + CONV_COLS] = _silu(y)
    dt = _softplus(_dot(hb, w_ref[:, lo_dt:]) + bias_ref[...])
    dt_ref[0] = dt
    da = dt * (-jnp.exp(alog_ref[...]))
    q = SSD_CHUNK
    ii = lax.broadcasted_iota(jnp.int32, (q, q), 0)
    jj = lax.broadcasted_iota(jnp.int32, (q, q), 1)
    low, up = (jj <= ii).astype(F32), (jj >= ii).astype(F32)
    fwd_lane = lax.broadcasted_iota(jnp.int32, (q, LANES), 1) < SSD_HEADS
    for c in range(tm // q):
        dac = da[c * q:(c + 1) * q]
        cum_ref[0, c * q:(c + 1) * q, :] = jnp.where(fwd_lane, _dot_hi(low, dac), _dot_hi(up, dac))


def _mod_spec(mod):
    per_batch = mod.shape[0] > 1
    return pl.BlockSpec((1, 1, mod.shape[-1]), (lambda b, m: (b, 0, 0)) if per_batch else (lambda b, m: (0, 0, 0)))


def _halo_specs(width, tl, seq):
    r = tl // SUBLANES
    last = seq // SUBLANES - 1
    return (pl.BlockSpec((1, SUBLANES, width), lambda b, m: (b, jnp.maximum(m * r - 1, 0), 0)),
            pl.BlockSpec((1, SUBLANES, width), lambda b, m: (b, jnp.minimum((m + 1) * r, last), 0)))


def _phase_perm(tm):
    ph = tm // HY_PHASES
    i = jnp.arange(tm, dtype=jnp.int32)
    src = HY_PHASES * (i % ph) + i // ph
    return (src[:, None] == i[None, :]).astype(BF16)


def _phased_spec(tm, width):
    return pl.BlockSpec((1, HY_PHASES, tm // HY_PHASES, width), lambda b, m: (b, 0, m, 0))


def _inproj(kernel_fn, x, mod, norm_w, w_bf16, consts, outs, tm=256):
    bsz, seq, _ = x.shape
    pv, nx = _halo_specs(D_MODEL, tm, seq)
    const = lambda a: pl.BlockSpec(a.shape, lambda b, m: (0, 0))
    out_specs, out_shape = [], []
    for width, dtype, phased in outs:
        if phased:
            out_specs.append(_phased_spec(tm, width))
            out_shape.append(jax.ShapeDtypeStruct((bsz, HY_PHASES, seq // HY_PHASES, width), dtype))
        else:
            out_specs.append(pl.BlockSpec((1, tm, width), lambda b, m: (b, m, 0)))
            out_shape.append(jax.ShapeDtypeStruct((bsz, seq, width), dtype))
    return pl.pallas_call(
        kernel_fn,
        grid=(bsz, seq // tm),
        in_specs=[pl.BlockSpec((1, tm, D_MODEL), lambda b, m: (b, m, 0)), pv, nx, _mod_spec(mod),
                  pl.BlockSpec((1, D_MODEL), lambda b, m: (0, 0)), const(w_bf16)] + [const(a) for a in consts],
        out_specs=out_specs,
        out_shape=out_shape,
        compiler_params=_params("arbitrary", "arbitrary"),
        name="inproj",
    )(x, x, x, mod, norm_w.reshape(1, D_MODEL), w_bf16, *consts)


def _s5_kernel(u_ref, pin_ref, pout_ref, bw_ref, cw_ref, lr_ref, li_ref, h0_ref, ys_ref, hf_ref, bu_ref, hst_ref,
               *, bsz, tq, jgroup):
    d = pl.program_id(0)
    blk = pl.program_id(1)
    nblk = pl.num_programs(1)
    cs = S5_CHUNK_STATE

    @pl.when(blk == 0)
    def _():
        hst_ref[...] = h0_ref[0]

    nsub = cs // LANES
    u2 = u_ref[...].reshape(bsz * tq, S5_WIDTH).astype(BF16)
    u_tm = _dot(pin_ref[...], u2).astype(BF16)
    for j in range(S5_LANE_CHUNKS):
        bu_ref[:, 2 * cs * j:2 * cs * (j + 1)] = _dot(u_tm[:, LANES * j:LANES * (j + 1)], bw_ref[0, j])

    for j0 in range(0, S5_LANE_CHUNKS, jgroup):
        cols = [(2 * cs * j + LANES * c, 2 * cs * j + cs + LANES * c, j, c)
                for j in range(j0, j0 + jgroup) for c in range(nsub)]
        lam = [(jnp.broadcast_to(lr_ref[0, j, :, LANES * c:LANES * (c + 1)], (bsz, LANES)),
                jnp.broadcast_to(li_ref[0, j, :, LANES * c:LANES * (c + 1)], (bsz, LANES))) for _, _, j, c in cols]

        def body(s, carry, cols=cols, lam=lam):
            t = jnp.where(d == 0, s, tq - 1 - s)
            rows = pl.ds(pl.multiple_of(t * bsz, bsz), bsz)
            out = []
            for i, (re_lo, im_lo, _, _) in enumerate(cols):
                hr, hi = carry[2 * i], carry[2 * i + 1]
                lam_re, lam_im = lam[i]
                nr = lam_re * hr - lam_im * hi + bu_ref[rows, re_lo:re_lo + LANES]
                ni = lam_re * hi + lam_im * hr + bu_ref[rows, im_lo:im_lo + LANES]
                bu_ref[rows, re_lo:re_lo + LANES] = nr
                bu_ref[rows, im_lo:im_lo + LANES] = ni
                out += [nr, ni]
            return tuple(out)

        init = []
        for re_lo, im_lo, _, _ in cols:
            init += [hst_ref[:, re_lo:re_lo + LANES], hst_ref[:, im_lo:im_lo + LANES]]
        fin = lax.fori_loop(0, tq, body, tuple(init), unroll=2)
        for i, (re_lo, im_lo, _, _) in enumerate(cols):
            hst_ref[:, re_lo:re_lo + LANES] = fin[2 * i]
            hst_ref[:, im_lo:im_lo + LANES] = fin[2 * i + 1]

    y_tm = jnp.concatenate([_dot(bu_ref[:, 2 * cs * j:2 * cs * (j + 1)].astype(BF16), cw_ref[0, j])
                            for j in range(S5_LANE_CHUNKS)], axis=-1)
    y_hi = y_tm.astype(BF16)
    y_lo = (y_tm - y_hi.astype(F32)).astype(BF16)
    y_bm = _dot(pout_ref[...], y_hi) + _dot(pout_ref[...], y_lo)
    ys_ref[0] = y_bm.reshape(bsz, tq, S5_WIDTH)

    @pl.when(blk == nblk - 1)
    def _():
        hf_ref[0] = hst_ref[...]


def _s5_scan(u, bw, cw, lr, li, h0, tq):
    bsz, seq, _ = u.shape
    nblk = seq // tq
    rows = bsz * tq
    r = jnp.arange(rows, dtype=jnp.int32)
    src = (r % bsz) * tq + r // bsz
    p_in = (src[:, None] == r[None, :]).astype(BF16)
    p_out = p_in.T

    def tblk(d, i):
        return i + d * (nblk - 1 - 2 * i)

    nj, cs = S5_LANE_CHUNKS, S5_CHUNK_STATE
    jgroup = 2 if bsz <= SUBLANES else 1
    return pl.pallas_call(
        functools.partial(_s5_kernel, bsz=bsz, tq=tq, jgroup=jgroup),
        grid=(2, nblk),
        in_specs=[
            pl.BlockSpec((bsz, tq, S5_WIDTH), lambda d, i: (0, tblk(d, i), 0)),
            pl.BlockSpec((rows, rows), lambda d, i: (0, 0)),
            pl.BlockSpec((rows, rows), lambda d, i: (0, 0)),
            pl.BlockSpec((1, nj, LANES, 2 * cs), lambda d, i: (d, 0, 0, 0)),
            pl.BlockSpec((1, nj, 2 * cs, LANES), lambda d, i: (d, 0, 0, 0)),
            pl.BlockSpec((1, nj, 1, cs), lambda d, i: (d, 0, 0, 0)),
            pl.BlockSpec((1, nj, 1, cs), lambda d, i: (d, 0, 0, 0)),
            pl.BlockSpec((1, bsz, S5_STATE_LANES), lambda d, i: (d, 0, 0)),
        ],
        out_specs=[
            pl.BlockSpec((1, bsz, tq, S5_WIDTH), lambda d, i: (d, 0, tblk(d, i), 0)),
            pl.BlockSpec((1, bsz, S5_STATE_LANES), lambda d, i: (d, 0, 0)),
        ],
        out_shape=[
            jax.ShapeDtypeStruct((2, bsz, seq, S5_WIDTH), F32),
            jax.ShapeDtypeStruct((2, bsz, S5_STATE_LANES), F32),
        ],
        scratch_shapes=[
            pltpu.VMEM((rows, S5_STATE_LANES), F32),
            pltpu.VMEM((bsz, S5_STATE_LANES), F32),
        ],
        compiler_params=_params("arbitrary", "arbitrary"),
        name="s5_scan",
    )(u, p_in, p_out, bw, cw, lr, li, h0)


def _s5_prepare(a_re, a_im, log_dt, b_re, b_im, c_re, c_im):
    dt = jnp.exp(log_dt.astype(F32))[..., None]
    a_re, a_im = a_re.astype(F32), a_im.astype(F32)
    mag = jnp.exp(a_re * dt)
    lbr, lbi = mag * jnp.cos(a_im * dt), mag * jnp.sin(a_im * dt)
    den = a_re * a_re + a_im * a_im
    cr = ((lbr - 1.0) * a_re + lbi * a_im) / den
    ci = (lbi * a_re - (lbr - 1.0) * a_im) / den
    bbr = cr[..., None] * b_re - ci[..., None] * b_im
    bbi = cr[..., None] * b_im + ci[..., None] * b_re
    nj, gpc = S5_LANE_CHUNKS, LANES // S5_GROUP
    eye = jnp.eye(gpc, dtype=F32)

    def pack_b(m):
        m = m.reshape(2, nj, gpc, S5_STATE, S5_GROUP)
        return jnp.einsum('djgph,gk->djghkp', m, eye).reshape(2, nj, LANES, S5_CHUNK_STATE)

    def pack_c(m):
        m = m.reshape(2, nj, gpc, S5_GROUP, S5_STATE)
        return jnp.einsum('djghp,gk->djgpkh', m, eye).reshape(2, nj, S5_CHUNK_STATE, LANES)

    bw = jnp.concatenate([pack_b(bbr), pack_b(bbi)], axis=-1).astype(BF16)
    cw = jnp.concatenate([pack_c(c_re.astype(F32)), pack_c(-c_im.astype(F32))], axis=-2).astype(BF16)
    lr = lbr.reshape(2, nj, 1, S5_CHUNK_STATE)
    li = lbi.reshape(2, nj, 1, S5_CHUNK_STATE)
    return bw, cw, lr, li


def _s5_state_to_lanes(re, im):
    bsz = re.shape[0]
    st = jnp.stack([re.reshape(bsz, 2, S5_LANE_CHUNKS, S5_CHUNK_STATE),
                    im.reshape(bsz, 2, S5_LANE_CHUNKS, S5_CHUNK_STATE)], axis=3)
    return jnp.transpose(st.reshape(bsz, 2, S5_STATE_LANES), (1, 0, 2))


def _s5_state_from_lanes(hf):
    bsz = hf.shape[1]
    st = jnp.transpose(hf, (1, 0, 2)).reshape(bsz, 2, S5_LANE_CHUNKS, 2, S5_CHUNK_STATE)
    re = st[:, :, :, 0].reshape(bsz, 2, S5_GROUPS, S5_STATE)
    im = st[:, :, :, 1].reshape(bsz, 2, S5_GROUPS, S5_STATE)
    return re, im


def _hy_filter_kernel(bands_ref, deltas_ref, w1t_ref, w1c_ref, w1s_ref, b1_ref, f1_ref, w2_ref, b2_ref, f2_ref,
                      w3_ref, o_ref, *, seq, tl):
    m = pl.program_id(0)
    t = (lax.broadcasted_iota(jnp.int32, (tl, 1), 0) + m * tl).astype(F32)
    t_norm = t / (seq - 1)
    ang = (2.0 * math.pi / seq) * t * bands_ref[...]
    pre = t_norm * w1t_ref[...] + _dot_hi(jnp.cos(ang), w1c_ref[...]) + _dot_hi(jnp.sin(ang), w1s_ref[...])
    f = jnp.sin(f1_ref[...] * (pre + b1_ref[...]))
    f = jnp.sin(f2_ref[...] * (_dot_hi(f, w2_ref[...]) + b2_ref[...]))
    f = _dot_hi(f, w3_ref[...])
    window = jnp.exp(-t_norm * deltas_ref[...])
    for o in range(2):
        for direction in range(2):
            lo = (2 * o + direction) * HY_WIDTH
            o_ref[o, direction] = f[:, lo:lo + HY_WIDTH] * window


def _hy_filters(seq, w1, b1, freq1, w2, b2, freq2, w3, tl=256):
    bands = jnp.linspace(1e-4, HY_BANDS - 1, HY_BANDS, dtype=F32).reshape(1, HY_BANDS)
    deltas = jnp.abs(jnp.linspace(math.log(HY_DECAY_TARGET) / HY_SLOW, math.log(HY_DECAY_TARGET) / HY_FAST,
                                  HY_WIDTH, dtype=F32)).reshape(1, HY_WIDTH)
    small = [bands, deltas, w1[0:1], w1[1:1 + HY_BANDS], w1[1 + HY_BANDS:], b1.reshape(1, -1), freq1.reshape(1, -1),
             w2, b2.reshape(1, -1), freq2.reshape(1, -1), w3]
    return pl.pallas_call(
        functools.partial(_hy_filter_kernel, seq=seq, tl=tl),
        grid=(seq // tl,),
        in_specs=[pl.BlockSpec(a.shape, lambda m: (0, 0)) for a in small],
        out_specs=pl.BlockSpec((2, 2, tl, HY_WIDTH), lambda m: (0, 0, m, 0)),
        out_shape=jax.ShapeDtypeStruct((2, 2, seq, HY_WIDTH), F32),
        compiler_params=_params("arbitrary"),
        name="hy_filters",
    )(*small)


def _polyphase_filters(hf):
    n4 = hf.shape[2] // HY_PHASES
    fwd, bwd = hf[:, 0], hf[:, 1]
    zero = jnp.zeros_like(fwd[:, :1])
    out = []
    for q in range(1 - HY_PHASES, HY_PHASES):
        if q >= 0:
            pos = fwd[:, q::HY_PHASES]
        else:
            pos = jnp.concatenate([bwd[:, -q:-q + 1], fwd[:, HY_PHASES + q::HY_PHASES][:, :n4 - 1]], axis=1)
        neg = jnp.concatenate([zero, bwd[:, HY_PHASES - q::HY_PHASES][:, :n4 - 1]], axis=1)
        out.append(jnp.stack([pos + neg, pos - neg], axis=1))
    return jnp.stack(out, axis=1).astype(BF16)


def _dft_matrices(seq):
    n = 2 * seq
    k = jnp.arange(seq, dtype=jnp.int32)
    ang = ((k[:, None] * k[None, :]) % n).astype(F32) * (2.0 * math.pi / n)
    cos, sin = jnp.cos(ang), jnp.sin(ang)
    alt = jnp.where(k % 2 == 0, 1.0, -1.0).astype(F32)
    fwd = jnp.concatenate([cos, jnp.where(k[:, None] == 0, alt[None, :], -sin)], axis=0)
    scale = jnp.where(k == 0, 1.0 / n, 2.0 / n).astype(F32)
    inv = jnp.concatenate([cos * scale[None, :], jnp.where(k[None, :] == 0, alt[:, None], -sin) * scale[None, :]],
                          axis=1)
    return fwd.astype(BF16), inv.astype(BF16)


def _hy_spec_kernel(fwd_ref, h_ref, bias_ref, o_ref, *, half):
    q = pl.program_id(1)
    f = fwd_ref[...]
    s_sum = _dot(f, h_ref[0, 0, 0])
    s_dif = _dot(f, h_ref[0, 0, 1])
    ridx = lax.broadcasted_iota(jnp.int32, s_sum.shape, 0)
    real_row = ridx <= half
    lag0 = jnp.where(real_row & (q == HY_PHASES - 1), bias_ref[0], 0.0)
    o_ref[0, 0] = jnp.where(real_row, s_sum, s_dif) + lag0


def _hy_spectra(hs, fwd, bias, tc=512):
    _, nq, _, n4, width = hs.shape
    return pl.pallas_call(
        functools.partial(_hy_spec_kernel, half=n4),
        grid=(2, nq, width // tc),
        in_specs=[
            pl.BlockSpec((2 * n4, n4), lambda o, q, c: (0, 0)),
            pl.BlockSpec((1, 1, 2, n4, tc), lambda o, q, c: (o, q, 0, 0, c)),
            pl.BlockSpec((1, 1, tc), lambda o, q, c: (o, 0, c)),
        ],
        out_specs=pl.BlockSpec((1, 1, 2 * n4, tc), lambda o, q, c: (o, q, 0, c)),
        out_shape=jax.ShapeDtypeStruct((2, nq, 2 * n4, width), F32),
        compiler_params=_params("arbitrary", "arbitrary", "arbitrary"),
        name="hy_spectra",
    )(fwd, hs, bias.reshape(2, 1, width))


def _hy_conv_kernel(v_ref, g_ref, fwd_ref, inv_ref, ks_ref, o_ref, *, half):
    f = fwd_ref[...]
    row0 = lax.broadcasted_iota(jnp.int32, (half, v_ref.shape[-1]), 0) == 0

    def split(s):
        s_im = s[half:]
        return s[:half], jnp.where(row0, 0.0, s_im), s_im[0:1]

    spec = [split(_dot(f, v_ref[0, r].astype(BF16))) for r in range(HY_PHASES)]
    filt = [split(ks_ref[0, q]) for q in range(2 * HY_PHASES - 1)]
    for r in range(HY_PHASES):
        p_re = p_im = p_ny = None
        for rp in range(HY_PHASES):
            s_re, s_im, s_ny = spec[rp]
            k_re, k_im, k_ny = filt[r - rp + HY_PHASES - 1]
            t_re = s_re * k_re - s_im * k_im
            t_im = s_re * k_im + s_im * k_re
            t_ny = s_ny * k_ny
            p_re = t_re if p_re is None else p_re + t_re
            p_im = t_im if p_im is None else p_im + t_im
            p_ny = t_ny if p_ny is None else p_ny + t_ny
        p = jnp.concatenate([p_re, jnp.where(row0, p_ny, p_im)], axis=0).astype(BF16)
        o_ref[0, r] = (g_ref[0, r] * _dot(inv_ref[...], p)).astype(o_ref.dtype)


def _hy_conv(v, gate, fwd, inv, kspec, order, tc):
    bsz, _, n4, width = v.shape
    nq = 2 * HY_PHASES - 1
    blk = pl.BlockSpec((1, HY_PHASES, n4, tc), lambda c, b: (b, 0, 0, c))
    return pl.pallas_call(
        functools.partial(_hy_conv_kernel, half=n4),
        grid=(width // tc, bsz),
        in_specs=[
            blk, blk,
            pl.BlockSpec((2 * n4, n4), lambda c, b: (0, 0)),
            pl.BlockSpec((n4, 2 * n4), lambda c, b: (0, 0)),
            pl.BlockSpec((1, nq, 2 * n4, tc), lambda c, b: (order, 0, 0, c)),
        ],
        out_specs=blk,
        out_shape=jax.ShapeDtypeStruct(v.shape, BF16),
        compiler_params=_params("arbitrary", "arbitrary"),
        name="hy_conv",
    )(v, gate, fwd, inv, kspec)


def _gelu_tanh(x):
    return 0.5 * x * (1.0 + jnp.tanh(math.sqrt(2.0 / math.pi) * (x + 0.044715 * (x * x * x))))


def _rms(x, w):
    return x * lax.rsqrt(jnp.mean(x * x, axis=-1, keepdims=True) + EPS) * w


def _even_tail_kernel(x_ref, mod_ref, yf_ref, yb_ref, u_ref, ga_ref, hy_ref, unperm_ref, d_ref, gw_ref, gb_ref,
                      wo_ref, fw_ref, o_ref, *, final):
    y = yf_ref[0, 0] + yb_ref[0, 0] + u_ref[0] * d_ref[...]
    y = _gelu_tanh(y)
    y = y * jax.nn.sigmoid(_dot(y.astype(BF16), gw_ref[...]) + gb_ref[...])
    ya = (y * _silu(ga_ref[0])).astype(BF16)
    hy = _dot(unperm_ref[...], hy_ref[0].reshape(ya.shape[0], HY_WIDTH)).astype(BF16)
    out = _dot(ya, wo_ref[:S5_WIDTH, :]) + _dot(hy, wo_ref[S5_WIDTH:, :])
    xn = x_ref[0] + mod_ref[0][:, 2 * D_MODEL:] * out
    o_ref[0] = _rms(xn, fw_ref[...]) if final else xn


def _even_tail(x, mod, ys, u, g_a, hy, s5_d, glu_w, glu_b, w_out, final_w, final, tm=256):
    bsz, seq, _ = x.shape
    tok = lambda w: pl.BlockSpec((1, tm, w), lambda b, m: (b, m, 0))
    const = lambda r, c: pl.BlockSpec((r, c), lambda b, m: (0, 0))
    return pl.pallas_call(
        functools.partial(_even_tail_kernel, final=final),
        grid=(bsz, seq // tm),
        in_specs=[
            tok(D_MODEL), _mod_spec(mod),
            pl.BlockSpec((1, 1, tm, S5_WIDTH), lambda b, m: (0, b, m, 0)),
            pl.BlockSpec((1, 1, tm, S5_WIDTH), lambda b, m: (1, b, m, 0)),
            tok(S5_WIDTH), tok(S5_WIDTH), _phased_spec(tm, HY_WIDTH), const(tm, tm),
            const(1, S5_WIDTH), const(S5_WIDTH, S5_WIDTH), const(1, S5_WIDTH), const(D_INNER, D_MODEL),
            const(1, D_MODEL),
        ],
        out_specs=tok(D_MODEL),
        out_shape=jax.ShapeDtypeStruct((bsz, seq, D_MODEL), F32),
        compiler_params=_params("arbitrary", "arbitrary"),
        name="even_tail",
    )(x, mod, ys, ys, u, g_a, hy, _phase_perm(tm).T, s5_d.reshape(1, -1), glu_w, glu_b.reshape(1, -1), w_out,
      final_w.reshape(1, -1))


def _ssd_direction(xbc, dt, cum, st_ref, direction, d_row):
    q = SSD_CHUNK
    hp = LANES // SSD_HEADDIM
    ii = lax.broadcasted_iota(jnp.int32, (q, q), 0)
    jj = lax.broadcasted_iota(jnp.int32, (q, q), 1)
    if direction == 0:
        tri = (jj <= ii)
        last = q - 1
    else:
        tri = (jj >= ii)
        last = 0
    cum_t = cum.T
    src_t = cum_t - jnp.log(dt.T)
    w_t = jnp.exp(cum_t[:, last:last + 1] - src_t)
    last_row = cum[last:last + 1, :]
    first_half = lax.broadcasted_iota(jnp.int32, (q, LANES), 1) < SSD_HEADDIM
    first_half_row = lax.broadcasted_iota(jnp.int32, (1, LANES), 1) < SSD_HEADDIM
    x_all = xbc[:, :D_INNER]
    ys = []
    for g in range(SSD_GROUPS):
        b_g = xbc[:, D_INNER + g * SSD_STATE:D_INNER + (g + 1) * SSD_STATE]
        c_g = xbc[:, D_INNER + (SSD_GROUPS + g) * SSD_STATE:D_INNER + (SSD_GROUPS + g + 1) * SSD_STATE]
        b_bf, c_bf = b_g.astype(BF16), c_g.astype(BF16)
        cb = lax.dot_general(c_bf, b_bf, (((1,), (1,)), ((), ())), preferred_element_type=F32)
        b_t = b_g.T
        heads_per_group = SSD_HEADS // SSD_GROUPS
        for pair in range(heads_per_group // hp):
            h0 = g * heads_per_group + pair * hp
            lo = h0 * SSD_HEADDIM
            x_pair = x_all[:, lo:lo + LANES]
            x_blk = jnp.concatenate([jnp.where(first_half, x_pair, 0.0).astype(BF16),
                                     jnp.where(first_half, 0.0, x_pair).astype(BF16)], axis=0)
            scores, b_w, cols, tot = [], [], [], []
            for h in (h0, h0 + 1):
                hl = direction * SSD_HEADS + h
                col = jnp.broadcast_to(cum[:, hl:hl + 1], (q, q))
                arg = jnp.where(tri, col - src_t[hl:hl + 1, :], -jnp.inf)
                scores.append((cb * jnp.exp(arg)).astype(BF16))
                b_w.append((b_t * w_t[hl:hl + 1, :]).astype(BF16))
                cols.append(col)
                tot.append(jnp.broadcast_to(last_row[:, hl:hl + 1], (1, LANES)))
            s_pair = st_ref[:, lo:lo + LANES]
            y = _dot(jnp.concatenate(scores, axis=1), x_blk)
            y = y + _dot(c_bf, s_pair.astype(BF16)) * jnp.exp(jnp.where(first_half, cols[0], cols[1]))
            if d_row is not None:
                y = y + x_pair * d_row[:, lo:lo + LANES]
            ys.append(y)
            decay = jnp.exp(jnp.where(first_half_row, tot[0], tot[1]))
            st_ref[:, lo:lo + LANES] = s_pair * decay + _dot(jnp.concatenate(b_w, axis=1), x_blk)
    return jnp.concatenate(ys, axis=-1)


def _ssd_kernel(xf_ref, xb_ref, dtf_ref, dtb_ref, cumf_ref, cumb_ref, d_ref, h0_ref, yf_ref, yb_ref, hf_ref, st_ref):
    c = pl.program_id(1)
    nc = pl.num_programs(1)

    @pl.when(c == 0)
    def _():
        st_ref[...] = h0_ref[0]

    yf_ref[0] = _ssd_direction(xf_ref[0], dtf_ref[0], cumf_ref[0], st_ref.at[0], 0, d_ref[...])
    yb_ref[0] = _ssd_direction(xb_ref[0], dtb_ref[0], cumb_ref[0], st_ref.at[1], 1, None)

    @pl.when(c == nc - 1)
    def _():
        hf_ref[0] = st_ref[...]


def _ssd_scan(xbc, dt, cum, d_exp, h0):
    bsz, seq, width = xbc.shape
    q = SSD_CHUNK
    nc = seq // q
    fwd = lambda w: pl.BlockSpec((1, q, w), lambda b, c: (b, c, 0))
    bwd = lambda w: pl.BlockSpec((1, q, w), lambda b, c: (b, nc - 1 - c, 0))
    st_spec = pl.BlockSpec((1, 2, SSD_STATE, D_INNER), lambda b, c: (b, 0, 0, 0))
    return pl.pallas_call(
        _ssd_kernel,
        grid=(bsz, nc),
        in_specs=[fwd(width), bwd(width), fwd(LANES), bwd(LANES), fwd(LANES), bwd(LANES),
                  pl.BlockSpec((1, D_INNER), lambda b, c: (0, 0)), st_spec],
        out_specs=[fwd(D_INNER), bwd(D_INNER), st_spec],
        out_shape=[
            jax.ShapeDtypeStruct((bsz, seq, D_INNER), F32),
            jax.ShapeDtypeStruct((bsz, seq, D_INNER), F32),
            jax.ShapeDtypeStruct((bsz, 2, SSD_STATE, D_INNER), F32),
        ],
        scratch_shapes=[pltpu.VMEM((2, SSD_STATE, D_INNER), F32)],
        compiler_params=_params("arbitrary", "arbitrary"),
        name="ssd_scan",
    )(xbc, xbc, dt, dt, cum, cum, d_exp, h0)


def _odd_tail_kernel(x_ref, mod_ref, yf_ref, yb_ref, z_ref, nw_ref, wo_ref, fw_ref, o_ref, *, final):
    y = (yf_ref[0] + yb_ref[0]) * _silu(z_ref[0])
    y = _rms(y, nw_ref[...]).astype(BF16)
    xn = x_ref[0] + mod_ref[0][:, 2 * D_MODEL:] * _dot(y, wo_ref[...])
    o_ref[0] = _rms(xn, fw_ref[...]) if final else xn


def _odd_tail(x, mod, yf, yb, z, norm_w, w_out, final_w, final, tm=256):
    bsz, seq, _ = x.shape
    tok = lambda w: pl.BlockSpec((1, tm, w), lambda b, m: (b, m, 0))
    const = lambda r, c: pl.BlockSpec((r, c), lambda b, m: (0, 0))
    return pl.pallas_call(
        functools.partial(_odd_tail_kernel, final=final),
        grid=(bsz, seq // tm),
        in_specs=[tok(D_MODEL), _mod_spec(mod), tok(D_INNER), tok(D_INNER), tok(D_INNER),
                  const(1, D_INNER), const(D_INNER, D_MODEL), const(1, D_MODEL)],
        out_specs=tok(D_MODEL),
        out_shape=jax.ShapeDtypeStruct((bsz, seq, D_MODEL), F32),
        compiler_params=_params("arbitrary", "arbitrary"),
        name="odd_tail",
    )(x, mod, yf, yb, z, norm_w.reshape(1, -1), w_out, final_w.reshape(1, -1))


def _even_layer(x, mod, norm_w, w_in, w_out, s5, s5_d, glu_w, glu_b, short_w, short_b, dft, kspec, h0,
                final_w, final, s5_tq, conv_tc, tm=256):
    outs = ((S5_WIDTH, F32, False), (S5_WIDTH, F32, False),
            (HY_WIDTH, BF16, True), (HY_WIDTH, F32, True), (HY_WIDTH, F32, True))
    u, g_a, v, x1, x2g = _inproj(_inproj_even_kernel, x, mod, norm_w, w_in,
                                 (_phase_perm(tm), short_w, short_b.reshape(1, -1)), outs, tm=tm)
    ys, hf = _s5_scan(u, *s5, h0, s5_tq)
    fwd, inv = dft
    z1 = _hy_conv(v, x1, fwd, inv, kspec, 0, conv_tc)
    hy = _hy_conv(z1, x2g, fwd, inv, kspec, 1, conv_tc)
    xn = _even_tail(x, mod, ys, u, g_a, hy, s5_d, glu_w, glu_b, w_out, final_w, final, tm=tm)
    return xn, hf


def _odd_layer(x, mod, norm_w, w_in, conv_w, conv_b, a_log, dt_bias, d_exp, ssd_norm_w, w_out, h0, final_w, final):
    tm = 256
    z, xbc, dt, cum = _inproj(functools.partial(_inproj_odd_kernel, tm=tm), x, mod, norm_w, w_in,
                              (conv_w, conv_b.reshape(1, -1), dt_bias, a_log),
                              tuple((w, F32, False) for w in (D_INNER, SSD_CONV_DIM, LANES, LANES)), tm=tm)
    yf, yb, hf = _ssd_scan(xbc, dt, cum, d_exp, h0)
    xn = _odd_tail(x, mod, yf, yb, z, ssd_norm_w, w_out, final_w, final)
    return xn, hf


def _ssd_state_to_lanes(st):
    bsz = st.shape[0]
    return jnp.transpose(st.astype(F32), (0, 1, 4, 2, 3)).reshape(bsz, 2, SSD_STATE, D_INNER)


def _ssd_state_from_lanes(st):
    bsz = st.shape[0]
    return jnp.transpose(st.reshape(bsz, 2, SSD_STATE, SSD_HEADS, SSD_HEADDIM), (0, 1, 3, 4, 2))


def kernel(x_prompt, x_sample, state_s5_re, state_s5_im, state_ssd, c, c_ctx, mod_w, mod_b, norm_w, final_norm_w,
           ev_w_in, ev_w_out, s5_a_re, s5_a_im, s5_log_dt, s5_b_re, s5_b_im, s5_c_re, s5_c_im, s5_d, s5_glu_w,
           s5_glu_b, hy_short_w, hy_short_b, hy_f_w1, hy_f_b1, hy_f_freq1, hy_f_w2, hy_f_b2, hy_f_freq2, hy_f_w3,
           hy_bias, ssd_w_in, ssd_conv_w, ssd_conv_b, ssd_a_log, ssd_dt_bias, ssd_d, ssd_norm_w, ssd_w_out):
    bp, lp, _ = x_prompt.shape
    bs, ls, _ = x_sample.shape
    rows = 2 * SUBLANES
    cvec = jnp.zeros((rows, D_MODEL), F32).at[0].set(c_ctx.astype(F32)).at[1:1 + bs].set(c.astype(F32))
    mod = _modulation(cvec, mod_w, mod_b)
    dft_p, dft_s = _dft_matrices(lp // HY_PHASES), _dft_matrices(ls // HY_PHASES)
    xp, xs = x_prompt, x_sample
    new_re, new_im, new_ssd = [], [], []
    for i in range(DEPTH):
        j = i // 2
        final = i == DEPTH - 1
        mod_p = mod[i, 0:1].reshape(1, 1, 3 * D_MODEL)
        mod_s = mod[i, 1:1 + bs].reshape(bs, 1, 3 * D_MODEL)
        if i % 2 == 0:
            s5 = _s5_prepare(s5_a_re[j], s5_a_im[j], s5_log_dt[j], s5_b_re[j], s5_b_im[j], s5_c_re[j], s5_c_im[j])
            w_in = ev_w_in[j].astype(BF16)
            w_out = ev_w_out[j].astype(BF16)
            glu_w = s5_glu_w[j].astype(BF16)
            fargs = (hy_f_w1[j], hy_f_b1[j], hy_f_freq1[j], hy_f_w2[j], hy_f_b2[j], hy_f_freq2[j], hy_f_w3[j])
            ks_p = _hy_spectra(_polyphase_filters(_hy_filters(lp, *fargs)), dft_p[0], hy_bias[j])
            ks_s = _hy_spectra(_polyphase_filters(_hy_filters(ls, *fargs)), dft_s[0], hy_bias[j])
            h0_p = jnp.zeros((2, bp, S5_STATE_LANES), F32)
            h0_s = _s5_state_to_lanes(state_s5_re[:, j].astype(F32), state_s5_im[:, j].astype(F32))
            common = (s5_d[j], glu_w, s5_glu_b[j], hy_short_w[j], hy_short_b[j])
            xp, hf = _even_layer(xp, mod_p, norm_w[i], w_in, w_out, s5, *common, dft_p, ks_p, h0_p,
                                 final_norm_w, final, s5_tq=16, conv_tc=512)
            xs, _ = _even_layer(xs, mod_s, norm_w[i], w_in, w_out, s5, *common, dft_s, ks_s, h0_s,
                                final_norm_w, final, s5_tq=32, conv_tc=256)
            re, im = _s5_state_from_lanes(hf)
            new_re.append(re)
            new_im.append(im)
        else:
            pad = LANES - 2 * SSD_HEADS
            w_in = jnp.pad(ssd_w_in[j], ((0, 0), (0, pad))).astype(BF16)
            w_out = ssd_w_out[j].astype(BF16)
            dt_bias = jnp.pad(ssd_dt_bias[j].reshape(1, -1).astype(F32), ((0, 0), (0, pad)))
            a_log = jnp.pad(ssd_a_log[j].reshape(1, -1).astype(F32), ((0, 0), (0, pad)))
            d_exp = jnp.repeat(ssd_d[j].astype(F32), SSD_HEADDIM).reshape(1, D_INNER)
            common = (ssd_conv_w[j], ssd_conv_b[j], a_log, dt_bias, d_exp, ssd_norm_w[j], w_out)
            h0_p = jnp.zeros((bp, 2, SSD_STATE, D_INNER), F32)
            h0_s = _ssd_state_to_lanes(state_ssd[:, j])
            xp, hf = _odd_layer(xp, mod_p, norm_w[i], w_in, *common, h0_p, final_norm_w, final)
            xs, _ = _odd_layer(xs, mod_s, norm_w[i], w_in, *common, h0_s, final_norm_w, final)
            new_ssd.append(_ssd_state_from_lanes(hf))
    return (xp, xs, jnp.stack(new_re, axis=1), jnp.stack(new_im, axis=1), jnp.stack(new_ssd, axis=1))
```

```python
import functools
import math

import jax
import jax.numpy as jnp
from jax import lax
from jax.experimental import pallas as pl
from jax.experimental.pallas import tpu as pltpu

F32 = jnp.float32
BF16 = jnp.bfloat16
HIGHEST = lax.Precision.HIGHEST

D_MODEL = 1024
DEPTH = 4
D_INNER = 2048
EPS = 1e-6
S5_WIDTH = 1024
S5_GROUP = 16
S5_GROUPS = 64
S5_STATE = 64
HY_WIDTH = 1024
HY_BANDS = 16
HY_HID = 64
HY_DECAY_TARGET = 1e-2
HY_FAST = 0.3
HY_SLOW = 1.5
HY_PHASES = 4
SSD_HEADDIM = 64
SSD_HEADS = 32
SSD_STATE = 128
SSD_GROUPS = 4
SSD_CHUNK = 128
SSD_CONV_DIM = D_INNER + 2 * SSD_GROUPS * SSD_STATE

LANES = 128
SUBLANES = 8
VMEM_LIMIT = 56 * 1024 * 1024

S5_LANE_CHUNKS = S5_WIDTH // LANES
S5_CHUNK_STATE = (LANES // S5_GROUP) * S5_STATE
S5_STATE_LANES = S5_LANE_CHUNKS * 2 * S5_CHUNK_STATE
S5_PERM_ROWS = 256


def _params(*sem):
    return pltpu.CompilerParams(dimension_semantics=sem, vmem_limit_bytes=VMEM_LIMIT)


def _silu(x):
    return x * jax.nn.sigmoid(x)


def _dot(a, b):
    return jnp.dot(a, b, preferred_element_type=F32)


def _dot_hi(a, b):
    return jnp.dot(a, b, preferred_element_type=F32, precision=HIGHEST)


def _mod_kernel(c_ref, w_ref, b_ref, o_ref):
    s = _silu(c_ref[...])
    o_ref[0] = _dot_hi(s, w_ref[0]) + b_ref[0]


def _modulation(cvec, mod_w, mod_b):
    rows = cvec.shape[0]
    tn = D_MODEL
    return pl.pallas_call(
        _mod_kernel,
        grid=(DEPTH, 3 * D_MODEL // tn),
        in_specs=[
            pl.BlockSpec((rows, D_MODEL), lambda i, n: (0, 0)),
            pl.BlockSpec((1, D_MODEL, tn), lambda i, n: (i, 0, n)),
            pl.BlockSpec((1, 1, tn), lambda i, n: (i, 0, n)),
        ],
        out_specs=pl.BlockSpec((1, rows, tn), lambda i, n: (i, 0, n)),
        out_shape=jax.ShapeDtypeStruct((DEPTH, rows, 3 * D_MODEL), F32),
        compiler_params=_params("arbitrary", "arbitrary"),
        name="modulation",
    )(cvec, mod_w, mod_b.reshape(DEPTH, 1, 3 * D_MODEL))


def _norm_mod(x, mod, nw):
    shift = mod[:, :D_MODEL]
    scale = mod[:, D_MODEL:2 * D_MODEL]
    ms = jnp.mean(x * x, axis=-1, keepdims=True)
    return (x * lax.rsqrt(ms + EPS) * nw * (1.0 + scale) + shift).astype(BF16)


def _conv3(x, prev_row, next_row, w, b):
    rows = x.shape[0]
    ridx = lax.broadcasted_iota(jnp.int32, x.shape, 0)
    xp = jnp.where(ridx == 0, prev_row, pltpu.roll(x, 1, 0))
    xn = jnp.where(ridx == rows - 1, next_row, pltpu.roll(x, rows - 1, 0))
    return w[0:1] * xp + w[1:2] * x + w[2:3] * xn + b


def _conv3_phased(x, prev_row, next_row, w, b):
    ph = x.shape[0] // HY_PHASES
    blk = [x[r * ph:(r + 1) * ph] for r in range(HY_PHASES)]
    ridx = lax.broadcasted_iota(jnp.int32, blk[0].shape, 0)
    before = [jnp.where(ridx == 0, prev_row, pltpu.roll(blk[-1], 1, 0))] + blk[:-1]
    after = blk[1:] + [jnp.where(ridx == ph - 1, next_row, pltpu.roll(blk[0], ph - 1, 0))]
    return jnp.concatenate([w[0:1] * before[r] + w[1:2] * blk[r] + w[2:3] * after[r] + b
                            for r in range(HY_PHASES)], axis=0)


CONV_COLS = 512
TOKEN_BLOCK = 512


def _proj_conv3(conv, hb, pv_ref, nx_ref, mod, nw, w_ref, lo, width, cw_ref, cb_ref):
    m = pl.program_id(1)
    nm = pl.num_programs(1)
    h_prev = _norm_mod(pv_ref[0], mod, nw)
    h_next = _norm_mod(nx_ref[0], mod, nw)
    has_prev, has_next = (m > 0).astype(F32), (m < nm - 1).astype(F32)
    for off in range(0, width, CONV_COLS):
        w = w_ref[:, lo + off:lo + off + CONV_COLS]
        prev = _dot(h_prev, w)[SUBLANES - 1:SUBLANES] * has_prev
        nxt = _dot(h_next, w)[0:1] * has_next
        yield off, conv(_dot(hb, w), prev, nxt, cw_ref[:, off:off + CONV_COLS], cb_ref[:, off:off + CONV_COLS])


def _inproj_even_kernel(x_ref, pv_ref, nx_ref, mod_ref, nw_ref, w_ref, perm_ref, cw_ref, cb_ref,
                        u_ref, ga_ref, v_ref, x1_ref, x2_ref):
    mod, nw = mod_ref[0], nw_ref[...]
    hb = _norm_mod(x_ref[0], mod, nw)
    lo_vx, hi_vx = 2 * S5_WIDTH, 2 * S5_WIDTH + 3 * HY_WIDTH
    u_ref[0] = _dot(hb, w_ref[:, :S5_WIDTH])
    ga_ref[0] = _dot(hb, w_ref[:, S5_WIDTH:lo_vx])
    hbp = _dot(perm_ref[...], hb).astype(BF16)
    ph = hb.shape[0] // HY_PHASES
    outs = (v_ref, x1_ref, x2_ref)
    for off, y in _proj_conv3(_conv3_phased, hbp, pv_ref, nx_ref, mod, nw, w_ref, lo_vx, 3 * HY_WIDTH,
                              cw_ref, cb_ref):
        which, col = off // HY_WIDTH, off % HY_WIDTH
        if which == 2:
            y = y * _silu(_dot(hbp, w_ref[:, hi_vx + col:hi_vx + col + CONV_COLS]))
        o_ref = outs[which]
        o_ref[0, :, :, col:col + CONV_COLS] = y.reshape(HY_PHASES, ph, CONV_COLS).astype(o_ref.dtype)


def _softplus(x):
    return jnp.maximum(x, 0.0) + jnp.log(1.0 + jnp.exp(-jnp.abs(x)))


def _inproj_odd_kernel(x_ref, pv_ref, nx_ref, mod_ref, nw_ref, w_ref, cw_ref, cb_ref, bias_ref, alog_ref,
                       z_ref, xbc_ref, dt_ref, cum_ref, *, tm):
    mod, nw = mod_ref[0], nw_ref[...]
    hb = _norm_mod(x_ref[0], mod, nw)
    lo_dt = D_INNER + SSD_CONV_DIM
    z_ref[0] = _dot(hb, w_ref[:, :D_INNER])
    for off, y in _proj_conv3(_conv3, hb, pv_ref, nx_ref, mod, nw, w_ref, D_INNER, SSD_CONV_DIM, cw_ref, cb_ref):
        xbc_ref[0, :, off:off + CONV_COLS] = _silu(y)
    dt = _softplus(_dot(hb, w_ref[:, lo_dt:]) + bias_ref[...])
    dt_ref[0] = dt
    da = dt * (-jnp.exp(alog_ref[...]))
    q = SSD_CHUNK
    ii = lax.broadcasted_iota(jnp.int32, (q, q), 0)
    jj = lax.broadcasted_iota(jnp.int32, (q, q), 1)
    low, up = (jj <= ii).astype(F32), (jj >= ii).astype(F32)
    fwd_lane = lax.broadcasted_iota(jnp.int32, (q, LANES), 1) < SSD_HEADS
    for c in range(tm // q):
        dac = da[c * q:(c + 1) * q]
        cum_ref[0, c * q:(c + 1) * q, :] = jnp.where(fwd_lane, _dot_hi(low, dac), _dot_hi(up, dac))


def _mod_spec(mod):
    per_batch = mod.shape[0] > 1
    return pl.BlockSpec((1, 1, mod.shape[-1]), (lambda b, m: (b, 0, 0)) if per_batch else (lambda b, m: (0, 0, 0)))


def _halo_specs(width, tl, seq):
    r = tl // SUBLANES
    last = seq // SUBLANES - 1
    return (pl.BlockSpec((1, SUBLANES, width), lambda b, m: (b, jnp.maximum(m * r - 1, 0), 0)),
            pl.BlockSpec((1, SUBLANES, width), lambda b, m: (b, jnp.minimum((m + 1) * r, last), 0)))


def _phase_perm(tm):
    ph = tm // HY_PHASES
    i = jnp.arange(tm, dtype=jnp.int32)
    src = HY_PHASES * (i % ph) + i // ph
    return (src[:, None] == i[None, :]).astype(BF16)


def _phased_spec(tm, width):
    return pl.BlockSpec((1, HY_PHASES, tm // HY_PHASES, width), lambda b, m: (b, 0, m, 0))


def _inproj(kernel_fn, x, mod, norm_w, w_bf16, consts, outs, tm=256):
    bsz, seq, _ = x.shape
    pv, nx = _halo_specs(D_MODEL, tm, seq)
    const = lambda a: pl.BlockSpec(a.shape, lambda b, m: (0, 0))
    out_specs, out_shape = [], []
    for width, dtype, phased in outs:
        if phased:
            out_specs.append(_phased_spec(tm, width))
            out_shape.append(jax.ShapeDtypeStruct((bsz, HY_PHASES, seq // HY_PHASES, width), dtype))
        else:
            out_specs.append(pl.BlockSpec((1, tm, width), lambda b, m: (b, m, 0)))
            out_shape.append(jax.ShapeDtypeStruct((bsz, seq, width), dtype))
    return pl.pallas_call(
        kernel_fn,
        grid=(bsz, seq // tm),
        in_specs=[pl.BlockSpec((1, tm, D_MODEL), lambda b, m: (b, m, 0)), pv, nx, _mod_spec(mod),
                  pl.BlockSpec((1, D_MODEL), lambda b, m: (0, 0)),
                  pl.BlockSpec(w_bf16.shape, lambda b, m: (0, 0), pipeline_mode=pl.Buffered(1))]
                 + [const(a) for a in consts],
        out_specs=out_specs,
        out_shape=out_shape,
        compiler_params=_params("arbitrary", "arbitrary"),
        name="inproj",
    )(x, x, x, mod, norm_w.reshape(1, D_MODEL), w_bf16, *consts)


def _s5_kernel(u_ref, pin_ref, pout_ref, bw_ref, cw_ref, lr_ref, li_ref, h0_ref, ys_ref, hf_ref, bu_ref, hst_ref,
               *, bsz, tq, jgroup):
    d = pl.program_id(0)
    blk = pl.program_id(1)
    nblk = pl.num_programs(1)
    cs = S5_CHUNK_STATE

    @pl.when(blk == 0)
    def _():
        hst_ref[...] = h0_ref[0]

    nsub = cs // LANES
    tsub = pin_ref.shape[0] // bsz
    u_tm = jnp.concatenate(
        [_dot(pin_ref[...], u_ref[:, s:s + tsub, :].reshape(bsz * tsub, S5_WIDTH).astype(BF16)).astype(BF16)
         for s in range(0, tq, tsub)], axis=0)
    for j in range(S5_LANE_CHUNKS):
        bu_ref[:, 2 * cs * j:2 * cs * (j + 1)] = _dot(u_tm[:, LANES * j:LANES * (j + 1)], bw_ref[0, j])

    for j0 in range(0, S5_LANE_CHUNKS, jgroup):
        cols = [(2 * cs * j + LANES * c, 2 * cs * j + cs + LANES * c, j, c)
                for j in range(j0, j0 + jgroup) for c in range(nsub)]
        lam = [(jnp.broadcast_to(lr_ref[0, j, :, LANES * c:LANES * (c + 1)], (bsz, LANES)),
                jnp.broadcast_to(li_ref[0, j, :, LANES * c:LANES * (c + 1)], (bsz, LANES))) for _, _, j, c in cols]

        def body(s, carry, cols=cols, lam=lam):
            t = jnp.where(d == 0, s, tq - 1 - s)
            rows = pl.ds(pl.multiple_of(t * bsz, bsz), bsz)
            out = []
            for i, (re_lo, im_lo, _, _) in enumerate(cols):
                hr, hi = carry[2 * i], carry[2 * i + 1]
                lam_re, lam_im = lam[i]
                nr = lam_re * hr - lam_im * hi + bu_ref[rows, re_lo:re_lo + LANES]
                ni = lam_re * hi + lam_im * hr + bu_ref[rows, im_lo:im_lo + LANES]
                bu_ref[rows, re_lo:re_lo + LANES] = nr
                bu_ref[rows, im_lo:im_lo + LANES] = ni
                out += [nr, ni]
            return tuple(out)

        init = []
        for re_lo, im_lo, _, _ in cols:
            init += [hst_ref[:, re_lo:re_lo + LANES], hst_ref[:, im_lo:im_lo + LANES]]
        fin = lax.fori_loop(0, tq, body, tuple(init), unroll=2)
        for i, (re_lo, im_lo, _, _) in enumerate(cols):
            hst_ref[:, re_lo:re_lo + LANES] = fin[2 * i]
            hst_ref[:, im_lo:im_lo + LANES] = fin[2 * i + 1]

    y_tm = jnp.concatenate([_dot(bu_ref[:, 2 * cs * j:2 * cs * (j + 1)].astype(BF16), cw_ref[0, j])
                            for j in range(S5_LANE_CHUNKS)], axis=-1)
    y_hi = y_tm.astype(BF16)
    y_lo = (y_tm - y_hi.astype(F32)).astype(BF16)
    for s in range(0, tq, tsub):
        rows = slice(s * bsz, (s + tsub) * bsz)
        y_bm = _dot(pout_ref[...], y_hi[rows]) + _dot(pout_ref[...], y_lo[rows])
        ys_ref[0, :, s:s + tsub, :] = y_bm.reshape(bsz, tsub, S5_WIDTH)

    @pl.when(blk == nblk - 1)
    def _():
        hf_ref[0] = hst_ref[...]


def _s5_scan(u, bw, cw, lr, li, h0, tq):
    bsz, seq, _ = u.shape
    nblk = seq // tq
    rows = bsz * tq
    prow = min(rows, S5_PERM_ROWS)
    tsub = prow // bsz
    r = jnp.arange(prow, dtype=jnp.int32)
    src = (r % bsz) * tsub + r // bsz
    p_in = (src[:, None] == r[None, :]).astype(BF16)
    p_out = p_in.T

    def tblk(d, i):
        return i + d * (nblk - 1 - 2 * i)

    nj, cs = S5_LANE_CHUNKS, S5_CHUNK_STATE
    jgroup = 2 if bsz <= SUBLANES else 1
    return pl.pallas_call(
        functools.partial(_s5_kernel, bsz=bsz, tq=tq, jgroup=jgroup),
        grid=(2, nblk),
        in_specs=[
            pl.BlockSpec((bsz, tq, S5_WIDTH), lambda d, i: (0, tblk(d, i), 0)),
            pl.BlockSpec((prow, prow), lambda d, i: (0, 0)),
            pl.BlockSpec((prow, prow), lambda d, i: (0, 0)),
            pl.BlockSpec((1, nj, LANES, 2 * cs), lambda d, i: (d, 0, 0, 0)),
            pl.BlockSpec((1, nj, 2 * cs, LANES), lambda d, i: (d, 0, 0, 0)),
            pl.BlockSpec((1, nj, 1, cs), lambda d, i: (d, 0, 0, 0)),
            pl.BlockSpec((1, nj, 1, cs), lambda d, i: (d, 0, 0, 0)),
            pl.BlockSpec((1, bsz, S5_STATE_LANES), lambda d, i: (d, 0, 0)),
        ],
        out_specs=[
            pl.BlockSpec((1, bsz, tq, S5_WIDTH), lambda d, i: (d, 0, tblk(d, i), 0)),
            pl.BlockSpec((1, bsz, S5_STATE_LANES), lambda d, i: (d, 0, 0)),
        ],
        out_shape=[
            jax.ShapeDtypeStruct((2, bsz, seq, S5_WIDTH), F32),
            jax.ShapeDtypeStruct((2, bsz, S5_STATE_LANES), F32),
        ],
        scratch_shapes=[
            pltpu.VMEM((rows, S5_STATE_LANES), F32),
            pltpu.VMEM((bsz, S5_STATE_LANES), F32),
        ],
        compiler_params=_params("arbitrary", "arbitrary"),
        name="s5_scan",
    )(u, p_in, p_out, bw, cw, lr, li, h0)


def _s5_prepare(a_re, a_im, log_dt, b_re, b_im, c_re, c_im):
    dt = jnp.exp(log_dt.astype(F32))[..., None]
    a_re, a_im = a_re.astype(F32), a_im.astype(F32)
    mag = jnp.exp(a_re * dt)
    lbr, lbi = mag * jnp.cos(a_im * dt), mag * jnp.sin(a_im * dt)
    den = a_re * a_re + a_im * a_im
    cr = ((lbr - 1.0) * a_re + lbi * a_im) / den
    ci = (lbi * a_re - (lbr - 1.0) * a_im) / den
    bbr = cr[..., None] * b_re - ci[..., None] * b_im
    bbi = cr[..., None] * b_im + ci[..., None] * b_re
    nj, gpc = S5_LANE_CHUNKS, LANES // S5_GROUP
    eye = jnp.eye(gpc, dtype=F32)

    def pack_b(m):
        m = m.reshape(2, nj, gpc, S5_STATE, S5_GROUP)
        return jnp.einsum('djgph,gk->djghkp', m, eye).reshape(2, nj, LANES, S5_CHUNK_STATE)

    def pack_c(m):
        m = m.reshape(2, nj, gpc, S5_GROUP, S5_STATE)
        return jnp.einsum('djghp,gk->djgpkh', m, eye).reshape(2, nj, S5_CHUNK_STATE, LANES)

    bw = jnp.concatenate([pack_b(bbr), pack_b(bbi)], axis=-1).astype(BF16)
    cw = jnp.concatenate([pack_c(c_re.astype(F32)), pack_c(-c_im.astype(F32))], axis=-2).astype(BF16)
    lr = lbr.reshape(2, nj, 1, S5_CHUNK_STATE)
    li = lbi.reshape(2, nj, 1, S5_CHUNK_STATE)
    return bw, cw, lr, li


def _s5_state_to_lanes(re, im):
    bsz = re.shape[0]
    st = jnp.stack([re.reshape(bsz, 2, S5_LANE_CHUNKS, S5_CHUNK_STATE),
                    im.reshape(bsz, 2, S5_LANE_CHUNKS, S5_CHUNK_STATE)], axis=3)
    return jnp.transpose(st.reshape(bsz, 2, S5_STATE_LANES), (1, 0, 2))


def _s5_state_from_lanes(hf):
    bsz = hf.shape[1]
    st = jnp.transpose(hf, (1, 0, 2)).reshape(bsz, 2, S5_LANE_CHUNKS, 2, S5_CHUNK_STATE)
    re = st[:, :, :, 0].reshape(bsz, 2, S5_GROUPS, S5_STATE)
    im = st[:, :, :, 1].reshape(bsz, 2, S5_GROUPS, S5_STATE)
    return re, im


def _hy_filter_kernel(bands_ref, deltas_ref, w1t_ref, w1c_ref, w1s_ref, b1_ref, f1_ref, w2_ref, b2_ref, f2_ref,
                      w3_ref, o_ref, *, seq, tl):
    r = pl.program_id(0)
    m = pl.program_id(1)
    t = (HY_PHASES * (lax.broadcasted_iota(jnp.int32, (tl, 1), 0) + m * tl) + r).astype(F32)
    t_norm = t / (seq - 1)
    ang = (2.0 * math.pi / seq) * t * bands_ref[...]
    pre = t_norm * w1t_ref[...] + _dot_hi(jnp.cos(ang), w1c_ref[...]) + _dot_hi(jnp.sin(ang), w1s_ref[...])
    f = jnp.sin(f1_ref[...] * (pre + b1_ref[...]))
    f = jnp.sin(f2_ref[...] * (_dot_hi(f, w2_ref[...]) + b2_ref[...]))
    f = _dot_hi(f, w3_ref[...])
    window = jnp.exp(-t_norm * deltas_ref[...])
    for o in range(2):
        for direction in range(2):
            lo = (2 * o + direction) * HY_WIDTH
            o_ref[o, direction, 0] = f[:, lo:lo + HY_WIDTH] * window


def _hy_filters(seq, w1, b1, freq1, w2, b2, freq2, w3):
    n4 = seq // HY_PHASES
    tl = min(256, n4)
    bands = jnp.linspace(1e-4, HY_BANDS - 1, HY_BANDS, dtype=F32).reshape(1, HY_BANDS)
    deltas = jnp.abs(jnp.linspace(math.log(HY_DECAY_TARGET) / HY_SLOW, math.log(HY_DECAY_TARGET) / HY_FAST,
                                  HY_WIDTH, dtype=F32)).reshape(1, HY_WIDTH)
    small = [bands, deltas, w1[0:1], w1[1:1 + HY_BANDS], w1[1 + HY_BANDS:], b1.reshape(1, -1), freq1.reshape(1, -1),
             w2, b2.reshape(1, -1), freq2.reshape(1, -1), w3]
    return pl.pallas_call(
        functools.partial(_hy_filter_kernel, seq=seq, tl=tl),
        grid=(HY_PHASES, n4 // tl),
        in_specs=[pl.BlockSpec(a.shape, lambda r, m: (0, 0)) for a in small],
        out_specs=pl.BlockSpec((2, 2, 1, tl, HY_WIDTH), lambda r, m: (0, 0, r, m, 0)),
        out_shape=jax.ShapeDtypeStruct((2, 2, HY_PHASES, n4, HY_WIDTH), F32),
        compiler_params=_params("arbitrary", "arbitrary"),
        name="hy_filters",
    )(*small)


def _polyphase_filters(hf):
    fwd, bwd = hf[:, 0], hf[:, 1]
    zero = jnp.zeros_like(fwd[:, 0, :1])

    def lags(h, first, start):
        r, j0 = start % HY_PHASES, start // HY_PHASES
        tail = h[:, r, j0:] if j0 else h[:, r, :-1]
        return jnp.concatenate([first, tail], axis=1)

    out = []
    for q in range(1 - HY_PHASES, HY_PHASES):
        pos = fwd[:, q] if q >= 0 else lags(fwd, bwd[:, -q, :1], HY_PHASES + q)
        neg = lags(bwd, zero, HY_PHASES - q)
        out.append(jnp.stack([pos + neg, pos - neg], axis=1))
    return jnp.stack(out, axis=1).astype(BF16)


def _dft_matrices(seq):
    n = 2 * seq
    k = jnp.arange(seq, dtype=jnp.int32)
    ang = ((k[:, None] * k[None, :]) % n).astype(F32) * (2.0 * math.pi / n)
    cos, sin = jnp.cos(ang), jnp.sin(ang)
    alt = jnp.where(k % 2 == 0, 1.0, -1.0).astype(F32)
    fwd = jnp.concatenate([cos, jnp.where(k[:, None] == 0, alt[None, :], -sin)], axis=0)
    scale = jnp.where(k == 0, 1.0 / n, 2.0 / n).astype(F32)
    inv = jnp.concatenate([cos * scale[None, :], jnp.where(k[None, :] == 0, alt[:, None], -sin) * scale[None, :]],
                          axis=1)
    return fwd.astype(BF16), inv.astype(BF16)


def _hy_spec_kernel(fwd_ref, h_ref, bias_ref, o_ref, *, half):
    q = pl.program_id(1)
    f = fwd_ref[...]
    s_sum = _dot(f, h_ref[0, 0, 0])
    s_dif = _dot(f, h_ref[0, 0, 1])
    ridx = lax.broadcasted_iota(jnp.int32, s_sum.shape, 0)
    real_row = ridx <= half
    lag0 = jnp.where(real_row & (q == HY_PHASES - 1), bias_ref[0], 0.0)
    o_ref[0, 0] = jnp.where(real_row, s_sum, s_dif) + lag0


def _hy_spectra(hs, fwd, bias, tc=512):
    _, nq, _, n4, width = hs.shape
    return pl.pallas_call(
        functools.partial(_hy_spec_kernel, half=n4),
        grid=(2, nq, width // tc),
        in_specs=[
            pl.BlockSpec((2 * n4, n4), lambda o, q, c: (0, 0)),
            pl.BlockSpec((1, 1, 2, n4, tc), lambda o, q, c: (o, q, 0, 0, c)),
            pl.BlockSpec((1, 1, tc), lambda o, q, c: (o, 0, c)),
        ],
        out_specs=pl.BlockSpec((1, 1, 2 * n4, tc), lambda o, q, c: (o, q, 0, c)),
        out_shape=jax.ShapeDtypeStruct((2, nq, 2 * n4, width), F32),
        compiler_params=_params("arbitrary", "arbitrary", "arbitrary"),
        name="hy_spectra",
    )(fwd, hs, bias.reshape(2, 1, width))


def _hy_conv_kernel(v_ref, g_ref, fwd_ref, inv_ref, ks_ref, o_ref, *, half):
    f = fwd_ref[...]
    row0 = lax.broadcasted_iota(jnp.int32, (half, v_ref.shape[-1]), 0) == 0

    def split(s):
        s_im = s[half:]
        return s[:half], jnp.where(row0, 0.0, s_im), s_im[0:1]

    spec = [split(_dot(f, v_ref[0, r].astype(BF16))) for r in range(HY_PHASES)]
    filt = [split(ks_ref[0, q]) for q in range(2 * HY_PHASES - 1)]
    for r in range(HY_PHASES):
        p_re = p_im = p_ny = None
        for rp in range(HY_PHASES):
            s_re, s_im, s_ny = spec[rp]
            k_re, k_im, k_ny = filt[r - rp + HY_PHASES - 1]
            t_re = s_re * k_re - s_im * k_im
            t_im = s_re * k_im + s_im * k_re
            t_ny = s_ny * k_ny
            p_re = t_re if p_re is None else p_re + t_re
            p_im = t_im if p_im is None else p_im + t_im
            p_ny = t_ny if p_ny is None else p_ny + t_ny
        p = jnp.concatenate([p_re, jnp.where(row0, p_ny, p_im)], axis=0).astype(BF16)
        o_ref[0, r] = (g_ref[0, r] * _dot(inv_ref[...], p)).astype(o_ref.dtype)


def _hy_conv(v, gate, fwd, inv, kspec, order, tc):
    bsz, _, n4, width = v.shape
    nq = 2 * HY_PHASES - 1
    blk = pl.BlockSpec((1, HY_PHASES, n4, tc), lambda c, b: (b, 0, 0, c))
    return pl.pallas_call(
        functools.partial(_hy_conv_kernel, half=n4),
        grid=(width // tc, bsz),
        in_specs=[
            blk, blk,
            pl.BlockSpec((2 * n4, n4), lambda c, b: (0, 0)),
            pl.BlockSpec((n4, 2 * n4), lambda c, b: (0, 0)),
            pl.BlockSpec((1, nq, 2 * n4, tc), lambda c, b: (order, 0, 0, c)),
        ],
        out_specs=blk,
        out_shape=jax.ShapeDtypeStruct(v.shape, BF16),
        compiler_params=_params("arbitrary", "arbitrary"),
        name="hy_conv",
    )(v, gate, fwd, inv, kspec)


def _gelu_tanh(x):
    return 0.5 * x * (1.0 + jnp.tanh(math.sqrt(2.0 / math.pi) * (x + 0.044715 * (x * x * x))))


def _rms(x, w):
    return x * lax.rsqrt(jnp.mean(x * x, axis=-1, keepdims=True) + EPS) * w


def _even_tail_kernel(x_ref, mod_ref, yf_ref, yb_ref, u_ref, ga_ref, hy_ref, unperm_ref, d_ref, gw_ref, gb_ref,
                      wo_ref, fw_ref, o_ref, *, final):
    y = yf_ref[0, 0] + yb_ref[0, 0] + u_ref[0] * d_ref[...]
    y = _gelu_tanh(y)
    y = y * jax.nn.sigmoid(_dot(y.astype(BF16), gw_ref[...]) + gb_ref[...])
    ya = (y * _silu(ga_ref[0])).astype(BF16)
    hy = _dot(unperm_ref[...], hy_ref[0].reshape(ya.shape[0], HY_WIDTH)).astype(BF16)
    out = _dot(ya, wo_ref[:S5_WIDTH, :]) + _dot(hy, wo_ref[S5_WIDTH:, :])
    xn = x_ref[0] + mod_ref[0][:, 2 * D_MODEL:] * out
    o_ref[0] = _rms(xn, fw_ref[...]) if final else xn


def _even_tail(x, mod, ys, u, g_a, hy, s5_d, glu_w, glu_b, w_out, final_w, final, tm=256):
    bsz, seq, _ = x.shape
    tok = lambda w: pl.BlockSpec((1, tm, w), lambda b, m: (b, m, 0))
    const = lambda r, c: pl.BlockSpec((r, c), lambda b, m: (0, 0))
    return pl.pallas_call(
        functools.partial(_even_tail_kernel, final=final),
        grid=(bsz, seq // tm),
        in_specs=[
            tok(D_MODEL), _mod_spec(mod),
            pl.BlockSpec((1, 1, tm, S5_WIDTH), lambda b, m: (0, b, m, 0)),
            pl.BlockSpec((1, 1, tm, S5_WIDTH), lambda b, m: (1, b, m, 0)),
            tok(S5_WIDTH), tok(S5_WIDTH), _phased_spec(tm, HY_WIDTH), const(tm, tm),
            const(1, S5_WIDTH), const(S5_WIDTH, S5_WIDTH), const(1, S5_WIDTH), const(D_INNER, D_MODEL),
            const(1, D_MODEL),
        ],
        out_specs=tok(D_MODEL),
        out_shape=jax.ShapeDtypeStruct((bsz, seq, D_MODEL), F32),
        compiler_params=_params("arbitrary", "arbitrary"),
        name="even_tail",
    )(x, mod, ys, ys, u, g_a, hy, _phase_perm(tm).T, s5_d.reshape(1, -1), glu_w, glu_b.reshape(1, -1), w_out,
      final_w.reshape(1, -1))


def _ssd_direction(xbc, dt, cum, st_ref, direction, d_row):
    q = SSD_CHUNK
    hp = LANES // SSD_HEADDIM
    ii = lax.broadcasted_iota(jnp.int32, (q, q), 0)
    jj = lax.broadcasted_iota(jnp.int32, (q, q), 1)
    if direction == 0:
        tri = (jj <= ii)
        last = q - 1
    else:
        tri = (jj >= ii)
        last = 0
    cum_t = cum.T
    src_t = cum_t - jnp.log(dt.T)
    w_t = jnp.exp(cum_t[:, last:last + 1] - src_t)
    last_row = cum[last:last + 1, :]
    first_half = lax.broadcasted_iota(jnp.int32, (q, LANES), 1) < SSD_HEADDIM
    first_half_row = lax.broadcasted_iota(jnp.int32, (1, LANES), 1) < SSD_HEADDIM
    x_all = xbc[:, :D_INNER]
    ys = []
    for g in range(SSD_GROUPS):
        b_g = xbc[:, D_INNER + g * SSD_STATE:D_INNER + (g + 1) * SSD_STATE]
        c_g = xbc[:, D_INNER + (SSD_GROUPS + g) * SSD_STATE:D_INNER + (SSD_GROUPS + g + 1) * SSD_STATE]
        b_bf, c_bf = b_g.astype(BF16), c_g.astype(BF16)
        cb = lax.dot_general(c_bf, b_bf, (((1,), (1,)), ((), ())), preferred_element_type=F32)
        b_t = b_g.T
        heads_per_group = SSD_HEADS // SSD_GROUPS
        for pair in range(heads_per_group // hp):
            h0 = g * heads_per_group + pair * hp
            lo = h0 * SSD_HEADDIM
            x_pair = x_all[:, lo:lo + LANES]
            x_blk = jnp.concatenate([jnp.where(first_half, x_pair, 0.0).astype(BF16),
                                     jnp.where(first_half, 0.0, x_pair).astype(BF16)], axis=0)
            scores, b_w, cols, tot = [], [], [], []
            for h in (h0, h0 + 1):
                hl = direction * SSD_HEADS + h
                col = jnp.broadcast_to(cum[:, hl:hl + 1], (q, q))
                arg = jnp.where(tri, col - src_t[hl:hl + 1, :], -jnp.inf)
                scores.append((cb * jnp.exp(arg)).astype(BF16))
                b_w.append((b_t * w_t[hl:hl + 1, :]).astype(BF16))
                cols.append(col)
                tot.append(jnp.broadcast_to(last_row[:, hl:hl + 1], (1, LANES)))
            s_pair = st_ref[:, lo:lo + LANES]
            y = _dot(jnp.concatenate(scores, axis=1), x_blk)
            y = y + _dot(c_bf, s_pair.astype(BF16)) * jnp.exp(jnp.where(first_half, cols[0], cols[1]))
            if d_row is not None:
                y = y + x_pair * d_row[:, lo:lo + LANES]
            ys.append(y)
            decay = jnp.exp(jnp.where(first_half_row, tot[0], tot[1]))
            st_ref[:, lo:lo + LANES] = s_pair * decay + _dot(jnp.concatenate(b_w, axis=1), x_blk)
    return jnp.concatenate(ys, axis=-1)


def _ssd_kernel(xf_ref, xb_ref, dtf_ref, dtb_ref, cumf_ref, cumb_ref, d_ref, h0_ref, yf_ref, yb_ref, hf_ref, st_ref):
    c = pl.program_id(1)
    nc = pl.num_programs(1)

    @pl.when(c == 0)
    def _():
        st_ref[...] = h0_ref[0]

    yf_ref[0] = _ssd_direction(xf_ref[0], dtf_ref[0], cumf_ref[0], st_ref.at[0], 0, d_ref[...])
    yb_ref[0] = _ssd_direction(xb_ref[0], dtb_ref[0], cumb_ref[0], st_ref.at[1], 1, None)

    @pl.when(c == nc - 1)
    def _():
        hf_ref[0] = st_ref[...]


def _ssd_scan(xbc, dt, cum, d_exp, h0):
    bsz, seq, width = xbc.shape
    q = SSD_CHUNK
    nc = seq // q
    fwd = lambda w: pl.BlockSpec((1, q, w), lambda b, c: (b, c, 0))
    bwd = lambda w: pl.BlockSpec((1, q, w), lambda b, c: (b, nc - 1 - c, 0))
    st_spec = pl.BlockSpec((1, 2, SSD_STATE, D_INNER), lambda b, c: (b, 0, 0, 0))
    return pl.pallas_call(
        _ssd_kernel,
        grid=(bsz, nc),
        in_specs=[fwd(width), bwd(width), fwd(LANES), bwd(LANES), fwd(LANES), bwd(LANES),
                  pl.BlockSpec((1, D_INNER), lambda b, c: (0, 0)), st_spec],
        out_specs=[fwd(D_INNER), bwd(D_INNER), st_spec],
        out_shape=[
            jax.ShapeDtypeStruct((bsz, seq, D_INNER), F32),
            jax.ShapeDtypeStruct((bsz, seq, D_INNER), F32),
            jax.ShapeDtypeStruct((bsz, 2, SSD_STATE, D_INNER), F32),
        ],
        scratch_shapes=[pltpu.VMEM((2, SSD_STATE, D_INNER), F32)],
        compiler_params=_params("arbitrary", "arbitrary"),
        name="ssd_scan",
    )(xbc, xbc, dt, dt, cum, cum, d_exp, h0)


def _odd_tail_kernel(x_ref, mod_ref, yf_ref, yb_ref, z_ref, nw_ref, wo_ref, fw_ref, o_ref, *, final):
    y = (yf_ref[0] + yb_ref[0]) * _silu(z_ref[0])
    y = _rms(y, nw_ref[...]).astype(BF16)
    xn = x_ref[0] + mod_ref[0][:, 2 * D_MODEL:] * _dot(y, wo_ref[...])
    o_ref[0] = _rms(xn, fw_ref[...]) if final else xn


def _odd_tail(x, mod, yf, yb, z, norm_w, w_out, final_w, final, tm=256):
    bsz, seq, _ = x.shape
    tok = lambda w: pl.BlockSpec((1, tm, w), lambda b, m: (b, m, 0))
    const = lambda r, c: pl.BlockSpec((r, c), lambda b, m: (0, 0))
    return pl.pallas_call(
        functools.partial(_odd_tail_kernel, final=final),
        grid=(bsz, seq // tm),
        in_specs=[tok(D_MODEL), _mod_spec(mod), tok(D_INNER), tok(D_INNER), tok(D_INNER),
                  const(1, D_INNER), const(D_INNER, D_MODEL), const(1, D_MODEL)],
        out_specs=tok(D_MODEL),
        out_shape=jax.ShapeDtypeStruct((bsz, seq, D_MODEL), F32),
        compiler_params=_params("arbitrary", "arbitrary"),
        name="odd_tail",
    )(x, mod, yf, yb, z, norm_w.reshape(1, -1), w_out, final_w.reshape(1, -1))


def _even_layer(x, mod, norm_w, w_in, w_out, s5, s5_d, glu_w, glu_b, short_w, short_b, dft, kspec, h0,
                final_w, final, s5_tq, conv_tc):
    tm = min(TOKEN_BLOCK, x.shape[1])
    outs = ((S5_WIDTH, F32, False), (S5_WIDTH, F32, False),
            (HY_WIDTH, BF16, True), (HY_WIDTH, F32, True), (HY_WIDTH, F32, True))
    u, g_a, v, x1, x2g = _inproj(_inproj_even_kernel, x, mod, norm_w, w_in,
                                 (_phase_perm(tm), short_w, short_b.reshape(1, -1)), outs, tm=tm)
    ys, hf = _s5_scan(u, *s5, h0, s5_tq)
    fwd, inv = dft
    z1 = _hy_conv(v, x1, fwd, inv, kspec, 0, conv_tc)
    hy = _hy_conv(z1, x2g, fwd, inv, kspec, 1, conv_tc)
    xn = _even_tail(x, mod, ys, u, g_a, hy, s5_d, glu_w, glu_b, w_out, final_w, final, tm=tm)
    return xn, hf


def _odd_layer(x, mod, norm_w, w_in, conv_w, conv_b, a_log, dt_bias, d_exp, ssd_norm_w, w_out, h0, final_w, final):
    tm = min(TOKEN_BLOCK, x.shape[1])
    z, xbc, dt, cum = _inproj(functools.partial(_inproj_odd_kernel, tm=tm), x, mod, norm_w, w_in,
                              (conv_w, conv_b.reshape(1, -1), dt_bias, a_log),
                              tuple((w, F32, False) for w in (D_INNER, SSD_CONV_DIM, LANES, LANES)), tm=tm)
    yf, yb, hf = _ssd_scan(xbc, dt, cum, d_exp, h0)
    xn = _odd_tail(x, mod, yf, yb, z, ssd_norm_w, w_out, final_w, final)
    return xn, hf


def _ssd_state_to_lanes(st):
    bsz = st.shape[0]
    return jnp.transpose(st.astype(F32), (0, 1, 4, 2, 3)).reshape(bsz, 2, SSD_STATE, D_INNER)


def _ssd_state_from_lanes(st):
    bsz = st.shape[0]
    return jnp.transpose(st.reshape(bsz, 2, SSD_STATE, SSD_HEADS, SSD_HEADDIM), (0, 1, 3, 4, 2))


def kernel(x_prompt, x_sample, state_s5_re, state_s5_im, state_ssd, c, c_ctx, mod_w, mod_b, norm_w, final_norm_w,
           ev_w_in, ev_w_out, s5_a_re, s5_a_im, s5_log_dt, s5_b_re, s5_b_im, s5_c_re, s5_c_im, s5_d, s5_glu_w,
           s5_glu_b, hy_short_w, hy_short_b, hy_f_w1, hy_f_b1, hy_f_freq1, hy_f_w2, hy_f_b2, hy_f_freq2, hy_f_w3,
           hy_bias, ssd_w_in, ssd_conv_w, ssd_conv_b, ssd_a_log, ssd_dt_bias, ssd_d, ssd_norm_w, ssd_w_out):
    bp, lp, _ = x_prompt.shape
    bs, ls, _ = x_sample.shape
    rows = 2 * SUBLANES
    cvec = jnp.zeros((rows, D_MODEL), F32).at[0].set(c_ctx.astype(F32)).at[1:1 + bs].set(c.astype(F32))
    mod = _modulation(cvec, mod_w, mod_b)
    dft_p, dft_s = _dft_matrices(lp // HY_PHASES), _dft_matrices(ls // HY_PHASES)
    xp, xs = x_prompt, x_sample
    new_re, new_im, new_ssd = [], [], []
    for i in range(DEPTH):
        j = i // 2
        final = i == DEPTH - 1
        mod_p = mod[i, 0:1].reshape(1, 1, 3 * D_MODEL)
        mod_s = mod[i, 1:1 + bs].reshape(bs, 1, 3 * D_MODEL)
        if i % 2 == 0:
            s5 = _s5_prepare(s5_a_re[j], s5_a_im[j], s5_log_dt[j], s5_b_re[j], s5_b_im[j], s5_c_re[j], s5_c_im[j])
            w_in = ev_w_in[j].astype(BF16)
            w_out = ev_w_out[j].astype(BF16)
            glu_w = s5_glu_w[j].astype(BF16)
            fargs = (hy_f_w1[j], hy_f_b1[j], hy_f_freq1[j], hy_f_w2[j], hy_f_b2[j], hy_f_freq2[j], hy_f_w3[j])
            ks_p = _hy_spectra(_polyphase_filters(_hy_filters(lp, *fargs)), dft_p[0], hy_bias[j])
            ks_s = _hy_spectra(_polyphase_filters(_hy_filters(ls, *fargs)), dft_s[0], hy_bias[j])
            h0_p = jnp.zeros((2, bp, S5_STATE_LANES), F32)
            h0_s = _s5_state_to_lanes(state_s5_re[:, j].astype(F32), state_s5_im[:, j].astype(F32))
            common = (s5_d[j], glu_w, s5_glu_b[j], hy_short_w[j], hy_short_b[j])
            xp, hf = _even_layer(xp, mod_p, norm_w[i], w_in, w_out, s5, *common, dft_p, ks_p, h0_p,
                                 final_norm_w, final, s5_tq=32, conv_tc=512)
            xs, _ = _even_layer(xs, mod_s, norm_w[i], w_in, w_out, s5, *common, dft_s, ks_s, h0_s,
                                final_norm_w, final, s5_tq=64, conv_tc=256)
            re, im = _s5_state_from_lanes(hf)
            new_re.append(re)
            new_im.append(im)
        else:
            pad = LANES - 2 * SSD_HEADS
            w_in = jnp.pad(ssd_w_in[j], ((0, 0), (0, pad))).astype(BF16)
            w_out = ssd_w_out[j].astype(BF16)
            dt_bias = jnp.pad(ssd_dt_bias[j].reshape(1, -1).astype(F32), ((0, 0), (0, pad)))
            a_log = jnp.pad(ssd_a_log[j].reshape(1, -1).astype(F32), ((0, 0), (0, pad)))
            d_exp = jnp.repeat(ssd_d[j].astype(F32), SSD_HEADDIM).reshape(1, D_INNER)
            common = (ssd_conv_w[j], ssd_conv_b[j], a_log, dt_bias, d_exp, ssd_norm_w[j], w_out)
            h0_p = jnp.zeros((bp, 2, SSD_STATE, D_INNER), F32)
            h0_s = _ssd_state_to_lanes(state_ssd[:, j])
            xp, hf = _odd_layer(xp, mod_p, norm_w[i], w_in, *common, h0_p, final_norm_w, final)
            xs, _ = _odd_layer(xs, mod_s, norm_w[i], w_in, *common, h0_s, final_norm_w, final)
            new_ssd.append(_ssd_state_from_lanes(hf))
    return (xp, xs, jnp.stack(new_re, axis=1), jnp.stack(new_im, axis=1), jnp.stack(new_ssd, axis=1))
```

```python
import functools
import math

import jax
import jax.numpy as jnp
from jax import lax
from jax.experimental import pallas as pl
from jax.experimental.pallas import tpu as pltpu

F32 = jnp.float32
BF16 = jnp.bfloat16
HIGHEST = lax.Precision.HIGHEST

D_MODEL = 1024
DEPTH = 4
D_INNER = 2048
EPS = 1e-6
S5_WIDTH = 1024
S5_GROUP = 16
S5_GROUPS = 64
S5_STATE = 64
HY_WIDTH = 1024
HY_BANDS = 16
HY_HID = 64
HY_DECAY_TARGET = 1e-2
HY_FAST = 0.3
HY_SLOW = 1.5
HY_PHASES = 4
SSD_HEADDIM = 64
SSD_HEADS = 32
SSD_STATE = 128
SSD_GROUPS = 4
SSD_CHUNK = 128
SSD_CONV_DIM = D_INNER + 2 * SSD_GROUPS * SSD_STATE

LANES = 128
SUBLANES = 8
VMEM_LIMIT = 56 * 1024 * 1024

S5_LANE_CHUNKS = S5_WIDTH // LANES
S5_CHUNK_STATE = (LANES // S5_GROUP) * S5_STATE
S5_STATE_LANES = S5_LANE_CHUNKS * 2 * S5_CHUNK_STATE
S5_PERM_ROWS = 256


def _params(*sem):
    return pltpu.CompilerParams(dimension_semantics=sem, vmem_limit_bytes=VMEM_LIMIT)


def _silu(x):
    return x * jax.nn.sigmoid(x)


def _dot(a, b):
    return jnp.dot(a, b, preferred_element_type=F32)


def _dot_hi(a, b):
    return jnp.dot(a, b, preferred_element_type=F32, precision=HIGHEST)


def _mod_kernel(c_ref, w_ref, b_ref, o_ref):
    s = _silu(c_ref[...])
    o_ref[0] = _dot_hi(s, w_ref[0]) + b_ref[0]


def _modulation(cvec, mod_w, mod_b):
    rows = cvec.shape[0]
    tn = D_MODEL
    return pl.pallas_call(
        _mod_kernel,
        grid=(DEPTH, 3 * D_MODEL // tn),
        in_specs=[
            pl.BlockSpec((rows, D_MODEL), lambda i, n: (0, 0)),
            pl.BlockSpec((1, D_MODEL, tn), lambda i, n: (i, 0, n)),
            pl.BlockSpec((1, 1, tn), lambda i, n: (i, 0, n)),
        ],
        out_specs=pl.BlockSpec((1, rows, tn), lambda i, n: (i, 0, n)),
        out_shape=jax.ShapeDtypeStruct((DEPTH, rows, 3 * D_MODEL), F32),
        compiler_params=_params("arbitrary", "arbitrary"),
        name="modulation",
    )(cvec, mod_w, mod_b.reshape(DEPTH, 1, 3 * D_MODEL))


def _norm_mod(x, mod, nw):
    shift = mod[:, :D_MODEL]
    scale = mod[:, D_MODEL:2 * D_MODEL]
    ms = jnp.mean(x * x, axis=-1, keepdims=True)
    return (x * lax.rsqrt(ms + EPS) * nw * (1.0 + scale) + shift).astype(BF16)


def _conv3(x, prev_row, next_row, w, b):
    rows = x.shape[0]
    ridx = lax.broadcasted_iota(jnp.int32, x.shape, 0)
    xp = jnp.where(ridx == 0, prev_row, pltpu.roll(x, 1, 0))
    xn = jnp.where(ridx == rows - 1, next_row, pltpu.roll(x, rows - 1, 0))
    return w[0:1] * xp + w[1:2] * x + w[2:3] * xn + b


def _conv3_phased(x, prev_row, next_row, w, b):
    ph = x.shape[0] // HY_PHASES
    blk = [x[r * ph:(r + 1) * ph] for r in range(HY_PHASES)]
    ridx = lax.broadcasted_iota(jnp.int32, blk[0].shape, 0)
    before = [jnp.where(ridx == 0, prev_row, pltpu.roll(blk[-1], 1, 0))] + blk[:-1]
    after = blk[1:] + [jnp.where(ridx == ph - 1, next_row, pltpu.roll(blk[0], ph - 1, 0))]
    return jnp.concatenate([w[0:1] * before[r] + w[1:2] * blk[r] + w[2:3] * after[r] + b
                            for r in range(HY_PHASES)], axis=0)


CONV_COLS = 512
TOKEN_BLOCK = 512


def _proj_conv3(conv, hb, pv_ref, nx_ref, mod, nw, w_ref, lo, width, cw_ref, cb_ref):
    m = pl.program_id(1)
    nm = pl.num_programs(1)
    h_prev = _norm_mod(pv_ref[0], mod, nw)
    h_next = _norm_mod(nx_ref[0], mod, nw)
    has_prev, has_next = (m > 0).astype(F32), (m < nm - 1).astype(F32)
    for off in range(0, width, CONV_COLS):
        w = w_ref[:, lo + off:lo + off + CONV_COLS]
        prev = _dot(h_prev, w)[SUBLANES - 1:SUBLANES] * has_prev
        nxt = _dot(h_next, w)[0:1] * has_next
        yield off, conv(_dot(hb, w), prev, nxt, cw_ref[:, off:off + CONV_COLS], cb_ref[:, off:off + CONV_COLS])


def _inproj_even_kernel(x_ref, pv_ref, nx_ref, mod_ref, nw_ref, w_ref, perm_ref, cw_ref, cb_ref,
                        u_ref, ga_ref, v_ref, x1_ref, x2_ref):
    mod, nw = mod_ref[0], nw_ref[...]
    hb = _norm_mod(x_ref[0], mod, nw)
    lo_vx, hi_vx = 2 * S5_WIDTH, 2 * S5_WIDTH + 3 * HY_WIDTH
    u_ref[0] = _dot(hb, w_ref[:, :S5_WIDTH])
    ga_ref[0] = _dot(hb, w_ref[:, S5_WIDTH:lo_vx]).astype(ga_ref.dtype)
    hbp = _dot(perm_ref[...], hb).astype(BF16)
    ph = hb.shape[0] // HY_PHASES
    outs = (v_ref, x1_ref, x2_ref)
    for off, y in _proj_conv3(_conv3_phased, hbp, pv_ref, nx_ref, mod, nw, w_ref, lo_vx, 3 * HY_WIDTH,
                              cw_ref, cb_ref):
        which, col = off // HY_WIDTH, off % HY_WIDTH
        if which == 2:
            y = y * _silu(_dot(hbp, w_ref[:, hi_vx + col:hi_vx + col + CONV_COLS]))
        o_ref = outs[which]
        o_ref[0, :, :, col:col + CONV_COLS] = y.reshape(HY_PHASES, ph, CONV_COLS).astype(o_ref.dtype)


def _softplus(x):
    return jnp.maximum(x, 0.0) + jnp.log(1.0 + jnp.exp(-jnp.abs(x)))


def _inproj_odd_kernel(x_ref, pv_ref, nx_ref, mod_ref, nw_ref, w_ref, cw_ref, cb_ref, bias_ref, alog_ref,
                       z_ref, xbc_ref, dt_ref, cum_ref, *, tm):
    mod, nw = mod_ref[0], nw_ref[...]
    hb = _norm_mod(x_ref[0], mod, nw)
    lo_dt = D_INNER + SSD_CONV_DIM
    z_ref[0] = _dot(hb, w_ref[:, :D_INNER]).astype(z_ref.dtype)
    for off, y in _proj_conv3(_conv3, hb, pv_ref, nx_ref, mod, nw, w_ref, D_INNER, SSD_CONV_DIM, cw_ref, cb_ref):
        xbc_ref[0, :, off:off + CONV_COLS] = _silu(y).astype(xbc_ref.dtype)
    dt = _softplus(_dot(hb, w_ref[:, lo_dt:]) + bias_ref[...])
    dt_ref[0] = dt
    da = dt * (-jnp.exp(alog_ref[...]))
    q = SSD_CHUNK
    ii = lax.broadcasted_iota(jnp.int32, (q, q), 0)
    jj = lax.broadcasted_iota(jnp.int32, (q, q), 1)
    low, up = (jj <= ii).astype(F32), (jj >= ii).astype(F32)
    fwd_lane = lax.broadcasted_iota(jnp.int32, (q, LANES), 1) < SSD_HEADS
    for c in range(tm // q):
        dac = da[c * q:(c + 1) * q]
        cum_ref[0, c * q:(c + 1) * q, :] = jnp.where(fwd_lane, _dot_hi(low, dac), _dot_hi(up, dac))


def _mod_spec(mod):
    per_batch = mod.shape[0] > 1
    return pl.BlockSpec((1, 1, mod.shape[-1]), (lambda b, m: (b, 0, 0)) if per_batch else (lambda b, m: (0, 0, 0)))


def _halo_specs(width, tl, seq):
    r = tl // SUBLANES
    last = seq // SUBLANES - 1
    return (pl.BlockSpec((1, SUBLANES, width), lambda b, m: (b, jnp.maximum(m * r - 1, 0), 0)),
            pl.BlockSpec((1, SUBLANES, width), lambda b, m: (b, jnp.minimum((m + 1) * r, last), 0)))


def _phase_perm(tm):
    ph = tm // HY_PHASES
    i = jnp.arange(tm, dtype=jnp.int32)
    src = HY_PHASES * (i % ph) + i // ph
    return (src[:, None] == i[None, :]).astype(BF16)


def _phased_spec(tm, width):
    return pl.BlockSpec((1, HY_PHASES, tm // HY_PHASES, width), lambda b, m: (b, 0, m, 0))


def _inproj(kernel_fn, x, mod, norm_w, w_bf16, consts, outs, tm=256):
    bsz, seq, _ = x.shape
    pv, nx = _halo_specs(D_MODEL, tm, seq)
    const = lambda a: pl.BlockSpec(a.shape, lambda b, m: (0, 0))
    out_specs, out_shape = [], []
    for width, dtype, phased in outs:
        if phased:
            out_specs.append(_phased_spec(tm, width))
            out_shape.append(jax.ShapeDtypeStruct((bsz, HY_PHASES, seq // HY_PHASES, width), dtype))
        else:
            out_specs.append(pl.BlockSpec((1, tm, width), lambda b, m: (b, m, 0)))
            out_shape.append(jax.ShapeDtypeStruct((bsz, seq, width), dtype))
    return pl.pallas_call(
        kernel_fn,
        grid=(bsz, seq // tm),
        in_specs=[pl.BlockSpec((1, tm, D_MODEL), lambda b, m: (b, m, 0)), pv, nx, _mod_spec(mod),
                  pl.BlockSpec((1, D_MODEL), lambda b, m: (0, 0)),
                  pl.BlockSpec(w_bf16.shape, lambda b, m: (0, 0), pipeline_mode=pl.Buffered(1))]
                 + [const(a) for a in consts],
        out_specs=out_specs,
        out_shape=out_shape,
        compiler_params=_params("arbitrary", "arbitrary"),
        name="inproj",
    )(x, x, x, mod, norm_w.reshape(1, D_MODEL), w_bf16, *consts)


def _s5_kernel(u_ref, pin_ref, pout_ref, bw_ref, cw_ref, lr_ref, li_ref, h0_ref, ys_ref, hf_ref, bu_ref, hst_ref,
               *, bsz, tq, jgroup):
    d = pl.program_id(0)
    blk = pl.program_id(1)
    nblk = pl.num_programs(1)
    cs = S5_CHUNK_STATE
    nsub = cs // LANES
    prow = pin_ref.shape[1]
    tsub = prow // bsz
    ns = tq // tsub

    @pl.when(blk == 0)
    def _():
        hst_ref[...] = h0_ref[0]

    def pick(blocks, s):
        return blocks[s] if ns == 1 else jnp.where(d == 0, blocks[s], blocks[ns - 1 - s])

    u_nat = [u_ref[:, s * tsub:(s + 1) * tsub, :].reshape(prow, S5_WIDTH) for s in range(ns)]
    u_tm = jnp.concatenate([_dot(pin_ref[0], pick(u_nat, s).astype(BF16)).astype(BF16) for s in range(ns)], axis=0)
    for j in range(S5_LANE_CHUNKS):
        bu_ref[:, 2 * cs * j:2 * cs * (j + 1)] = _dot(u_tm[:, LANES * j:LANES * (j + 1)], bw_ref[0, j])

    for j0 in range(0, S5_LANE_CHUNKS, jgroup):
        cols = [(2 * cs * j + LANES * c, 2 * cs * j + cs + LANES * c, j, c)
                for j in range(j0, j0 + jgroup) for c in range(nsub)]
        lam = [(jnp.broadcast_to(lr_ref[0, j, :, LANES * c:LANES * (c + 1)], (bsz, LANES)),
                jnp.broadcast_to(li_ref[0, j, :, LANES * c:LANES * (c + 1)], (bsz, LANES))) for _, _, j, c in cols]
        state = [(hst_ref[:, re_lo:re_lo + LANES], hst_ref[:, im_lo:im_lo + LANES]) for re_lo, im_lo, _, _ in cols]
        for step in range(tq):
            r0 = step * bsz
            for i, (re_lo, im_lo, _, _) in enumerate(cols):
                hr, hi = state[i]
                lam_re, lam_im = lam[i]
                nr = lam_re * hr - lam_im * hi + bu_ref[r0:r0 + bsz, re_lo:re_lo + LANES]
                ni = lam_re * hi + lam_im * hr + bu_ref[r0:r0 + bsz, im_lo:im_lo + LANES]
                bu_ref[r0:r0 + bsz, re_lo:re_lo + LANES] = nr
                bu_ref[r0:r0 + bsz, im_lo:im_lo + LANES] = ni
                state[i] = (nr, ni)
        for i, (re_lo, im_lo, _, _) in enumerate(cols):
            hst_ref[:, re_lo:re_lo + LANES] = state[i][0]
            hst_ref[:, im_lo:im_lo + LANES] = state[i][1]

    y_tm = jnp.concatenate([_dot(bu_ref[:, 2 * cs * j:2 * cs * (j + 1)].astype(BF16), cw_ref[0, j])
                            for j in range(S5_LANE_CHUNKS)], axis=-1)
    y_grp = [y_tm[s * prow:(s + 1) * prow] for s in range(ns)]
    for s in range(ns):
        y = pick(y_grp, s)
        y_hi = y.astype(BF16)
        y_lo = (y - y_hi.astype(F32)).astype(BF16)
        y_bm = _dot(pout_ref[0], y_hi) + _dot(pout_ref[0], y_lo)
        ys_ref[0, :, s * tsub:(s + 1) * tsub, :] = y_bm.reshape(bsz, tsub, S5_WIDTH).astype(ys_ref.dtype)

    @pl.when(blk == nblk - 1)
    def _():
        hf_ref[0] = hst_ref[...]


def _s5_scan(u, bw, cw, lr, li, h0, tq):
    bsz, seq, _ = u.shape
    nblk = seq // tq
    rows = bsz * tq
    prow = min(rows, S5_PERM_ROWS)
    tsub = prow // bsz
    r = jnp.arange(prow, dtype=jnp.int32)
    step, b = r // bsz, r % bsz
    src = jnp.stack([b * tsub + step, b * tsub + (tsub - 1 - step)])
    p_in = (src[:, :, None] == r[None, None, :]).astype(BF16)
    p_out = jnp.transpose(p_in, (0, 2, 1))

    def tblk(d, i):
        return i + d * (nblk - 1 - 2 * i)

    nj, cs = S5_LANE_CHUNKS, S5_CHUNK_STATE
    jgroup = 2 if bsz <= SUBLANES else 1
    return pl.pallas_call(
        functools.partial(_s5_kernel, bsz=bsz, tq=tq, jgroup=jgroup),
        grid=(2, nblk),
        in_specs=[
            pl.BlockSpec((bsz, tq, S5_WIDTH), lambda d, i: (0, tblk(d, i), 0)),
            pl.BlockSpec((1, prow, prow), lambda d, i: (d, 0, 0)),
            pl.BlockSpec((1, prow, prow), lambda d, i: (d, 0, 0)),
            pl.BlockSpec((1, nj, LANES, 2 * cs), lambda d, i: (d, 0, 0, 0)),
            pl.BlockSpec((1, nj, 2 * cs, LANES), lambda d, i: (d, 0, 0, 0)),
            pl.BlockSpec((1, nj, 1, cs), lambda d, i: (d, 0, 0, 0)),
            pl.BlockSpec((1, nj, 1, cs), lambda d, i: (d, 0, 0, 0)),
            pl.BlockSpec((1, bsz, S5_STATE_LANES), lambda d, i: (d, 0, 0)),
        ],
        out_specs=[
            pl.BlockSpec((1, bsz, tq, S5_WIDTH), lambda d, i: (d, 0, tblk(d, i), 0)),
            pl.BlockSpec((1, bsz, S5_STATE_LANES), lambda d, i: (d, 0, 0)),
        ],
        out_shape=[
            jax.ShapeDtypeStruct((2, bsz, seq, S5_WIDTH), BF16),
            jax.ShapeDtypeStruct((2, bsz, S5_STATE_LANES), F32),
        ],
        scratch_shapes=[
            pltpu.VMEM((rows, S5_STATE_LANES), F32),
            pltpu.VMEM((bsz, S5_STATE_LANES), F32),
        ],
        compiler_params=_params("arbitrary", "arbitrary"),
        name="s5_scan",
    )(u, p_in, p_out, bw, cw, lr, li, h0)


def _s5_prepare(a_re, a_im, log_dt, b_re, b_im, c_re, c_im):
    dt = jnp.exp(log_dt.astype(F32))[..., None]
    a_re, a_im = a_re.astype(F32), a_im.astype(F32)
    mag = jnp.exp(a_re * dt)
    lbr, lbi = mag * jnp.cos(a_im * dt), mag * jnp.sin(a_im * dt)
    den = a_re * a_re + a_im * a_im
    cr = ((lbr - 1.0) * a_re + lbi * a_im) / den
    ci = (lbi * a_re - (lbr - 1.0) * a_im) / den
    bbr = cr[..., None] * b_re - ci[..., None] * b_im
    bbi = cr[..., None] * b_im + ci[..., None] * b_re
    nj, gpc = S5_LANE_CHUNKS, LANES // S5_GROUP
    eye = jnp.eye(gpc, dtype=F32)

    def pack_b(m):
        m = m.reshape(2, nj, gpc, S5_STATE, S5_GROUP)
        return jnp.einsum('djgph,gk->djghkp', m, eye).reshape(2, nj, LANES, S5_CHUNK_STATE)

    def pack_c(m):
        m = m.reshape(2, nj, gpc, S5_GROUP, S5_STATE)
        return jnp.einsum('djghp,gk->djgpkh', m, eye).reshape(2, nj, S5_CHUNK_STATE, LANES)

    bw = jnp.concatenate([pack_b(bbr), pack_b(bbi)], axis=-1).astype(BF16)
    cw = jnp.concatenate([pack_c(c_re.astype(F32)), pack_c(-c_im.astype(F32))], axis=-2).astype(BF16)
    lr = lbr.reshape(2, nj, 1, S5_CHUNK_STATE)
    li = lbi.reshape(2, nj, 1, S5_CHUNK_STATE)
    return bw, cw, lr, li


def _s5_state_to_lanes(re, im):
    bsz = re.shape[0]
    st = jnp.stack([re.reshape(bsz, 2, S5_LANE_CHUNKS, S5_CHUNK_STATE),
                    im.reshape(bsz, 2, S5_LANE_CHUNKS, S5_CHUNK_STATE)], axis=3)
    return jnp.transpose(st.reshape(bsz, 2, S5_STATE_LANES), (1, 0, 2))


def _s5_state_from_lanes(hf):
    bsz = hf.shape[1]
    st = jnp.transpose(hf, (1, 0, 2)).reshape(bsz, 2, S5_LANE_CHUNKS, 2, S5_CHUNK_STATE)
    re = st[:, :, :, 0].reshape(bsz, 2, S5_GROUPS, S5_STATE)
    im = st[:, :, :, 1].reshape(bsz, 2, S5_GROUPS, S5_STATE)
    return re, im


def _hy_filter_kernel(bands_ref, deltas_ref, w1t_ref, w1c_ref, w1s_ref, b1_ref, f1_ref, w2_ref, b2_ref, f2_ref,
                      w3_ref, o_ref, *, seq, tl):
    r = pl.program_id(0)
    m = pl.program_id(1)
    t = (HY_PHASES * (lax.broadcasted_iota(jnp.int32, (tl, 1), 0) + m * tl) + r).astype(F32)
    t_norm = t / (seq - 1)
    ang = (2.0 * math.pi / seq) * t * bands_ref[...]
    pre = t_norm * w1t_ref[...] + _dot_hi(jnp.cos(ang), w1c_ref[...]) + _dot_hi(jnp.sin(ang), w1s_ref[...])
    f = jnp.sin(f1_ref[...] * (pre + b1_ref[...]))
    f = jnp.sin(f2_ref[...] * (_dot_hi(f, w2_ref[...]) + b2_ref[...]))
    f = _dot_hi(f, w3_ref[...])
    window = jnp.exp(-t_norm * deltas_ref[...])
    for o in range(2):
        for direction in range(2):
            lo = (2 * o + direction) * HY_WIDTH
            o_ref[o, direction, 0] = f[:, lo:lo + HY_WIDTH] * window


def _hy_filters(seq, w1, b1, freq1, w2, b2, freq2, w3):
    n4 = seq // HY_PHASES
    tl = min(256, n4)
    bands = jnp.linspace(1e-4, HY_BANDS - 1, HY_BANDS, dtype=F32).reshape(1, HY_BANDS)
    deltas = jnp.abs(jnp.linspace(math.log(HY_DECAY_TARGET) / HY_SLOW, math.log(HY_DECAY_TARGET) / HY_FAST,
                                  HY_WIDTH, dtype=F32)).reshape(1, HY_WIDTH)
    small = [bands, deltas, w1[0:1], w1[1:1 + HY_BANDS], w1[1 + HY_BANDS:], b1.reshape(1, -1), freq1.reshape(1, -1),
             w2, b2.reshape(1, -1), freq2.reshape(1, -1), w3]
    return pl.pallas_call(
        functools.partial(_hy_filter_kernel, seq=seq, tl=tl),
        grid=(HY_PHASES, n4 // tl),
        in_specs=[pl.BlockSpec(a.shape, lambda r, m: (0, 0)) for a in small],
        out_specs=pl.BlockSpec((2, 2, 1, tl, HY_WIDTH), lambda r, m: (0, 0, r, m, 0)),
        out_shape=jax.ShapeDtypeStruct((2, 2, HY_PHASES, n4, HY_WIDTH), F32),
        compiler_params=_params("arbitrary", "arbitrary"),
        name="hy_filters",
    )(*small)


def _polyphase_filters(hf):
    fwd, bwd = hf[:, 0], hf[:, 1]
    zero = jnp.zeros_like(fwd[:, 0, :1])

    def lags(h, first, start):
        r, j0 = start % HY_PHASES, start // HY_PHASES
        tail = h[:, r, j0:] if j0 else h[:, r, :-1]
        return jnp.concatenate([first, tail], axis=1)

    out = []
    for q in range(1 - HY_PHASES, HY_PHASES):
        pos = fwd[:, q] if q >= 0 else lags(fwd, bwd[:, -q, :1], HY_PHASES + q)
        neg = lags(bwd, zero, HY_PHASES - q)
        out.append(jnp.stack([pos + neg, pos - neg], axis=1))
    return jnp.stack(out, axis=1).astype(BF16)


def _dft_matrices(seq):
    n = 2 * seq
    k = jnp.arange(seq, dtype=jnp.int32)
    ang = ((k[:, None] * k[None, :]) % n).astype(F32) * (2.0 * math.pi / n)
    cos, sin = jnp.cos(ang), jnp.sin(ang)
    alt = jnp.where(k % 2 == 0, 1.0, -1.0).astype(F32)
    fwd = jnp.concatenate([cos, jnp.where(k[:, None] == 0, alt[None, :], -sin)], axis=0)
    scale = jnp.where(k == 0, 1.0 / n, 2.0 / n).astype(F32)
    inv = jnp.concatenate([cos * scale[None, :], jnp.where(k[None, :] == 0, alt[:, None], -sin) * scale[None, :]],
                          axis=1)
    return fwd.astype(BF16), inv.astype(BF16)


def _hy_spec_kernel(fwd_ref, h_ref, bias_ref, o_ref, *, half):
    q = pl.program_id(1)
    f = fwd_ref[...]
    s_sum = _dot(f, h_ref[0, 0, 0])
    s_dif = _dot(f, h_ref[0, 0, 1])
    ridx = lax.broadcasted_iota(jnp.int32, s_sum.shape, 0)
    real_row = ridx <= half
    lag0 = jnp.where(real_row & (q == HY_PHASES - 1), bias_ref[0], 0.0)
    o_ref[0, 0] = jnp.where(real_row, s_sum, s_dif) + lag0


def _hy_spectra(hs, fwd, bias, tc=512):
    _, nq, _, n4, width = hs.shape
    return pl.pallas_call(
        functools.partial(_hy_spec_kernel, half=n4),
        grid=(2, nq, width // tc),
        in_specs=[
            pl.BlockSpec((2 * n4, n4), lambda o, q, c: (0, 0)),
            pl.BlockSpec((1, 1, 2, n4, tc), lambda o, q, c: (o, q, 0, 0, c)),
            pl.BlockSpec((1, 1, tc), lambda o, q, c: (o, 0, c)),
        ],
        out_specs=pl.BlockSpec((1, 1, 2 * n4, tc), lambda o, q, c: (o, q, 0, c)),
        out_shape=jax.ShapeDtypeStruct((2, nq, 2 * n4, width), F32),
        compiler_params=_params("arbitrary", "arbitrary", "arbitrary"),
        name="hy_spectra",
    )(fwd, hs, bias.reshape(2, 1, width))


def _hy_conv_kernel(v_ref, g_ref, fwd_ref, inv_ref, ks_ref, o_ref, *, half):
    f = fwd_ref[...]
    row0 = lax.broadcasted_iota(jnp.int32, (half, v_ref.shape[-1]), 0) == 0

    def split(s):
        s_im = s[half:]
        return s[:half], jnp.where(row0, 0.0, s_im), s_im[0:1]

    spec = [split(_dot(f, v_ref[0, r].astype(BF16))) for r in range(HY_PHASES)]
    filt = [split(ks_ref[0, q]) for q in range(2 * HY_PHASES - 1)]
    for r in range(HY_PHASES):
        p_re = p_im = p_ny = None
        for rp in range(HY_PHASES):
            s_re, s_im, s_ny = spec[rp]
            k_re, k_im, k_ny = filt[r - rp + HY_PHASES - 1]
            t_re = s_re * k_re - s_im * k_im
            t_im = s_re * k_im + s_im * k_re
            t_ny = s_ny * k_ny
            p_re = t_re if p_re is None else p_re + t_re
            p_im = t_im if p_im is None else p_im + t_im
            p_ny = t_ny if p_ny is None else p_ny + t_ny
        p = jnp.concatenate([p_re, jnp.where(row0, p_ny, p_im)], axis=0).astype(BF16)
        o_ref[0, r] = (g_ref[0, r] * _dot(inv_ref[...], p)).astype(o_ref.dtype)


def _hy_conv(v, gate, fwd, inv, kspec, order, tc):
    bsz, _, n4, width = v.shape
    nq = 2 * HY_PHASES - 1
    blk = pl.BlockSpec((1, HY_PHASES, n4, tc), lambda c, b: (b, 0, 0, c))
    return pl.pallas_call(
        functools.partial(_hy_conv_kernel, half=n4),
        grid=(width // tc, bsz),
        in_specs=[
            blk, blk,
            pl.BlockSpec((2 * n4, n4), lambda c, b: (0, 0)),
            pl.BlockSpec((n4, 2 * n4), lambda c, b: (0, 0)),
            pl.BlockSpec((1, nq, 2 * n4, tc), lambda c, b: (order, 0, 0, c)),
        ],
        out_specs=blk,
        out_shape=jax.ShapeDtypeStruct(v.shape, BF16),
        compiler_params=_params("arbitrary", "arbitrary"),
        name="hy_conv",
    )(v, gate, fwd, inv, kspec)


def _gelu_tanh(x):
    return 0.5 * x * (1.0 + jnp.tanh(math.sqrt(2.0 / math.pi) * (x + 0.044715 * (x * x * x))))


def _rms(x, w):
    return x * lax.rsqrt(jnp.mean(x * x, axis=-1, keepdims=True) + EPS) * w


def _even_tail_kernel(x_ref, mod_ref, yf_ref, yb_ref, u_ref, ga_ref, hy_ref, unperm_ref, d_ref, gw_ref, gb_ref,
                      wo_ref, fw_ref, o_ref, *, final):
    y = yf_ref[0, 0].astype(F32) + yb_ref[0, 0].astype(F32) + u_ref[0] * d_ref[...]
    y = _gelu_tanh(y)
    y = y * jax.nn.sigmoid(_dot(y.astype(BF16), gw_ref[...]) + gb_ref[...])
    ya = (y * _silu(ga_ref[0].astype(F32))).astype(BF16)
    hy = _dot(unperm_ref[...], hy_ref[0].reshape(ya.shape[0], HY_WIDTH)).astype(BF16)
    out = _dot(ya, wo_ref[:S5_WIDTH, :]) + _dot(hy, wo_ref[S5_WIDTH:, :])
    xn = x_ref[0] + mod_ref[0][:, 2 * D_MODEL:] * out
    o_ref[0] = _rms(xn, fw_ref[...]) if final else xn


def _even_tail(x, mod, ys, u, g_a, hy, s5_d, glu_w, glu_b, w_out, final_w, final, tm=256):
    bsz, seq, _ = x.shape
    tok = lambda w: pl.BlockSpec((1, tm, w), lambda b, m: (b, m, 0))
    const = lambda r, c: pl.BlockSpec((r, c), lambda b, m: (0, 0))
    return pl.pallas_call(
        functools.partial(_even_tail_kernel, final=final),
        grid=(bsz, seq // tm),
        in_specs=[
            tok(D_MODEL), _mod_spec(mod),
            pl.BlockSpec((1, 1, tm, S5_WIDTH), lambda b, m: (0, b, m, 0)),
            pl.BlockSpec((1, 1, tm, S5_WIDTH), lambda b, m: (1, b, m, 0)),
            tok(S5_WIDTH), tok(S5_WIDTH), _phased_spec(tm, HY_WIDTH), const(tm, tm),
            const(1, S5_WIDTH), const(S5_WIDTH, S5_WIDTH), const(1, S5_WIDTH), const(D_INNER, D_MODEL),
            const(1, D_MODEL),
        ],
        out_specs=tok(D_MODEL),
        out_shape=jax.ShapeDtypeStruct((bsz, seq, D_MODEL), F32),
        compiler_params=_params("arbitrary", "arbitrary"),
        name="even_tail",
    )(x, mod, ys, ys, u, g_a, hy, _phase_perm(tm).T, s5_d.reshape(1, -1), glu_w, glu_b.reshape(1, -1), w_out,
      final_w.reshape(1, -1))


def _ssd_direction(xbc, dt, cum, st_ref, direction, d_row):
    q = SSD_CHUNK
    hp = LANES // SSD_HEADDIM
    ii = lax.broadcasted_iota(jnp.int32, (q, q), 0)
    jj = lax.broadcasted_iota(jnp.int32, (q, q), 1)
    if direction == 0:
        tri = (jj <= ii)
        last = q - 1
    else:
        tri = (jj >= ii)
        last = 0
    cum_t = cum.T
    src_t = cum_t - jnp.log(dt.T)
    w_t = jnp.exp(cum_t[:, last:last + 1] - src_t)
    last_row = cum[last:last + 1, :]
    first_half = lax.broadcasted_iota(jnp.int32, (q, LANES), 1) < SSD_HEADDIM
    first_half_row = lax.broadcasted_iota(jnp.int32, (1, LANES), 1) < SSD_HEADDIM
    x_all = xbc[:, :D_INNER]
    ys = []
    for g in range(SSD_GROUPS):
        b_g = xbc[:, D_INNER + g * SSD_STATE:D_INNER + (g + 1) * SSD_STATE]
        c_g = xbc[:, D_INNER + (SSD_GROUPS + g) * SSD_STATE:D_INNER + (SSD_GROUPS + g + 1) * SSD_STATE]
        b_bf, c_bf = b_g.astype(BF16), c_g.astype(BF16)
        cb = lax.dot_general(c_bf, b_bf, (((1,), (1,)), ((), ())), preferred_element_type=F32)
        b_t = b_g.T
        heads_per_group = SSD_HEADS // SSD_GROUPS
        for pair in range(heads_per_group // hp):
            h0 = g * heads_per_group + pair * hp
            lo = h0 * SSD_HEADDIM
            x_pair = x_all[:, lo:lo + LANES]
            x_blk = jnp.concatenate([jnp.where(first_half, x_pair, 0.0).astype(BF16),
                                     jnp.where(first_half, 0.0, x_pair).astype(BF16)], axis=0)
            scores, b_w, cols, tot = [], [], [], []
            for h in (h0, h0 + 1):
                hl = direction * SSD_HEADS + h
                col = jnp.broadcast_to(cum[:, hl:hl + 1], (q, q))
                arg = jnp.where(tri, col - src_t[hl:hl + 1, :], -jnp.inf)
                scores.append((cb * jnp.exp(arg)).astype(BF16))
                b_w.append((b_t * w_t[hl:hl + 1, :]).astype(BF16))
                cols.append(col)
                tot.append(jnp.broadcast_to(last_row[:, hl:hl + 1], (1, LANES)))
            s_pair = st_ref[:, lo:lo + LANES]
            y = _dot(jnp.concatenate(scores, axis=1), x_blk)
            y = y + _dot(c_bf, s_pair.astype(BF16)) * jnp.exp(jnp.where(first_half, cols[0], cols[1]))
            if d_row is not None:
                y = y + x_pair * d_row[:, lo:lo + LANES]
            ys.append(y)
            decay = jnp.exp(jnp.where(first_half_row, tot[0], tot[1]))
            st_ref[:, lo:lo + LANES] = s_pair * decay + _dot(jnp.concatenate(b_w, axis=1), x_blk)
    return jnp.concatenate(ys, axis=-1)


def _ssd_kernel(xf_ref, xb_ref, dtf_ref, dtb_ref, cumf_ref, cumb_ref, d_ref, h0_ref, yf_ref, yb_ref, hf_ref, st_ref):
    c = pl.program_id(1)
    nc = pl.num_programs(1)

    @pl.when(c == 0)
    def _():
        st_ref[...] = h0_ref[0]

    yf = _ssd_direction(xf_ref[0].astype(F32), dtf_ref[0], cumf_ref[0], st_ref.at[0], 0, d_ref[...])
    yf_ref[0] = yf.astype(yf_ref.dtype)
    yb = _ssd_direction(xb_ref[0].astype(F32), dtb_ref[0], cumb_ref[0], st_ref.at[1], 1, None)
    yb_ref[0] = yb.astype(yb_ref.dtype)

    @pl.when(c == nc - 1)
    def _():
        hf_ref[0] = st_ref[...]


def _ssd_scan(xbc, dt, cum, d_exp, h0):
    bsz, seq, width = xbc.shape
    q = SSD_CHUNK
    nc = seq // q
    fwd = lambda w: pl.BlockSpec((1, q, w), lambda b, c: (b, c, 0))
    bwd = lambda w: pl.BlockSpec((1, q, w), lambda b, c: (b, nc - 1 - c, 0))
    st_spec = pl.BlockSpec((1, 2, SSD_STATE, D_INNER), lambda b, c: (b, 0, 0, 0))
    return pl.pallas_call(
        _ssd_kernel,
        grid=(bsz, nc),
        in_specs=[fwd(width), bwd(width), fwd(LANES), bwd(LANES), fwd(LANES), bwd(LANES),
                  pl.BlockSpec((1, D_INNER), lambda b, c: (0, 0)), st_spec],
        out_specs=[fwd(D_INNER), bwd(D_INNER), st_spec],
        out_shape=[
            jax.ShapeDtypeStruct((bsz, seq, D_INNER), BF16),
            jax.ShapeDtypeStruct((bsz, seq, D_INNER), BF16),
            jax.ShapeDtypeStruct((bsz, 2, SSD_STATE, D_INNER), F32),
        ],
        scratch_shapes=[pltpu.VMEM((2, SSD_STATE, D_INNER), F32)],
        compiler_params=_params("arbitrary", "arbitrary"),
        name="ssd_scan",
    )(xbc, xbc, dt, dt, cum, cum, d_exp, h0)


def _odd_tail_kernel(x_ref, mod_ref, yf_ref, yb_ref, z_ref, nw_ref, wo_ref, fw_ref, o_ref, *, final):
    y = (yf_ref[0].astype(F32) + yb_ref[0].astype(F32)) * _silu(z_ref[0].astype(F32))
    y = _rms(y, nw_ref[...]).astype(BF16)
    xn = x_ref[0] + mod_ref[0][:, 2 * D_MODEL:] * _dot(y, wo_ref[...])
    o_ref[0] = _rms(xn, fw_ref[...]) if final else xn


def _odd_tail(x, mod, yf, yb, z, norm_w, w_out, final_w, final, tm=256):
    bsz, seq, _ = x.shape
    tok = lambda w: pl.BlockSpec((1, tm, w), lambda b, m: (b, m, 0))
    const = lambda r, c: pl.BlockSpec((r, c), lambda b, m: (0, 0))
    return pl.pallas_call(
        functools.partial(_odd_tail_kernel, final=final),
        grid=(bsz, seq // tm),
        in_specs=[tok(D_MODEL), _mod_spec(mod), tok(D_INNER), tok(D_INNER), tok(D_INNER),
                  const(1, D_INNER), const(D_INNER, D_MODEL), const(1, D_MODEL)],
        out_specs=tok(D_MODEL),
        out_shape=jax.ShapeDtypeStruct((bsz, seq, D_MODEL), F32),
        compiler_params=_params("arbitrary", "arbitrary"),
        name="odd_tail",
    )(x, mod, yf, yb, z, norm_w.reshape(1, -1), w_out, final_w.reshape(1, -1))


def _even_layer(x, mod, norm_w, w_in, w_out, s5, s5_d, glu_w, glu_b, short_w, short_b, dft, kspec, h0,
                final_w, final, s5_tq, conv_tc):
    tm = min(TOKEN_BLOCK, x.shape[1])
    outs = ((S5_WIDTH, F32, False), (S5_WIDTH, BF16, False),
            (HY_WIDTH, BF16, True), (HY_WIDTH, BF16, True), (HY_WIDTH, BF16, True))
    u, g_a, v, x1, x2g = _inproj(_inproj_even_kernel, x, mod, norm_w, w_in,
                                 (_phase_perm(tm), short_w, short_b.reshape(1, -1)), outs, tm=tm)
    ys, hf = _s5_scan(u, *s5, h0, s5_tq)
    fwd, inv = dft
    z1 = _hy_conv(v, x1, fwd, inv, kspec, 0, conv_tc)
    hy = _hy_conv(z1, x2g, fwd, inv, kspec, 1, conv_tc)
    xn = _even_tail(x, mod, ys, u, g_a, hy, s5_d, glu_w, glu_b, w_out, final_w, final, tm=tm)
    return xn, hf


def _odd_layer(x, mod, norm_w, w_in, conv_w, conv_b, a_log, dt_bias, d_exp, ssd_norm_w, w_out, h0, final_w, final):
    tm = min(TOKEN_BLOCK, x.shape[1])
    z, xbc, dt, cum = _inproj(functools.partial(_inproj_odd_kernel, tm=tm), x, mod, norm_w, w_in,
                              (conv_w, conv_b.reshape(1, -1), dt_bias, a_log),
                              ((D_INNER, BF16, False), (SSD_CONV_DIM, BF16, False), (LANES, F32, False),
                               (LANES, F32, False)), tm=tm)
    yf, yb, hf = _ssd_scan(xbc, dt, cum, d_exp, h0)
    xn = _odd_tail(x, mod, yf, yb, z, ssd_norm_w, w_out, final_w, final)
    return xn, hf


def _ssd_state_to_lanes(st):
    bsz = st.shape[0]
    return jnp.transpose(st.astype(F32), (0, 1, 4, 2, 3)).reshape(bsz, 2, SSD_STATE, D_INNER)


def _ssd_state_from_lanes(st):
    bsz = st.shape[0]
    return jnp.transpose(st.reshape(bsz, 2, SSD_STATE, SSD_HEADS, SSD_HEADDIM), (0, 1, 3, 4, 2))


def kernel(x_prompt, x_sample, state_s5_re, state_s5_im, state_ssd, c, c_ctx, mod_w, mod_b, norm_w, final_norm_w,
           ev_w_in, ev_w_out, s5_a_re, s5_a_im, s5_log_dt, s5_b_re, s5_b_im, s5_c_re, s5_c_im, s5_d, s5_glu_w,
           s5_glu_b, hy_short_w, hy_short_b, hy_f_w1, hy_f_b1, hy_f_freq1, hy_f_w2, hy_f_b2, hy_f_freq2, hy_f_w3,
           hy_bias, ssd_w_in, ssd_conv_w, ssd_conv_b, ssd_a_log, ssd_dt_bias, ssd_d, ssd_norm_w, ssd_w_out):
    bp, lp, _ = x_prompt.shape
    bs, ls, _ = x_sample.shape
    rows = 2 * SUBLANES
    cvec = jnp.zeros((rows, D_MODEL), F32).at[0].set(c_ctx.astype(F32)).at[1:1 + bs].set(c.astype(F32))
    mod = _modulation(cvec, mod_w, mod_b)
    dft_p, dft_s = _dft_matrices(lp // HY_PHASES), _dft_matrices(ls // HY_PHASES)
    xp, xs = x_prompt, x_sample
    new_re, new_im, new_ssd = [], [], []
    for i in range(DEPTH):
        j = i // 2
        final = i == DEPTH - 1
        mod_p = mod[i, 0:1].reshape(1, 1, 3 * D_MODEL)
        mod_s = mod[i, 1:1 + bs].reshape(bs, 1, 3 * D_MODEL)
        if i % 2 == 0:
            s5 = _s5_prepare(s5_a_re[j], s5_a_im[j], s5_log_dt[j], s5_b_re[j], s5_b_im[j], s5_c_re[j], s5_c_im[j])
            w_in = ev_w_in[j].astype(BF16)
            w_out = ev_w_out[j].astype(BF16)
            glu_w = s5_glu_w[j].astype(BF16)
            fargs = (hy_f_w1[j], hy_f_b1[j], hy_f_freq1[j], hy_f_w2[j], hy_f_b2[j], hy_f_freq2[j], hy_f_w3[j])
            ks_p = _hy_spectra(_polyphase_filters(_hy_filters(lp, *fargs)), dft_p[0], hy_bias[j])
            ks_s = _hy_spectra(_polyphase_filters(_hy_filters(ls, *fargs)), dft_s[0], hy_bias[j])
            h0_p = jnp.zeros((2, bp, S5_STATE_LANES), F32)
            h0_s = _s5_state_to_lanes(state_s5_re[:, j].astype(F32), state_s5_im[:, j].astype(F32))
            common = (s5_d[j], glu_w, s5_glu_b[j], hy_short_w[j], hy_short_b[j])
            xp, hf = _even_layer(xp, mod_p, norm_w[i], w_in, w_out, s5, *common, dft_p, ks_p, h0_p,
                                 final_norm_w, final, s5_tq=32, conv_tc=512)
            xs, _ = _even_layer(xs, mod_s, norm_w[i], w_in, w_out, s5, *common, dft_s, ks_s, h0_s,
                                final_norm_w, final, s5_tq=64, conv_tc=256)
            re, im = _s5_state_from_lanes(hf)
            new_re.append(re)
            new_im.append(im)
        else:
            pad = LANES - 2 * SSD_HEADS
            w_in = jnp.pad(ssd_w_in[j], ((0, 0), (0, pad))).astype(BF16)
            w_out = ssd_w_out[j].astype(BF16)
            dt_bias = jnp.pad(ssd_dt_bias[j].reshape(1, -1).astype(F32), ((0, 0), (0, pad)))
            a_log = jnp.pad(ssd_a_log[j].reshape(1, -1).astype(F32), ((0, 0), (0, pad)))
            d_exp = jnp.repeat(ssd_d[j].astype(F32), SSD_HEADDIM).reshape(1, D_INNER)
            common = (ssd_conv_w[j], ssd_conv_b[j], a_log, dt_bias, d_exp, ssd_norm_w[j], w_out)
            h0_p = jnp.zeros((bp, 2, SSD_STATE, D_INNER), F32)
            h0_s = _ssd_state_to_lanes(state_ssd[:, j])
            xp, hf = _odd_layer(xp, mod_p, norm_w[i], w_in, *common, h0_p, final_norm_w, final)
            xs, _ = _odd_layer(xs, mod_s, norm_w[i], w_in, *common, h0_s, final_norm_w, final)
            new_ssd.append(_ssd_state_from_lanes(hf))
    return (xp, xs, jnp.stack(new_re, axis=1), jnp.stack(new_im, axis=1), jnp.stack(new_ssd, axis=1))
```

```python
import functools
import math

import jax
import jax.numpy as jnp
from jax import lax
from jax.experimental import pallas as pl
from jax.experimental.pallas import tpu as pltpu

F32 = jnp.float32
BF16 = jnp.bfloat16
HIGHEST = lax.Precision.HIGHEST

D_MODEL = 1024
DEPTH = 4
D_INNER = 2048
EPS = 1e-6
S5_WIDTH = 1024
S5_GROUP = 16
S5_GROUPS = 64
S5_STATE = 64
HY_WIDTH = 1024
HY_BANDS = 16
HY_HID = 64
HY_DECAY_TARGET = 1e-2
HY_FAST = 0.3
HY_SLOW = 1.5
HY_PHASES = 4
SSD_HEADDIM = 64
SSD_HEADS = 32
SSD_STATE = 128
SSD_GROUPS = 4
SSD_CHUNK = 128
SSD_CONV_DIM = D_INNER + 2 * SSD_GROUPS * SSD_STATE

LANES = 128
SUBLANES = 8
VMEM_LIMIT = 56 * 1024 * 1024

S5_LANE_CHUNKS = S5_WIDTH // LANES
S5_CHUNK_STATE = (LANES // S5_GROUP) * S5_STATE
S5_STATE_LANES = S5_LANE_CHUNKS * 2 * S5_CHUNK_STATE
S5_PERM_ROWS = 256


def _params(*sem):
    return pltpu.CompilerParams(dimension_semantics=sem, vmem_limit_bytes=VMEM_LIMIT)


def _silu(x):
    return x * jax.nn.sigmoid(x)


def _dot(a, b):
    return jnp.dot(a, b, preferred_element_type=F32)


def _dot_hi(a, b):
    return jnp.dot(a, b, preferred_element_type=F32, precision=HIGHEST)


def _mod_kernel(c_ref, w_ref, b_ref, o_ref):
    s = _silu(c_ref[...])
    o_ref[0] = _dot_hi(s, w_ref[0]) + b_ref[0]


def _modulation(cvec, mod_w, mod_b):
    rows = cvec.shape[0]
    tn = D_MODEL
    return pl.pallas_call(
        _mod_kernel,
        grid=(DEPTH, 3 * D_MODEL // tn),
        in_specs=[
            pl.BlockSpec((rows, D_MODEL), lambda i, n: (0, 0)),
            pl.BlockSpec((1, D_MODEL, tn), lambda i, n: (i, 0, n)),
            pl.BlockSpec((1, 1, tn), lambda i, n: (i, 0, n)),
        ],
        out_specs=pl.BlockSpec((1, rows, tn), lambda i, n: (i, 0, n)),
        out_shape=jax.ShapeDtypeStruct((DEPTH, rows, 3 * D_MODEL), F32),
        compiler_params=_params("arbitrary", "arbitrary"),
        name="modulation",
    )(cvec, mod_w, mod_b.reshape(DEPTH, 1, 3 * D_MODEL))


def _norm_mod(x, mod, nw):
    shift = mod[:, :D_MODEL]
    scale = mod[:, D_MODEL:2 * D_MODEL]
    ms = jnp.mean(x * x, axis=-1, keepdims=True)
    return (x * lax.rsqrt(ms + EPS) * nw * (1.0 + scale) + shift).astype(BF16)


def _conv3(x, prev_row, next_row, w, b):
    rows = x.shape[0]
    ridx = lax.broadcasted_iota(jnp.int32, x.shape, 0)
    xp = jnp.where(ridx == 0, prev_row, pltpu.roll(x, 1, 0))
    xn = jnp.where(ridx == rows - 1, next_row, pltpu.roll(x, rows - 1, 0))
    return w[0:1] * xp + w[1:2] * x + w[2:3] * xn + b


def _conv3_phased(x, prev_row, next_row, w, b):
    ph = x.shape[0] // HY_PHASES
    blk = [x[r * ph:(r + 1) * ph] for r in range(HY_PHASES)]
    ridx = lax.broadcasted_iota(jnp.int32, blk[0].shape, 0)
    before = [jnp.where(ridx == 0, prev_row, pltpu.roll(blk[-1], 1, 0))] + blk[:-1]
    after = blk[1:] + [jnp.where(ridx == ph - 1, next_row, pltpu.roll(blk[0], ph - 1, 0))]
    return jnp.concatenate([w[0:1] * before[r] + w[1:2] * blk[r] + w[2:3] * after[r] + b
                            for r in range(HY_PHASES)], axis=0)


CONV_COLS = 512
TOKEN_BLOCK = 512


def _proj_conv3(conv, hb, pv_ref, nx_ref, mod, nw, w_ref, lo, width, cw_ref, cb_ref):
    m = pl.program_id(1)
    nm = pl.num_programs(1)
    h_prev = _norm_mod(pv_ref[0], mod, nw)
    h_next = _norm_mod(nx_ref[0], mod, nw)
    has_prev, has_next = (m > 0).astype(F32), (m < nm - 1).astype(F32)
    for off in range(0, width, CONV_COLS):
        w = w_ref[:, lo + off:lo + off + CONV_COLS]
        prev = _dot(h_prev, w)[SUBLANES - 1:SUBLANES] * has_prev
        nxt = _dot(h_next, w)[0:1] * has_next
        yield off, conv(_dot(hb, w), prev, nxt, cw_ref[:, off:off + CONV_COLS], cb_ref[:, off:off + CONV_COLS])


def _inproj_even_kernel(x_ref, pv_ref, nx_ref, mod_ref, nw_ref, w_ref, perm_ref, cw_ref, cb_ref,
                        u_ref, ga_ref, v_ref, x1_ref, x2_ref):
    mod, nw = mod_ref[0], nw_ref[...]
    hb = _norm_mod(x_ref[0], mod, nw)
    lo_vx, hi_vx = 2 * S5_WIDTH, 2 * S5_WIDTH + 3 * HY_WIDTH
    u_ref[0] = _dot(hb, w_ref[:, :S5_WIDTH])
    ga_ref[0] = _dot(hb, w_ref[:, S5_WIDTH:lo_vx]).astype(ga_ref.dtype)
    hbp = _dot(perm_ref[...], hb).astype(BF16)
    ph = hb.shape[0] // HY_PHASES
    outs = (v_ref, x1_ref, x2_ref)
    for off, y in _proj_conv3(_conv3_phased, hbp, pv_ref, nx_ref, mod, nw, w_ref, lo_vx, 3 * HY_WIDTH,
                              cw_ref, cb_ref):
        which, col = off // HY_WIDTH, off % HY_WIDTH
        if which == 2:
            y = y * _silu(_dot(hbp, w_ref[:, hi_vx + col:hi_vx + col + CONV_COLS]))
        o_ref = outs[which]
        o_ref[0, :, :, col:col + CONV_COLS] = y.reshape(HY_PHASES, ph, CONV_COLS).astype(o_ref.dtype)


def _softplus(x):
    return jnp.maximum(x, 0.0) + jnp.log(1.0 + jnp.exp(-jnp.abs(x)))


def _inproj_odd_kernel(x_ref, pv_ref, nx_ref, mod_ref, nw_ref, w_ref, cw_ref, cb_ref, bias_ref, alog_ref,
                       z_ref, xbc_ref, dt_ref, cum_ref, *, tm):
    mod, nw = mod_ref[0], nw_ref[...]
    hb = _norm_mod(x_ref[0], mod, nw)
    lo_dt = D_INNER + SSD_CONV_DIM
    z_ref[0] = _dot(hb, w_ref[:, :D_INNER]).astype(z_ref.dtype)
    for off, y in _proj_conv3(_conv3, hb, pv_ref, nx_ref, mod, nw, w_ref, D_INNER, SSD_CONV_DIM, cw_ref, cb_ref):
        xbc_ref[0, :, off:off + CONV_COLS] = _silu(y).astype(xbc_ref.dtype)
    dt = _softplus(_dot(hb, w_ref[:, lo_dt:]) + bias_ref[...])
    dt_ref[0] = dt
    da = dt * (-jnp.exp(alog_ref[...]))
    q = SSD_CHUNK
    ii = lax.broadcasted_iota(jnp.int32, (q, q), 0)
    jj = lax.broadcasted_iota(jnp.int32, (q, q), 1)
    low, up = (jj <= ii).astype(F32), (jj >= ii).astype(F32)
    fwd_lane = lax.broadcasted_iota(jnp.int32, (q, LANES), 1) < SSD_HEADS
    for c in range(tm // q):
        dac = da[c * q:(c + 1) * q]
        cum_ref[0, c * q:(c + 1) * q, :] = jnp.where(fwd_lane, _dot_hi(low, dac), _dot_hi(up, dac))


def _mod_spec(mod):
    per_batch = mod.shape[0] > 1
    return pl.BlockSpec((1, 1, mod.shape[-1]), (lambda b, m: (b, 0, 0)) if per_batch else (lambda b, m: (0, 0, 0)))


def _halo_specs(width, tl, seq):
    r = tl // SUBLANES
    last = seq // SUBLANES - 1
    return (pl.BlockSpec((1, SUBLANES, width), lambda b, m: (b, jnp.maximum(m * r - 1, 0), 0)),
            pl.BlockSpec((1, SUBLANES, width), lambda b, m: (b, jnp.minimum((m + 1) * r, last), 0)))


def _phase_perm(tm):
    ph = tm // HY_PHASES
    i = jnp.arange(tm, dtype=jnp.int32)
    src = HY_PHASES * (i % ph) + i // ph
    return (src[:, None] == i[None, :]).astype(BF16)


def _phased_spec(tm, width):
    return pl.BlockSpec((1, HY_PHASES, tm // HY_PHASES, width), lambda b, m: (b, 0, m, 0))


def _inproj(kernel_fn, x, mod, norm_w, w_bf16, consts, outs, tm=256):
    bsz, seq, _ = x.shape
    pv, nx = _halo_specs(D_MODEL, tm, seq)
    const = lambda a: pl.BlockSpec(a.shape, lambda b, m: (0, 0))
    out_specs, out_shape = [], []
    for width, dtype, phased in outs:
        if phased:
            out_specs.append(_phased_spec(tm, width))
            out_shape.append(jax.ShapeDtypeStruct((bsz, HY_PHASES, seq // HY_PHASES, width), dtype))
        else:
            out_specs.append(pl.BlockSpec((1, tm, width), lambda b, m: (b, m, 0)))
            out_shape.append(jax.ShapeDtypeStruct((bsz, seq, width), dtype))
    return pl.pallas_call(
        kernel_fn,
        grid=(bsz, seq // tm),
        in_specs=[pl.BlockSpec((1, tm, D_MODEL), lambda b, m: (b, m, 0)), pv, nx, _mod_spec(mod),
                  pl.BlockSpec((1, D_MODEL), lambda b, m: (0, 0)),
                  pl.BlockSpec(w_bf16.shape, lambda b, m: (0, 0), pipeline_mode=pl.Buffered(1))]
                 + [const(a) for a in consts],
        out_specs=out_specs,
        out_shape=out_shape,
        compiler_params=_params("arbitrary", "arbitrary"),
        name="inproj",
    )(x, x, x, mod, norm_w.reshape(1, D_MODEL), w_bf16, *consts)


def _s5_kernel(u_ref, pin_ref, pout_ref, bw_ref, cw_ref, lr_ref, li_ref, h0_ref, ys_ref, hf_ref, bu_ref, hst_ref,
               *, bsz, tq, jgroup):
    d = pl.program_id(0)
    blk = pl.program_id(1)
    nblk = pl.num_programs(1)
    cs = S5_CHUNK_STATE
    nsub = cs // LANES
    prow = pin_ref.shape[1]
    tsub = prow // bsz
    ns = tq // tsub

    @pl.when(blk == 0)
    def _():
        hst_ref[...] = h0_ref[0]

    def pick(blocks, s):
        return blocks[s] if ns == 1 else jnp.where(d == 0, blocks[s], blocks[ns - 1 - s])

    u_nat = [u_ref[:, s * tsub:(s + 1) * tsub, :].reshape(prow, S5_WIDTH) for s in range(ns)]
    u_tm = jnp.concatenate([_dot(pin_ref[0], pick(u_nat, s).astype(BF16)).astype(BF16) for s in range(ns)], axis=0)
    for j in range(S5_LANE_CHUNKS):
        bu_ref[:, 2 * cs * j:2 * cs * (j + 1)] = _dot(u_tm[:, LANES * j:LANES * (j + 1)], bw_ref[0, j])

    for j0 in range(0, S5_LANE_CHUNKS, jgroup):
        cols = [(2 * cs * j + LANES * c, 2 * cs * j + cs + LANES * c, j, c)
                for j in range(j0, j0 + jgroup) for c in range(nsub)]
        lam = [(jnp.broadcast_to(lr_ref[0, j, :, LANES * c:LANES * (c + 1)], (bsz, LANES)),
                jnp.broadcast_to(li_ref[0, j, :, LANES * c:LANES * (c + 1)], (bsz, LANES))) for _, _, j, c in cols]
        state = [(hst_ref[:, re_lo:re_lo + LANES], hst_ref[:, im_lo:im_lo + LANES]) for re_lo, im_lo, _, _ in cols]
        for step in range(tq):
            r0 = step * bsz
            for i, (re_lo, im_lo, _, _) in enumerate(cols):
                hr, hi = state[i]
                lam_re, lam_im = lam[i]
                nr = lam_re * hr - lam_im * hi + bu_ref[r0:r0 + bsz, re_lo:re_lo + LANES]
                ni = lam_re * hi + lam_im * hr + bu_ref[r0:r0 + bsz, im_lo:im_lo + LANES]
                bu_ref[r0:r0 + bsz, re_lo:re_lo + LANES] = nr
                bu_ref[r0:r0 + bsz, im_lo:im_lo + LANES] = ni
                state[i] = (nr, ni)
        for i, (re_lo, im_lo, _, _) in enumerate(cols):
            hst_ref[:, re_lo:re_lo + LANES] = state[i][0]
            hst_ref[:, im_lo:im_lo + LANES] = state[i][1]

    y_tm = jnp.concatenate([_dot(bu_ref[:, 2 * cs * j:2 * cs * (j + 1)].astype(BF16), cw_ref[0, j])
                            for j in range(S5_LANE_CHUNKS)], axis=-1)
    y_grp = [y_tm[s * prow:(s + 1) * prow] for s in range(ns)]
    for s in range(ns):
        y = pick(y_grp, s)
        y_hi = y.astype(BF16)
        y_lo = (y - y_hi.astype(F32)).astype(BF16)
        y_bm = _dot(pout_ref[0], y_hi) + _dot(pout_ref[0], y_lo)
        ys_ref[0, :, s * tsub:(s + 1) * tsub, :] = y_bm.reshape(bsz, tsub, S5_WIDTH).astype(ys_ref.dtype)

    @pl.when(blk == nblk - 1)
    def _():
        hf_ref[0] = hst_ref[...]


def _s5_scan(u, bw, cw, lr, li, h0, tq):
    bsz, seq, _ = u.shape
    nblk = seq // tq
    rows = bsz * tq
    prow = min(rows, S5_PERM_ROWS)
    tsub = prow // bsz
    r = jnp.arange(prow, dtype=jnp.int32)
    step, b = r // bsz, r % bsz
    src = jnp.stack([b * tsub + step, b * tsub + (tsub - 1 - step)])
    p_in = (src[:, :, None] == r[None, None, :]).astype(BF16)
    p_out = jnp.transpose(p_in, (0, 2, 1))

    def tblk(d, i):
        return i + d * (nblk - 1 - 2 * i)

    nj, cs = S5_LANE_CHUNKS, S5_CHUNK_STATE
    jgroup = 2 if bsz <= SUBLANES else 1
    return pl.pallas_call(
        functools.partial(_s5_kernel, bsz=bsz, tq=tq, jgroup=jgroup),
        grid=(2, nblk),
        in_specs=[
            pl.BlockSpec((bsz, tq, S5_WIDTH), lambda d, i: (0, tblk(d, i), 0)),
            pl.BlockSpec((1, prow, prow), lambda d, i: (d, 0, 0)),
            pl.BlockSpec((1, prow, prow), lambda d, i: (d, 0, 0)),
            pl.BlockSpec((1, nj, LANES, 2 * cs), lambda d, i: (d, 0, 0, 0)),
            pl.BlockSpec((1, nj, 2 * cs, LANES), lambda d, i: (d, 0, 0, 0)),
            pl.BlockSpec((1, nj, 1, cs), lambda d, i: (d, 0, 0, 0)),
            pl.BlockSpec((1, nj, 1, cs), lambda d, i: (d, 0, 0, 0)),
            pl.BlockSpec((1, bsz, S5_STATE_LANES), lambda d, i: (d, 0, 0)),
        ],
        out_specs=[
            pl.BlockSpec((1, bsz, tq, S5_WIDTH), lambda d, i: (d, 0, tblk(d, i), 0)),
            pl.BlockSpec((1, bsz, S5_STATE_LANES), lambda d, i: (d, 0, 0)),
        ],
        out_shape=[
            jax.ShapeDtypeStruct((2, bsz, seq, S5_WIDTH), BF16),
            jax.ShapeDtypeStruct((2, bsz, S5_STATE_LANES), F32),
        ],
        scratch_shapes=[
            pltpu.VMEM((rows, S5_STATE_LANES), F32),
            pltpu.VMEM((bsz, S5_STATE_LANES), F32),
        ],
        compiler_params=_params("arbitrary", "arbitrary"),
        name="s5_scan",
    )(u, p_in, p_out, bw, cw, lr, li, h0)


def _s5_prepare(a_re, a_im, log_dt, b_re, b_im, c_re, c_im):
    dt = jnp.exp(log_dt.astype(F32))[..., None]
    a_re, a_im = a_re.astype(F32), a_im.astype(F32)
    mag = jnp.exp(a_re * dt)
    lbr, lbi = mag * jnp.cos(a_im * dt), mag * jnp.sin(a_im * dt)
    den = a_re * a_re + a_im * a_im
    cr = ((lbr - 1.0) * a_re + lbi * a_im) / den
    ci = (lbi * a_re - (lbr - 1.0) * a_im) / den
    bbr = cr[..., None] * b_re - ci[..., None] * b_im
    bbi = cr[..., None] * b_im + ci[..., None] * b_re
    nj, gpc = S5_LANE_CHUNKS, LANES // S5_GROUP
    eye = jnp.eye(gpc, dtype=F32)

    def pack_b(m):
        m = m.reshape(2, nj, gpc, S5_STATE, S5_GROUP)
        return jnp.einsum('djgph,gk->djghkp', m, eye).reshape(2, nj, LANES, S5_CHUNK_STATE)

    def pack_c(m):
        m = m.reshape(2, nj, gpc, S5_GROUP, S5_STATE)
        return jnp.einsum('djghp,gk->djgpkh', m, eye).reshape(2, nj, S5_CHUNK_STATE, LANES)

    bw = jnp.concatenate([pack_b(bbr), pack_b(bbi)], axis=-1).astype(BF16)
    cw = jnp.concatenate([pack_c(c_re.astype(F32)), pack_c(-c_im.astype(F32))], axis=-2).astype(BF16)
    lr = lbr.reshape(2, nj, 1, S5_CHUNK_STATE)
    li = lbi.reshape(2, nj, 1, S5_CHUNK_STATE)
    return bw, cw, lr, li


def _s5_state_to_lanes(re, im):
    bsz = re.shape[0]
    st = jnp.stack([re.reshape(bsz, 2, S5_LANE_CHUNKS, S5_CHUNK_STATE),
                    im.reshape(bsz, 2, S5_LANE_CHUNKS, S5_CHUNK_STATE)], axis=3)
    return jnp.transpose(st.reshape(bsz, 2, S5_STATE_LANES), (1, 0, 2))


def _s5_state_from_lanes(hf):
    bsz = hf.shape[1]
    st = jnp.transpose(hf, (1, 0, 2)).reshape(bsz, 2, S5_LANE_CHUNKS, 2, S5_CHUNK_STATE)
    re = st[:, :, :, 0].reshape(bsz, 2, S5_GROUPS, S5_STATE)
    im = st[:, :, :, 1].reshape(bsz, 2, S5_GROUPS, S5_STATE)
    return re, im


def _hy_filter_kernel(bands_ref, deltas_ref, w1t_ref, w1c_ref, w1s_ref, b1_ref, f1_ref, w2_ref, b2_ref, f2_ref,
                      w3_ref, o_ref, *, seq, tl):
    r = pl.program_id(0)
    m = pl.program_id(1)
    t = (HY_PHASES * (lax.broadcasted_iota(jnp.int32, (tl, 1), 0) + m * tl) + r).astype(F32)
    t_norm = t / (seq - 1)
    ang = (2.0 * math.pi / seq) * t * bands_ref[...]
    pre = t_norm * w1t_ref[...] + _dot_hi(jnp.cos(ang), w1c_ref[...]) + _dot_hi(jnp.sin(ang), w1s_ref[...])
    f = jnp.sin(f1_ref[...] * (pre + b1_ref[...]))
    f = jnp.sin(f2_ref[...] * (_dot_hi(f, w2_ref[...]) + b2_ref[...]))
    f = _dot_hi(f, w3_ref[...])
    window = jnp.exp(-t_norm * deltas_ref[...])
    for o in range(2):
        for direction in range(2):
            lo = (2 * o + direction) * HY_WIDTH
            o_ref[o, direction, 0] = f[:, lo:lo + HY_WIDTH] * window


def _hy_filters(seq, w1, b1, freq1, w2, b2, freq2, w3):
    n4 = seq // HY_PHASES
    tl = min(256, n4)
    bands = jnp.linspace(1e-4, HY_BANDS - 1, HY_BANDS, dtype=F32).reshape(1, HY_BANDS)
    deltas = jnp.abs(jnp.linspace(math.log(HY_DECAY_TARGET) / HY_SLOW, math.log(HY_DECAY_TARGET) / HY_FAST,
                                  HY_WIDTH, dtype=F32)).reshape(1, HY_WIDTH)
    small = [bands, deltas, w1[0:1], w1[1:1 + HY_BANDS], w1[1 + HY_BANDS:], b1.reshape(1, -1), freq1.reshape(1, -1),
             w2, b2.reshape(1, -1), freq2.reshape(1, -1), w3]
    return pl.pallas_call(
        functools.partial(_hy_filter_kernel, seq=seq, tl=tl),
        grid=(HY_PHASES, n4 // tl),
        in_specs=[pl.BlockSpec(a.shape, lambda r, m: (0, 0)) for a in small],
        out_specs=pl.BlockSpec((2, 2, 1, tl, HY_WIDTH), lambda r, m: (0, 0, r, m, 0)),
        out_shape=jax.ShapeDtypeStruct((2, 2, HY_PHASES, n4, HY_WIDTH), F32),
        compiler_params=_params("arbitrary", "arbitrary"),
        name="hy_filters",
    )(*small)


def _polyphase_filters(hf):
    fwd, bwd = hf[:, 0], hf[:, 1]
    zero = jnp.zeros_like(fwd[:, 0, :1])

    def lags(h, first, start):
        r, j0 = start % HY_PHASES, start // HY_PHASES
        tail = h[:, r, j0:] if j0 else h[:, r, :-1]
        return jnp.concatenate([first, tail], axis=1)

    out = []
    for q in range(1 - HY_PHASES, HY_PHASES):
        pos = fwd[:, q] if q >= 0 else lags(fwd, bwd[:, -q, :1], HY_PHASES + q)
        neg = lags(bwd, zero, HY_PHASES - q)
        out.append(jnp.stack([pos + neg, pos - neg], axis=1))
    return jnp.stack(out, axis=1).astype(BF16)


def _dft_matrices(seq):
    n = 2 * seq
    k = jnp.arange(seq, dtype=jnp.int32)
    ang = ((k[:, None] * k[None, :]) % n).astype(F32) * (2.0 * math.pi / n)
    cos, sin = jnp.cos(ang), jnp.sin(ang)
    alt = jnp.where(k % 2 == 0, 1.0, -1.0).astype(F32)
    fwd = jnp.concatenate([cos, jnp.where(k[:, None] == 0, alt[None, :], -sin)], axis=0)
    scale = jnp.where(k == 0, 1.0 / n, 2.0 / n).astype(F32)
    inv = jnp.concatenate([cos * scale[None, :], jnp.where(k[None, :] == 0, alt[:, None], -sin) * scale[None, :]],
                          axis=1)
    return fwd.astype(BF16), inv.astype(BF16)


def _hy_spec_kernel(fwd_ref, h_ref, bias_ref, o_ref, *, half):
    q = pl.program_id(1)
    f = fwd_ref[...]
    s_sum = _dot(f, h_ref[0, 0, 0])
    s_dif = _dot(f, h_ref[0, 0, 1])
    ridx = lax.broadcasted_iota(jnp.int32, s_sum.shape, 0)
    real_row = ridx <= half
    lag0 = jnp.where(real_row & (q == HY_PHASES - 1), bias_ref[0], 0.0)
    o_ref[0, 0] = jnp.where(real_row, s_sum, s_dif) + lag0


def _hy_spectra(hs, fwd, bias, tc=512):
    _, nq, _, n4, width = hs.shape
    return pl.pallas_call(
        functools.partial(_hy_spec_kernel, half=n4),
        grid=(2, nq, width // tc),
        in_specs=[
            pl.BlockSpec((2 * n4, n4), lambda o, q, c: (0, 0)),
            pl.BlockSpec((1, 1, 2, n4, tc), lambda o, q, c: (o, q, 0, 0, c)),
            pl.BlockSpec((1, 1, tc), lambda o, q, c: (o, 0, c)),
        ],
        out_specs=pl.BlockSpec((1, 1, 2 * n4, tc), lambda o, q, c: (o, q, 0, c)),
        out_shape=jax.ShapeDtypeStruct((2, nq, 2 * n4, width), F32),
        compiler_params=_params("arbitrary", "arbitrary", "arbitrary"),
        name="hy_spectra",
    )(fwd, hs, bias.reshape(2, 1, width))


CONV_LANES = 256


def _hy_conv_kernel(v_ref, g_ref, fwd_ref, inv_ref, ks_ref, o_ref, s_ref, p_ref, *, half):
    f = fwd_ref[...]
    for r in range(HY_PHASES):
        s_ref[r] = _dot(f, v_ref[0, r])
    nq = 2 * HY_PHASES - 1
    row0 = lax.broadcasted_iota(jnp.int32, (SUBLANES, CONV_LANES), 0) == 0
    for lo in range(0, half, SUBLANES):
        for c0 in range(0, v_ref.shape[-1], CONV_LANES):
            rows_re, rows_im, cols = slice(lo, lo + SUBLANES), slice(half + lo, half + lo + SUBLANES), \
                slice(c0, c0 + CONV_LANES)
            s_re = [s_ref[r, rows_re, cols] for r in range(HY_PHASES)]
            s_im = [s_ref[r, rows_im, cols] for r in range(HY_PHASES)]
            k_re = [ks_ref[0, q, rows_re, cols] for q in range(nq)]
            k_im = [ks_ref[0, q, rows_im, cols] for q in range(nq)]
            if lo == 0:
                s_ny, k_ny = [s[0:1] for s in s_im], [k[0:1] for k in k_im]
                s_im = [jnp.where(row0, 0.0, s) for s in s_im]
                k_im = [jnp.where(row0, 0.0, k) for k in k_im]
            for r in range(HY_PHASES):
                qs = [r - rp + HY_PHASES - 1 for rp in range(HY_PHASES)]
                p_re = functools.reduce(jnp.add, [s_re[rp] * k_re[q] - s_im[rp] * k_im[q] for rp, q in enumerate(qs)])
                p_im = functools.reduce(jnp.add, [s_re[rp] * k_im[q] + s_im[rp] * k_re[q] for rp, q in enumerate(qs)])
                if lo == 0:
                    p_ny = functools.reduce(jnp.add, [s_ny[rp] * k_ny[q] for rp, q in enumerate(qs)])
                    p_im = jnp.where(row0, p_ny, p_im)
                p_ref[r, rows_re, cols] = p_re
                p_ref[r, rows_im, cols] = p_im
    for r in range(HY_PHASES):
        o_ref[0, r] = (g_ref[0, r] * _dot(inv_ref[...], p_ref[r].astype(BF16))).astype(o_ref.dtype)


def _hy_conv(v, gate, fwd, inv, kspec, order, tc):
    bsz, _, n4, width = v.shape
    nq = 2 * HY_PHASES - 1
    blk = pl.BlockSpec((1, HY_PHASES, n4, tc), lambda c, b: (b, 0, 0, c))
    return pl.pallas_call(
        functools.partial(_hy_conv_kernel, half=n4),
        grid=(width // tc, bsz),
        in_specs=[
            blk, blk,
            pl.BlockSpec((2 * n4, n4), lambda c, b: (0, 0)),
            pl.BlockSpec((n4, 2 * n4), lambda c, b: (0, 0)),
            pl.BlockSpec((1, nq, 2 * n4, tc), lambda c, b: (order, 0, 0, c)),
        ],
        out_specs=blk,
        out_shape=jax.ShapeDtypeStruct(v.shape, BF16),
        scratch_shapes=[pltpu.VMEM((HY_PHASES, 2 * n4, tc), F32), pltpu.VMEM((HY_PHASES, 2 * n4, tc), F32)],
        compiler_params=_params("arbitrary", "arbitrary"),
        name="hy_conv",
    )(v, gate, fwd, inv, kspec)


def _gelu_tanh(x):
    return 0.5 * x * (1.0 + jnp.tanh(math.sqrt(2.0 / math.pi) * (x + 0.044715 * (x * x * x))))


def _rms(x, w):
    return x * lax.rsqrt(jnp.mean(x * x, axis=-1, keepdims=True) + EPS) * w


def _even_tail_kernel(x_ref, mod_ref, yf_ref, yb_ref, u_ref, ga_ref, hy_ref, unperm_ref, d_ref, gw_ref, gb_ref,
                      wo_ref, fw_ref, o_ref, *, final):
    y = yf_ref[0, 0].astype(F32) + yb_ref[0, 0].astype(F32) + u_ref[0] * d_ref[...]
    y = _gelu_tanh(y)
    y = y * jax.nn.sigmoid(_dot(y.astype(BF16), gw_ref[...]) + gb_ref[...])
    ya = (y * _silu(ga_ref[0].astype(F32))).astype(BF16)
    hy = _dot(unperm_ref[...], hy_ref[0].reshape(ya.shape[0], HY_WIDTH)).astype(BF16)
    out = _dot(ya, wo_ref[:S5_WIDTH, :]) + _dot(hy, wo_ref[S5_WIDTH:, :])
    xn = x_ref[0] + mod_ref[0][:, 2 * D_MODEL:] * out
    o_ref[0] = _rms(xn, fw_ref[...]) if final else xn


def _even_tail(x, mod, ys, u, g_a, hy, s5_d, glu_w, glu_b, w_out, final_w, final, tm=256):
    bsz, seq, _ = x.shape
    tok = lambda w: pl.BlockSpec((1, tm, w), lambda b, m: (b, m, 0))
    const = lambda r, c: pl.BlockSpec((r, c), lambda b, m: (0, 0))
    return pl.pallas_call(
        functools.partial(_even_tail_kernel, final=final),
        grid=(bsz, seq // tm),
        in_specs=[
            tok(D_MODEL), _mod_spec(mod),
            pl.BlockSpec((1, 1, tm, S5_WIDTH), lambda b, m: (0, b, m, 0)),
            pl.BlockSpec((1, 1, tm, S5_WIDTH), lambda b, m: (1, b, m, 0)),
            tok(S5_WIDTH), tok(S5_WIDTH), _phased_spec(tm, HY_WIDTH), const(tm, tm),
            const(1, S5_WIDTH), const(S5_WIDTH, S5_WIDTH), const(1, S5_WIDTH), const(D_INNER, D_MODEL),
            const(1, D_MODEL),
        ],
        out_specs=tok(D_MODEL),
        out_shape=jax.ShapeDtypeStruct((bsz, seq, D_MODEL), F32),
        compiler_params=_params("arbitrary", "arbitrary"),
        name="even_tail",
    )(x, mod, ys, ys, u, g_a, hy, _phase_perm(tm).T, s5_d.reshape(1, -1), glu_w, glu_b.reshape(1, -1), w_out,
      final_w.reshape(1, -1))


def _ssd_direction(xbc, dt, cum, st_ref, direction, d_row):
    q = SSD_CHUNK
    hp = LANES // SSD_HEADDIM
    ii = lax.broadcasted_iota(jnp.int32, (q, q), 0)
    jj = lax.broadcasted_iota(jnp.int32, (q, q), 1)
    if direction == 0:
        tri = (jj <= ii)
        last = q - 1
    else:
        tri = (jj >= ii)
        last = 0
    cum_t = cum.T
    src_t = cum_t - jnp.log(dt.T)
    w_t = jnp.exp(cum_t[:, last:last + 1] - src_t)
    last_row = cum[last:last + 1, :]
    first_half = lax.broadcasted_iota(jnp.int32, (q, LANES), 1) < SSD_HEADDIM
    first_half_row = lax.broadcasted_iota(jnp.int32, (1, LANES), 1) < SSD_HEADDIM
    x_all = xbc[:, :D_INNER]
    ys = []
    for g in range(SSD_GROUPS):
        b_g = xbc[:, D_INNER + g * SSD_STATE:D_INNER + (g + 1) * SSD_STATE]
        c_g = xbc[:, D_INNER + (SSD_GROUPS + g) * SSD_STATE:D_INNER + (SSD_GROUPS + g + 1) * SSD_STATE]
        b_bf, c_bf = b_g.astype(BF16), c_g.astype(BF16)
        cb = lax.dot_general(c_bf, b_bf, (((1,), (1,)), ((), ())), preferred_element_type=F32)
        b_t = b_g.T
        heads_per_group = SSD_HEADS // SSD_GROUPS
        for pair in range(heads_per_group // hp):
            h0 = g * heads_per_group + pair * hp
            lo = h0 * SSD_HEADDIM
            x_pair = x_all[:, lo:lo + LANES]
            x_blk = jnp.concatenate([jnp.where(first_half, x_pair, 0.0).astype(BF16),
                                     jnp.where(first_half, 0.0, x_pair).astype(BF16)], axis=0)
            scores, b_w, cols, tot = [], [], [], []
            for h in (h0, h0 + 1):
                hl = direction * SSD_HEADS + h
                col = jnp.broadcast_to(cum[:, hl:hl + 1], (q, q))
                arg = jnp.where(tri, col - src_t[hl:hl + 1, :], -jnp.inf)
                scores.append((cb * jnp.exp(arg)).astype(BF16))
                b_w.append((b_t * w_t[hl:hl + 1, :]).astype(BF16))
                cols.append(col)
                tot.append(jnp.broadcast_to(last_row[:, hl:hl + 1], (1, LANES)))
            s_pair = st_ref[:, lo:lo + LANES]
            y = _dot(jnp.concatenate(scores, axis=1), x_blk)
            y = y + _dot(c_bf, s_pair.astype(BF16)) * jnp.exp(jnp.where(first_half, cols[0], cols[1]))
            if d_row is not None:
                y = y + x_pair * d_row[:, lo:lo + LANES]
            ys.append(y)
            decay = jnp.exp(jnp.where(first_half_row, tot[0], tot[1]))
            st_ref[:, lo:lo + LANES] = s_pair * decay + _dot(jnp.concatenate(b_w, axis=1), x_blk)
    return jnp.concatenate(ys, axis=-1)


def _ssd_kernel(*refs, has_h0, want_final):
    xf_ref, xb_ref, dtf_ref, dtb_ref, cumf_ref, cumb_ref, d_ref = refs[:7]
    h0_ref = refs[7] if has_h0 else None
    yf_ref, yb_ref = refs[7 + has_h0:9 + has_h0]
    hf_ref = refs[9 + has_h0] if want_final else None
    st_ref = refs[-1]
    c = pl.program_id(1)
    nc = pl.num_programs(1)

    hp = LANES // SSD_HEADDIM
    pairs = [(dr, p) for dr in range(2) for p in range(SSD_HEADS // hp)]

    @pl.when(c == 0)
    def _():
        if not has_h0:
            st_ref[...] = jnp.zeros_like(st_ref)
        for dr, p in pairs if has_h0 else ():
            st_ref[dr, :, LANES * p:LANES * (p + 1)] = h0_ref[0, dr, hp * p:hp * (p + 1)].reshape(LANES, SSD_STATE).T

    yf = _ssd_direction(xf_ref[0].astype(F32), dtf_ref[0], cumf_ref[0], st_ref.at[0], 0, d_ref[...])
    yf_ref[0] = yf.astype(yf_ref.dtype)
    yb = _ssd_direction(xb_ref[0].astype(F32), dtb_ref[0], cumb_ref[0], st_ref.at[1], 1, None)
    yb_ref[0] = yb.astype(yb_ref.dtype)

    if want_final:
        @pl.when(c == nc - 1)
        def _():
            for dr, p in pairs:
                hf_ref[0, dr, hp * p:hp * (p + 1)] = (
                    st_ref[dr, :, LANES * p:LANES * (p + 1)].T.reshape(hp, SSD_HEADDIM, SSD_STATE))


def _ssd_scan(xbc, dt, cum, d_exp, h0, want_final):
    bsz, seq, width = xbc.shape
    q = SSD_CHUNK
    nc = seq // q
    fwd = lambda w: pl.BlockSpec((1, q, w), lambda b, c: (b, c, 0))
    bwd = lambda w: pl.BlockSpec((1, q, w), lambda b, c: (b, nc - 1 - c, 0))
    st_shape = (bsz, 2, SSD_HEADS, SSD_HEADDIM, SSD_STATE)
    st_spec = pl.BlockSpec((1,) + st_shape[1:], lambda b, c: (b, 0, 0, 0, 0))
    has_h0 = h0 is not None
    y_shape = jax.ShapeDtypeStruct((bsz, seq, D_INNER), BF16)
    outs = pl.pallas_call(
        functools.partial(_ssd_kernel, has_h0=has_h0, want_final=want_final),
        grid=(bsz, nc),
        in_specs=[fwd(width), bwd(width), fwd(LANES), bwd(LANES), fwd(LANES), bwd(LANES),
                  pl.BlockSpec((1, D_INNER), lambda b, c: (0, 0))] + [st_spec] * has_h0,
        out_specs=[fwd(D_INNER), bwd(D_INNER)] + [st_spec] * want_final,
        out_shape=[y_shape, y_shape] + [jax.ShapeDtypeStruct(st_shape, F32)] * want_final,
        scratch_shapes=[pltpu.VMEM((2, SSD_STATE, D_INNER), F32)],
        compiler_params=_params("arbitrary", "arbitrary"),
        name="ssd_scan",
    )(xbc, xbc, dt, dt, cum, cum, d_exp, *([h0] if has_h0 else []))
    return (outs[0], outs[1], outs[2] if want_final else None)


def _odd_tail_kernel(x_ref, mod_ref, yf_ref, yb_ref, z_ref, nw_ref, wo_ref, fw_ref, o_ref, *, final):
    y = (yf_ref[0].astype(F32) + yb_ref[0].astype(F32)) * _silu(z_ref[0].astype(F32))
    y = _rms(y, nw_ref[...]).astype(BF16)
    xn = x_ref[0] + mod_ref[0][:, 2 * D_MODEL:] * _dot(y, wo_ref[...])
    o_ref[0] = _rms(xn, fw_ref[...]) if final else xn


def _odd_tail(x, mod, yf, yb, z, norm_w, w_out, final_w, final, tm=256):
    bsz, seq, _ = x.shape
    tok = lambda w: pl.BlockSpec((1, tm, w), lambda b, m: (b, m, 0))
    const = lambda r, c: pl.BlockSpec((r, c), lambda b, m: (0, 0))
    return pl.pallas_call(
        functools.partial(_odd_tail_kernel, final=final),
        grid=(bsz, seq // tm),
        in_specs=[tok(D_MODEL), _mod_spec(mod), tok(D_INNER), tok(D_INNER), tok(D_INNER),
                  const(1, D_INNER), const(D_INNER, D_MODEL), const(1, D_MODEL)],
        out_specs=tok(D_MODEL),
        out_shape=jax.ShapeDtypeStruct((bsz, seq, D_MODEL), F32),
        compiler_params=_params("arbitrary", "arbitrary"),
        name="odd_tail",
    )(x, mod, yf, yb, z, norm_w.reshape(1, -1), w_out, final_w.reshape(1, -1))


def _even_layer(x, mod, norm_w, w_in, w_out, s5, s5_d, glu_w, glu_b, short_w, short_b, dft, kspec, h0,
                final_w, final, s5_tq, conv_tc):
    tm = min(TOKEN_BLOCK, x.shape[1])
    outs = ((S5_WIDTH, F32, False), (S5_WIDTH, BF16, False),
            (HY_WIDTH, BF16, True), (HY_WIDTH, BF16, True), (HY_WIDTH, BF16, True))
    u, g_a, v, x1, x2g = _inproj(_inproj_even_kernel, x, mod, norm_w, w_in,
                                 (_phase_perm(tm), short_w, short_b.reshape(1, -1)), outs, tm=tm)
    ys, hf = _s5_scan(u, *s5, h0, s5_tq)
    fwd, inv = dft
    z1 = _hy_conv(v, x1, fwd, inv, kspec, 0, conv_tc)
    hy = _hy_conv(z1, x2g, fwd, inv, kspec, 1, conv_tc)
    xn = _even_tail(x, mod, ys, u, g_a, hy, s5_d, glu_w, glu_b, w_out, final_w, final, tm=tm)
    return xn, hf


def _odd_layer(x, mod, norm_w, w_in, conv_w, conv_b, a_log, dt_bias, d_exp, ssd_norm_w, w_out, h0, final_w, final):
    tm = min(TOKEN_BLOCK, x.shape[1])
    z, xbc, dt, cum = _inproj(functools.partial(_inproj_odd_kernel, tm=tm), x, mod, norm_w, w_in,
                              (conv_w, conv_b.reshape(1, -1), dt_bias, a_log),
                              ((D_INNER, BF16, False), (SSD_CONV_DIM, BF16, False), (LANES, F32, False),
                               (LANES, F32, False)), tm=tm)
    yf, yb, hf = _ssd_scan(xbc, dt, cum, d_exp, h0, want_final=h0 is None)
    xn = _odd_tail(x, mod, yf, yb, z, ssd_norm_w, w_out, final_w, final)
    return xn, hf


def kernel(x_prompt, x_sample, state_s5_re, state_s5_im, state_ssd, c, c_ctx, mod_w, mod_b, norm_w, final_norm_w,
           ev_w_in, ev_w_out, s5_a_re, s5_a_im, s5_log_dt, s5_b_re, s5_b_im, s5_c_re, s5_c_im, s5_d, s5_glu_w,
           s5_glu_b, hy_short_w, hy_short_b, hy_f_w1, hy_f_b1, hy_f_freq1, hy_f_w2, hy_f_b2, hy_f_freq2, hy_f_w3,
           hy_bias, ssd_w_in, ssd_conv_w, ssd_conv_b, ssd_a_log, ssd_dt_bias, ssd_d, ssd_norm_w, ssd_w_out):
    bp, lp, _ = x_prompt.shape
    bs, ls, _ = x_sample.shape
    rows = 2 * SUBLANES
    cvec = jnp.zeros((rows, D_MODEL), F32).at[0].set(c_ctx.astype(F32)).at[1:1 + bs].set(c.astype(F32))
    mod = _modulation(cvec, mod_w, mod_b)
    dft_p, dft_s = _dft_matrices(lp // HY_PHASES), _dft_matrices(ls // HY_PHASES)
    xp, xs = x_prompt, x_sample
    new_re, new_im, new_ssd = [], [], []
    for i in range(DEPTH):
        j = i // 2
        final = i == DEPTH - 1
        mod_p = mod[i, 0:1].reshape(1, 1, 3 * D_MODEL)
        mod_s = mod[i, 1:1 + bs].reshape(bs, 1, 3 * D_MODEL)
        if i % 2 == 0:
            s5 = _s5_prepare(s5_a_re[j], s5_a_im[j], s5_log_dt[j], s5_b_re[j], s5_b_im[j], s5_c_re[j], s5_c_im[j])
            w_in = ev_w_in[j].astype(BF16)
            w_out = ev_w_out[j].astype(BF16)
            glu_w = s5_glu_w[j].astype(BF16)
            fargs = (hy_f_w1[j], hy_f_b1[j], hy_f_freq1[j], hy_f_w2[j], hy_f_b2[j], hy_f_freq2[j], hy_f_w3[j])
            ks_p = _hy_spectra(_polyphase_filters(_hy_filters(lp, *fargs)), dft_p[0], hy_bias[j])
            ks_s = _hy_spectra(_polyphase_filters(_hy_filters(ls, *fargs)), dft_s[0], hy_bias[j])
            h0_p = jnp.zeros((2, bp, S5_STATE_LANES), F32)
            h0_s = _s5_state_to_lanes(state_s5_re[:, j].astype(F32), state_s5_im[:, j].astype(F32))
            common = (s5_d[j], glu_w, s5_glu_b[j], hy_short_w[j], hy_short_b[j])
            xp, hf = _even_layer(xp, mod_p, norm_w[i], w_in, w_out, s5, *common, dft_p, ks_p, h0_p,
                                 final_norm_w, final, s5_tq=32, conv_tc=512)
            xs, _ = _even_layer(xs, mod_s, norm_w[i], w_in, w_out, s5, *common, dft_s, ks_s, h0_s,
                                final_norm_w, final, s5_tq=64, conv_tc=256)
            re, im = _s5_state_from_lanes(hf)
            new_re.append(re)
            new_im.append(im)
        else:
            pad = LANES - 2 * SSD_HEADS
            w_in = jnp.pad(ssd_w_in[j], ((0, 0), (0, pad))).astype(BF16)
            w_out = ssd_w_out[j].astype(BF16)
            dt_bias = jnp.pad(ssd_dt_bias[j].reshape(1, -1).astype(F32), ((0, 0), (0, pad)))
            a_log = jnp.pad(ssd_a_log[j].reshape(1, -1).astype(F32), ((0, 0), (0, pad)))
            d_exp = jnp.repeat(ssd_d[j].astype(F32), SSD_HEADDIM).reshape(1, D_INNER)
            common = (ssd_conv_w[j], ssd_conv_b[j], a_log, dt_bias, d_exp, ssd_norm_w[j], w_out)
            h0_s = state_ssd[:, j].astype(F32)
            xp, hf = _odd_layer(xp, mod_p, norm_w[i], w_in, *common, None, final_norm_w, final)
            xs, _ = _odd_layer(xs, mod_s, norm_w[i], w_in, *common, h0_s, final_norm_w, final)
            new_ssd.append(hf)
    return (xp, xs, jnp.stack(new_re, axis=1), jnp.stack(new_im, axis=1), jnp.stack(new_ssd, axis=1))
```

```python
import functools
import math

import jax
import jax.numpy as jnp
from jax import lax
from jax.experimental import pallas as pl
from jax.experimental.pallas import tpu as pltpu

F32 = jnp.float32
BF16 = jnp.bfloat16
HIGHEST = lax.Precision.HIGHEST

D_MODEL = 1024
DEPTH = 4
D_INNER = 2048
EPS = 1e-6
S5_WIDTH = 1024
S5_GROUP = 16
S5_GROUPS = 64
S5_STATE = 64
HY_WIDTH = 1024
HY_BANDS = 16
HY_HID = 64
HY_DECAY_TARGET = 1e-2
HY_FAST = 0.3
HY_SLOW = 1.5
HY_PHASES = 4
SSD_HEADDIM = 64
SSD_HEADS = 32
SSD_STATE = 128
SSD_GROUPS = 4
SSD_CHUNK = 128
SSD_CONV_DIM = D_INNER + 2 * SSD_GROUPS * SSD_STATE

LANES = 128
SUBLANES = 8
VMEM_LIMIT = 56 * 1024 * 1024

S5_LANE_CHUNKS = S5_WIDTH // LANES
S5_CHUNK_STATE = (LANES // S5_GROUP) * S5_STATE
S5_STATE_LANES = S5_LANE_CHUNKS * 2 * S5_CHUNK_STATE
S5_PERM_ROWS = 256


def _params(*sem):
    return pltpu.CompilerParams(dimension_semantics=sem, vmem_limit_bytes=VMEM_LIMIT)


def _silu(x):
    return x * jax.nn.sigmoid(x)


def _dot(a, b):
    return jnp.dot(a, b, preferred_element_type=F32)


def _dot_hi(a, b):
    return jnp.dot(a, b, preferred_element_type=F32, precision=HIGHEST)


def _mod_kernel(c_ref, w_ref, b_ref, o_ref):
    s = _silu(c_ref[...])
    o_ref[0] = _dot_hi(s, w_ref[0]) + b_ref[0]


def _modulation(cvec, mod_w, mod_b):
    rows = cvec.shape[0]
    tn = D_MODEL
    return pl.pallas_call(
        _mod_kernel,
        grid=(DEPTH, 3 * D_MODEL // tn),
        in_specs=[
            pl.BlockSpec((rows, D_MODEL), lambda i, n: (0, 0)),
            pl.BlockSpec((1, D_MODEL, tn), lambda i, n: (i, 0, n)),
            pl.BlockSpec((1, 1, tn), lambda i, n: (i, 0, n)),
        ],
        out_specs=pl.BlockSpec((1, rows, tn), lambda i, n: (i, 0, n)),
        out_shape=jax.ShapeDtypeStruct((DEPTH, rows, 3 * D_MODEL), F32),
        compiler_params=_params("arbitrary", "arbitrary"),
        name="modulation",
    )(cvec, mod_w, mod_b.reshape(DEPTH, 1, 3 * D_MODEL))


def _norm_mod(x, mod, nw):
    shift = mod[:, :D_MODEL]
    scale = mod[:, D_MODEL:2 * D_MODEL]
    ms = jnp.mean(x * x, axis=-1, keepdims=True)
    return (x * lax.rsqrt(ms + EPS) * nw * (1.0 + scale) + shift).astype(BF16)


def _conv3(x, prev_row, next_row, w, b):
    rows = x.shape[0]
    ridx = lax.broadcasted_iota(jnp.int32, x.shape, 0)
    xp = jnp.where(ridx == 0, prev_row, pltpu.roll(x, 1, 0))
    xn = jnp.where(ridx == rows - 1, next_row, pltpu.roll(x, rows - 1, 0))
    return w[0:1] * xp + w[1:2] * x + w[2:3] * xn + b


def _conv3_phased(x, prev_row, next_row, w, b):
    ph = x.shape[0] // HY_PHASES
    blk = [x[r * ph:(r + 1) * ph] for r in range(HY_PHASES)]
    ridx = lax.broadcasted_iota(jnp.int32, blk[0].shape, 0)
    before = [jnp.where(ridx == 0, prev_row, pltpu.roll(blk[-1], 1, 0))] + blk[:-1]
    after = blk[1:] + [jnp.where(ridx == ph - 1, next_row, pltpu.roll(blk[0], ph - 1, 0))]
    return jnp.concatenate([w[0:1] * before[r] + w[1:2] * blk[r] + w[2:3] * after[r] + b
                            for r in range(HY_PHASES)], axis=0)


CONV_COLS = 256
TOKEN_BLOCK = 512
TAIL_BLOCK = 512


def _proj_conv3(conv, hb, pv_ref, nx_ref, mod, nw, w_ref, lo, width, cw_ref, cb_ref):
    m = pl.program_id(1)
    nm = pl.num_programs(1)
    h_prev = _norm_mod(pv_ref[0], mod, nw)
    h_next = _norm_mod(nx_ref[0], mod, nw)
    has_prev, has_next = (m > 0).astype(F32), (m < nm - 1).astype(F32)
    for off in range(0, width, CONV_COLS):
        w = w_ref[:, lo + off:lo + off + CONV_COLS]
        prev = _dot(h_prev, w)[SUBLANES - 1:SUBLANES] * has_prev
        nxt = _dot(h_next, w)[0:1] * has_next
        yield off, conv(_dot(hb, w), prev, nxt, cw_ref[:, off:off + CONV_COLS], cb_ref[:, off:off + CONV_COLS])


def _inproj_even_kernel(x_ref, pv_ref, nx_ref, mod_ref, nw_ref, w_ref, perm_ref, cw_ref, cb_ref,
                        u_ref, ga_ref, v_ref, x1_ref, x2_ref):
    mod, nw = mod_ref[0], nw_ref[...]
    hb = _norm_mod(x_ref[0], mod, nw)
    lo_vx, hi_vx = 2 * S5_WIDTH, 2 * S5_WIDTH + 3 * HY_WIDTH
    u_ref[0] = _dot(hb, w_ref[:, :S5_WIDTH])
    ga_ref[0] = _dot(hb, w_ref[:, S5_WIDTH:lo_vx]).astype(ga_ref.dtype)
    hbp = _dot(perm_ref[...], hb).astype(BF16)
    ph = hb.shape[0] // HY_PHASES
    outs = (v_ref, x1_ref, x2_ref)
    for off, y in _proj_conv3(_conv3_phased, hbp, pv_ref, nx_ref, mod, nw, w_ref, lo_vx, 3 * HY_WIDTH,
                              cw_ref, cb_ref):
        which, col = off // HY_WIDTH, off % HY_WIDTH
        if which == 2:
            y = y * _silu(_dot(hbp, w_ref[:, hi_vx + col:hi_vx + col + CONV_COLS]))
        o_ref = outs[which]
        o_ref[0, :, :, col:col + CONV_COLS] = y.reshape(HY_PHASES, ph, CONV_COLS).astype(o_ref.dtype)


def _softplus(x):
    return jnp.maximum(x, 0.0) + jnp.log(1.0 + jnp.exp(-jnp.abs(x)))


def _inproj_odd_kernel(x_ref, pv_ref, nx_ref, mod_ref, nw_ref, w_ref, cw_ref, cb_ref, bias_ref, alog_ref,
                       z_ref, xbc_ref, dt_ref, cum_ref, *, tm):
    mod, nw = mod_ref[0], nw_ref[...]
    hb = _norm_mod(x_ref[0], mod, nw)
    lo_dt = D_INNER + SSD_CONV_DIM
    z_ref[0] = _dot(hb, w_ref[:, :D_INNER]).astype(z_ref.dtype)
    for off, y in _proj_conv3(_conv3, hb, pv_ref, nx_ref, mod, nw, w_ref, D_INNER, SSD_CONV_DIM, cw_ref, cb_ref):
        xbc_ref[0, :, off:off + CONV_COLS] = _silu(y).astype(xbc_ref.dtype)
    dt = _softplus(_dot(hb, w_ref[:, lo_dt:]) + bias_ref[...])
    dt_ref[0] = dt
    da = dt * (-jnp.exp(alog_ref[...]))
    q = SSD_CHUNK
    ii = lax.broadcasted_iota(jnp.int32, (q, q), 0)
    jj = lax.broadcasted_iota(jnp.int32, (q, q), 1)
    low, up = (jj <= ii).astype(F32), (jj >= ii).astype(F32)
    fwd_lane = lax.broadcasted_iota(jnp.int32, (q, LANES), 1) < SSD_HEADS
    for c in range(tm // q):
        dac = da[c * q:(c + 1) * q]
        cum_ref[0, c * q:(c + 1) * q, :] = jnp.where(fwd_lane, _dot_hi(low, dac), _dot_hi(up, dac))


def _mod_spec(mod):
    per_batch = mod.shape[0] > 1
    return pl.BlockSpec((1, 1, mod.shape[-1]), (lambda b, m: (b, 0, 0)) if per_batch else (lambda b, m: (0, 0, 0)))


def _halo_specs(width, tl, seq):
    r = tl // SUBLANES
    last = seq // SUBLANES - 1
    return (pl.BlockSpec((1, SUBLANES, width), lambda b, m: (b, jnp.maximum(m * r - 1, 0), 0)),
            pl.BlockSpec((1, SUBLANES, width), lambda b, m: (b, jnp.minimum((m + 1) * r, last), 0)))


def _phase_perm(tm):
    ph = tm // HY_PHASES
    i = jnp.arange(tm, dtype=jnp.int32)
    src = HY_PHASES * (i % ph) + i // ph
    return (src[:, None] == i[None, :]).astype(BF16)


def _phased_spec(tm, width):
    return pl.BlockSpec((1, HY_PHASES, tm // HY_PHASES, width), lambda b, m: (b, 0, m, 0))


def _inproj(kernel_fn, x, mod, norm_w, w_bf16, consts, outs, tm=256):
    bsz, seq, _ = x.shape
    pv, nx = _halo_specs(D_MODEL, tm, seq)
    const = lambda a: pl.BlockSpec(a.shape, lambda b, m: (0, 0))
    out_specs, out_shape = [], []
    for width, dtype, phased in outs:
        if phased:
            out_specs.append(_phased_spec(tm, width))
            out_shape.append(jax.ShapeDtypeStruct((bsz, HY_PHASES, seq // HY_PHASES, width), dtype))
        else:
            out_specs.append(pl.BlockSpec((1, tm, width), lambda b, m: (b, m, 0)))
            out_shape.append(jax.ShapeDtypeStruct((bsz, seq, width), dtype))
    return pl.pallas_call(
        kernel_fn,
        grid=(bsz, seq // tm),
        in_specs=[pl.BlockSpec((1, tm, D_MODEL), lambda b, m: (b, m, 0)), pv, nx, _mod_spec(mod),
                  pl.BlockSpec((1, D_MODEL), lambda b, m: (0, 0)),
                  pl.BlockSpec(w_bf16.shape, lambda b, m: (0, 0), pipeline_mode=pl.Buffered(1))]
                 + [const(a) for a in consts],
        out_specs=out_specs,
        out_shape=out_shape,
        compiler_params=_params("arbitrary", "arbitrary"),
        name="inproj",
    )(x, x, x, mod, norm_w.reshape(1, D_MODEL), w_bf16, *consts)


def _s5_kernel(u_ref, pin_ref, pout_ref, bw_ref, cw_ref, lr_ref, li_ref, h0_ref, ys_ref, hf_ref, bu_ref, hst_ref,
               *, bsz, tq, jgroup):
    d = pl.program_id(0)
    blk = pl.program_id(1)
    nblk = pl.num_programs(1)
    cs = S5_CHUNK_STATE
    nsub = cs // LANES
    prow = pin_ref.shape[1]
    tsub = prow // bsz
    ns = tq // tsub

    @pl.when(blk == 0)
    def _():
        hst_ref[...] = h0_ref[0]

    def pick(blocks, s):
        return blocks[s] if ns == 1 else jnp.where(d == 0, blocks[s], blocks[ns - 1 - s])

    u_nat = [u_ref[:, s * tsub:(s + 1) * tsub, :].reshape(prow, S5_WIDTH) for s in range(ns)]
    u_tm = jnp.concatenate([_dot(pin_ref[0], pick(u_nat, s).astype(BF16)).astype(BF16) for s in range(ns)], axis=0)
    for j in range(S5_LANE_CHUNKS):
        bu_ref[:, 2 * cs * j:2 * cs * (j + 1)] = _dot(u_tm[:, LANES * j:LANES * (j + 1)], bw_ref[0, j])

    for j0 in range(0, S5_LANE_CHUNKS, jgroup):
        cols = [(2 * cs * j + LANES * c, 2 * cs * j + cs + LANES * c, j, c)
                for j in range(j0, j0 + jgroup) for c in range(nsub)]
        lam = [(jnp.broadcast_to(lr_ref[0, j, :, LANES * c:LANES * (c + 1)], (bsz, LANES)),
                jnp.broadcast_to(li_ref[0, j, :, LANES * c:LANES * (c + 1)], (bsz, LANES))) for _, _, j, c in cols]
        state = [(hst_ref[:, re_lo:re_lo + LANES], hst_ref[:, im_lo:im_lo + LANES]) for re_lo, im_lo, _, _ in cols]
        for step in range(tq):
            r0 = step * bsz
            for i, (re_lo, im_lo, _, _) in enumerate(cols):
                hr, hi = state[i]
                lam_re, lam_im = lam[i]
                nr = lam_re * hr - lam_im * hi + bu_ref[r0:r0 + bsz, re_lo:re_lo + LANES]
                ni = lam_re * hi + lam_im * hr + bu_ref[r0:r0 + bsz, im_lo:im_lo + LANES]
                bu_ref[r0:r0 + bsz, re_lo:re_lo + LANES] = nr
                bu_ref[r0:r0 + bsz, im_lo:im_lo + LANES] = ni
                state[i] = (nr, ni)
        for i, (re_lo, im_lo, _, _) in enumerate(cols):
            hst_ref[:, re_lo:re_lo + LANES] = state[i][0]
            hst_ref[:, im_lo:im_lo + LANES] = state[i][1]

    y_tm = jnp.concatenate([_dot(bu_ref[:, 2 * cs * j:2 * cs * (j + 1)].astype(BF16), cw_ref[0, j])
                            for j in range(S5_LANE_CHUNKS)], axis=-1)
    y_grp = [y_tm[s * prow:(s + 1) * prow] for s in range(ns)]
    for s in range(ns):
        y = pick(y_grp, s)
        y_hi = y.astype(BF16)
        y_lo = (y - y_hi.astype(F32)).astype(BF16)
        y_bm = _dot(pout_ref[0], y_hi) + _dot(pout_ref[0], y_lo)
        ys_ref[0, :, s * tsub:(s + 1) * tsub, :] = y_bm.reshape(bsz, tsub, S5_WIDTH).astype(ys_ref.dtype)

    @pl.when(blk == nblk - 1)
    def _():
        hf_ref[0] = hst_ref[...]


def _s5_scan(u, bw, cw, lr, li, h0, tq):
    bsz, seq, _ = u.shape
    nblk = seq // tq
    rows = bsz * tq
    prow = min(rows, S5_PERM_ROWS)
    tsub = prow // bsz
    r = jnp.arange(prow, dtype=jnp.int32)
    step, b = r // bsz, r % bsz
    src = jnp.stack([b * tsub + step, b * tsub + (tsub - 1 - step)])
    p_in = (src[:, :, None] == r[None, None, :]).astype(BF16)
    p_out = jnp.transpose(p_in, (0, 2, 1))

    def tblk(d, i):
        return i + d * (nblk - 1 - 2 * i)

    nj, cs = S5_LANE_CHUNKS, S5_CHUNK_STATE
    jgroup = 2 if bsz <= SUBLANES else 1
    return pl.pallas_call(
        functools.partial(_s5_kernel, bsz=bsz, tq=tq, jgroup=jgroup),
        grid=(2, nblk),
        in_specs=[
            pl.BlockSpec((bsz, tq, S5_WIDTH), lambda d, i: (0, tblk(d, i), 0)),
            pl.BlockSpec((1, prow, prow), lambda d, i: (d, 0, 0)),
            pl.BlockSpec((1, prow, prow), lambda d, i: (d, 0, 0)),
            pl.BlockSpec((1, nj, LANES, 2 * cs), lambda d, i: (d, 0, 0, 0)),
            pl.BlockSpec((1, nj, 2 * cs, LANES), lambda d, i: (d, 0, 0, 0)),
            pl.BlockSpec((1, nj, 1, cs), lambda d, i: (d, 0, 0, 0)),
            pl.BlockSpec((1, nj, 1, cs), lambda d, i: (d, 0, 0, 0)),
            pl.BlockSpec((1, bsz, S5_STATE_LANES), lambda d, i: (d, 0, 0)),
        ],
        out_specs=[
            pl.BlockSpec((1, bsz, tq, S5_WIDTH), lambda d, i: (d, 0, tblk(d, i), 0)),
            pl.BlockSpec((1, bsz, S5_STATE_LANES), lambda d, i: (d, 0, 0)),
        ],
        out_shape=[
            jax.ShapeDtypeStruct((2, bsz, seq, S5_WIDTH), BF16),
            jax.ShapeDtypeStruct((2, bsz, S5_STATE_LANES), F32),
        ],
        scratch_shapes=[
            pltpu.VMEM((rows, S5_STATE_LANES), F32),
            pltpu.VMEM((bsz, S5_STATE_LANES), F32),
        ],
        compiler_params=_params("arbitrary", "arbitrary"),
        name="s5_scan",
    )(u, p_in, p_out, bw, cw, lr, li, h0)


def _s5_prepare(a_re, a_im, log_dt, b_re, b_im, c_re, c_im):
    dt = jnp.exp(log_dt.astype(F32))[..., None]
    a_re, a_im = a_re.astype(F32), a_im.astype(F32)
    mag = jnp.exp(a_re * dt)
    lbr, lbi = mag * jnp.cos(a_im * dt), mag * jnp.sin(a_im * dt)
    den = a_re * a_re + a_im * a_im
    cr = ((lbr - 1.0) * a_re + lbi * a_im) / den
    ci = (lbi * a_re - (lbr - 1.0) * a_im) / den
    bbr = cr[..., None] * b_re - ci[..., None] * b_im
    bbi = cr[..., None] * b_im + ci[..., None] * b_re
    nj, gpc = S5_LANE_CHUNKS, LANES // S5_GROUP
    eye = jnp.eye(gpc, dtype=F32)

    def pack_b(m):
        m = m.reshape(2, nj, gpc, S5_STATE, S5_GROUP)
        return jnp.einsum('djgph,gk->djghkp', m, eye).reshape(2, nj, LANES, S5_CHUNK_STATE)

    def pack_c(m):
        m = m.reshape(2, nj, gpc, S5_GROUP, S5_STATE)
        return jnp.einsum('djghp,gk->djgpkh', m, eye).reshape(2, nj, S5_CHUNK_STATE, LANES)

    bw = jnp.concatenate([pack_b(bbr), pack_b(bbi)], axis=-1).astype(BF16)
    cw = jnp.concatenate([pack_c(c_re.astype(F32)), pack_c(-c_im.astype(F32))], axis=-2).astype(BF16)
    lr = lbr.reshape(2, nj, 1, S5_CHUNK_STATE)
    li = lbi.reshape(2, nj, 1, S5_CHUNK_STATE)
    return bw, cw, lr, li


def _s5_state_to_lanes(re, im):
    bsz = re.shape[0]
    st = jnp.stack([re.reshape(bsz, 2, S5_LANE_CHUNKS, S5_CHUNK_STATE),
                    im.reshape(bsz, 2, S5_LANE_CHUNKS, S5_CHUNK_STATE)], axis=3)
    return jnp.transpose(st.reshape(bsz, 2, S5_STATE_LANES), (1, 0, 2))


def _s5_state_from_lanes(hf):
    bsz = hf.shape[1]
    st = jnp.transpose(hf, (1, 0, 2)).reshape(bsz, 2, S5_LANE_CHUNKS, 2, S5_CHUNK_STATE)
    re = st[:, :, :, 0].reshape(bsz, 2, S5_GROUPS, S5_STATE)
    im = st[:, :, :, 1].reshape(bsz, 2, S5_GROUPS, S5_STATE)
    return re, im


def _hy_filter_kernel(bands_ref, deltas_ref, w1t_ref, w1c_ref, w1s_ref, b1_ref, f1_ref, w2_ref, b2_ref, f2_ref,
                      w3_ref, o_ref, *, seq, tl):
    r = pl.program_id(0)
    m = pl.program_id(1)
    t = (HY_PHASES * (lax.broadcasted_iota(jnp.int32, (tl, 1), 0) + m * tl) + r).astype(F32)
    t_norm = t / (seq - 1)
    ang = (2.0 * math.pi / seq) * t * bands_ref[...]
    pre = t_norm * w1t_ref[...] + _dot_hi(jnp.cos(ang), w1c_ref[...]) + _dot_hi(jnp.sin(ang), w1s_ref[...])
    f = jnp.sin(f1_ref[...] * (pre + b1_ref[...]))
    f = jnp.sin(f2_ref[...] * (_dot_hi(f, w2_ref[...]) + b2_ref[...]))
    f = _dot_hi(f, w3_ref[...])
    window = jnp.exp(-t_norm * deltas_ref[...])
    for o in range(2):
        for direction in range(2):
            lo = (2 * o + direction) * HY_WIDTH
            o_ref[o, direction, 0] = f[:, lo:lo + HY_WIDTH] * window


def _hy_filters(seq, w1, b1, freq1, w2, b2, freq2, w3):
    n4 = seq // HY_PHASES
    tl = min(256, n4)
    bands = jnp.linspace(1e-4, HY_BANDS - 1, HY_BANDS, dtype=F32).reshape(1, HY_BANDS)
    deltas = jnp.abs(jnp.linspace(math.log(HY_DECAY_TARGET) / HY_SLOW, math.log(HY_DECAY_TARGET) / HY_FAST,
                                  HY_WIDTH, dtype=F32)).reshape(1, HY_WIDTH)
    small = [bands, deltas, w1[0:1], w1[1:1 + HY_BANDS], w1[1 + HY_BANDS:], b1.reshape(1, -1), freq1.reshape(1, -1),
             w2, b2.reshape(1, -1), freq2.reshape(1, -1), w3]
    return pl.pallas_call(
        functools.partial(_hy_filter_kernel, seq=seq, tl=tl),
        grid=(HY_PHASES, n4 // tl),
        in_specs=[pl.BlockSpec(a.shape, lambda r, m: (0, 0)) for a in small],
        out_specs=pl.BlockSpec((2, 2, 1, tl, HY_WIDTH), lambda r, m: (0, 0, r, m, 0)),
        out_shape=jax.ShapeDtypeStruct((2, 2, HY_PHASES, n4, HY_WIDTH), F32),
        compiler_params=_params("arbitrary", "arbitrary"),
        name="hy_filters",
    )(*small)


def _polyphase_filters(hf):
    fwd, bwd = hf[:, 0], hf[:, 1]
    zero = jnp.zeros_like(fwd[:, 0, :1])

    def lags(h, first, start):
        r, j0 = start % HY_PHASES, start // HY_PHASES
        tail = h[:, r, j0:] if j0 else h[:, r, :-1]
        return jnp.concatenate([first, tail], axis=1)

    out = []
    for q in range(1 - HY_PHASES, HY_PHASES):
        pos = fwd[:, q] if q >= 0 else lags(fwd, bwd[:, -q, :1], HY_PHASES + q)
        neg = lags(bwd, zero, HY_PHASES - q)
        out.append(jnp.stack([pos + neg, pos - neg], axis=1))
    return jnp.stack(out, axis=1).astype(BF16)


def _dft_matrices(seq):
    n = 2 * seq
    k = jnp.arange(seq, dtype=jnp.int32)
    ang = ((k[:, None] * k[None, :]) % n).astype(F32) * (2.0 * math.pi / n)
    cos, sin = jnp.cos(ang), jnp.sin(ang)
    alt = jnp.where(k % 2 == 0, 1.0, -1.0).astype(F32)
    fwd = jnp.concatenate([cos, jnp.where(k[:, None] == 0, alt[None, :], -sin)], axis=0)
    scale = jnp.where(k == 0, 1.0 / n, 2.0 / n).astype(F32)
    inv = jnp.concatenate([cos * scale[None, :], jnp.where(k[None, :] == 0, alt[:, None], -sin) * scale[None, :]],
                          axis=1)
    return fwd.astype(BF16), inv.astype(BF16)


def _hy_spec_kernel(fwd_ref, h_ref, bias_ref, o_ref, *, half):
    q = pl.program_id(1)
    f = fwd_ref[...]
    s_sum = _dot(f, h_ref[0, 0, 0])
    s_dif = _dot(f, h_ref[0, 0, 1])
    ridx = lax.broadcasted_iota(jnp.int32, s_sum.shape, 0)
    real_row = ridx <= half
    lag0 = jnp.where(real_row & (q == HY_PHASES - 1), bias_ref[0], 0.0)
    o_ref[0, 0] = jnp.where(real_row, s_sum, s_dif) + lag0


def _hy_spectra(hs, fwd, bias, tc=512):
    _, nq, _, n4, width = hs.shape
    return pl.pallas_call(
        functools.partial(_hy_spec_kernel, half=n4),
        grid=(2, nq, width // tc),
        in_specs=[
            pl.BlockSpec((2 * n4, n4), lambda o, q, c: (0, 0)),
            pl.BlockSpec((1, 1, 2, n4, tc), lambda o, q, c: (o, q, 0, 0, c)),
            pl.BlockSpec((1, 1, tc), lambda o, q, c: (o, 0, c)),
        ],
        out_specs=pl.BlockSpec((1, 1, 2 * n4, tc), lambda o, q, c: (o, q, 0, c)),
        out_shape=jax.ShapeDtypeStruct((2, nq, 2 * n4, width), F32),
        compiler_params=_params("arbitrary", "arbitrary", "arbitrary"),
        name="hy_spectra",
    )(fwd, hs, bias.reshape(2, 1, width))


CONV_LANES = 256


def _hy_conv_kernel(v_ref, g_ref, fwd_ref, inv_ref, ks_ref, o_ref, s_ref, p_ref, *, half):
    f = fwd_ref[...]
    for r in range(HY_PHASES):
        s_ref[r] = _dot(f, v_ref[0, r])
    nq = 2 * HY_PHASES - 1
    row0 = lax.broadcasted_iota(jnp.int32, (SUBLANES, CONV_LANES), 0) == 0
    for lo in range(0, half, SUBLANES):
        for c0 in range(0, v_ref.shape[-1], CONV_LANES):
            rows_re, rows_im, cols = slice(lo, lo + SUBLANES), slice(half + lo, half + lo + SUBLANES), \
                slice(c0, c0 + CONV_LANES)
            s_re = [s_ref[r, rows_re, cols] for r in range(HY_PHASES)]
            s_im = [s_ref[r, rows_im, cols] for r in range(HY_PHASES)]
            k_re = [ks_ref[0, q, rows_re, cols] for q in range(nq)]
            k_im = [ks_ref[0, q, rows_im, cols] for q in range(nq)]
            if lo == 0:
                s_ny, k_ny = [s[0:1] for s in s_im], [k[0:1] for k in k_im]
                s_im = [jnp.where(row0, 0.0, s) for s in s_im]
                k_im = [jnp.where(row0, 0.0, k) for k in k_im]
            for r in range(HY_PHASES):
                qs = [r - rp + HY_PHASES - 1 for rp in range(HY_PHASES)]
                p_re = functools.reduce(jnp.add, [s_re[rp] * k_re[q] - s_im[rp] * k_im[q] for rp, q in enumerate(qs)])
                p_im = functools.reduce(jnp.add, [s_re[rp] * k_im[q] + s_im[rp] * k_re[q] for rp, q in enumerate(qs)])
                if lo == 0:
                    p_ny = functools.reduce(jnp.add, [s_ny[rp] * k_ny[q] for rp, q in enumerate(qs)])
                    p_im = jnp.where(row0, p_ny, p_im)
                p_ref[r, rows_re, cols] = p_re
                p_ref[r, rows_im, cols] = p_im
    for r in range(HY_PHASES):
        o_ref[0, r] = (g_ref[0, r] * _dot(inv_ref[...], p_ref[r].astype(BF16))).astype(o_ref.dtype)


def _hy_conv(v, gate, fwd, inv, kspec, order, tc):
    bsz, _, n4, width = v.shape
    nq = 2 * HY_PHASES - 1
    blk = pl.BlockSpec((1, HY_PHASES, n4, tc), lambda c, b: (b, 0, 0, c))
    return pl.pallas_call(
        functools.partial(_hy_conv_kernel, half=n4),
        grid=(width // tc, bsz),
        in_specs=[
            blk, blk,
            pl.BlockSpec((2 * n4, n4), lambda c, b: (0, 0)),
            pl.BlockSpec((n4, 2 * n4), lambda c, b: (0, 0)),
            pl.BlockSpec((1, nq, 2 * n4, tc), lambda c, b: (order, 0, 0, c)),
        ],
        out_specs=blk,
        out_shape=jax.ShapeDtypeStruct(v.shape, BF16),
        scratch_shapes=[pltpu.VMEM((HY_PHASES, 2 * n4, tc), F32), pltpu.VMEM((HY_PHASES, 2 * n4, tc), F32)],
        compiler_params=_params("arbitrary", "arbitrary"),
        name="hy_conv",
    )(v, gate, fwd, inv, kspec)


def _gelu_tanh(x):
    return 0.5 * x * (1.0 + jnp.tanh(math.sqrt(2.0 / math.pi) * (x + 0.044715 * (x * x * x))))


def _rms(x, w):
    return x * lax.rsqrt(jnp.mean(x * x, axis=-1, keepdims=True) + EPS) * w


def _even_tail_kernel(x_ref, mod_ref, yf_ref, yb_ref, u_ref, ga_ref, hy_ref, unperm_ref, d_ref, gw_ref, gb_ref,
                      wo_ref, fw_ref, o_ref, *, final):
    y = yf_ref[0, 0].astype(F32) + yb_ref[0, 0].astype(F32) + u_ref[0] * d_ref[...]
    y = _gelu_tanh(y)
    y = y * jax.nn.sigmoid(_dot(y.astype(BF16), gw_ref[...]) + gb_ref[...])
    ya = (y * _silu(ga_ref[0].astype(F32))).astype(BF16)
    hy = _dot(unperm_ref[...], hy_ref[0].reshape(ya.shape[0], HY_WIDTH)).astype(BF16)
    out = _dot(ya, wo_ref[:S5_WIDTH, :]) + _dot(hy, wo_ref[S5_WIDTH:, :])
    xn = x_ref[0] + mod_ref[0][:, 2 * D_MODEL:] * out
    o_ref[0] = _rms(xn, fw_ref[...]) if final else xn


def _even_tail(x, mod, ys, u, g_a, hy, s5_d, glu_w, glu_b, w_out, final_w, final, tm=256):
    bsz, seq, _ = x.shape
    tok = lambda w: pl.BlockSpec((1, tm, w), lambda b, m: (b, m, 0))
    const = lambda r, c: pl.BlockSpec((r, c), lambda b, m: (0, 0))
    return pl.pallas_call(
        functools.partial(_even_tail_kernel, final=final),
        grid=(bsz, seq // tm),
        in_specs=[
            tok(D_MODEL), _mod_spec(mod),
            pl.BlockSpec((1, 1, tm, S5_WIDTH), lambda b, m: (0, b, m, 0)),
            pl.BlockSpec((1, 1, tm, S5_WIDTH), lambda b, m: (1, b, m, 0)),
            tok(S5_WIDTH), tok(S5_WIDTH), _phased_spec(tm, HY_WIDTH), const(tm, tm),
            const(1, S5_WIDTH), const(S5_WIDTH, S5_WIDTH), const(1, S5_WIDTH), const(D_INNER, D_MODEL),
            const(1, D_MODEL),
        ],
        out_specs=tok(D_MODEL),
        out_shape=jax.ShapeDtypeStruct((bsz, seq, D_MODEL), F32),
        compiler_params=_params("arbitrary", "arbitrary"),
        name="even_tail",
    )(x, mod, ys, ys, u, g_a, hy, _phase_perm(tm).T, s5_d.reshape(1, -1), glu_w, glu_b.reshape(1, -1), w_out,
      final_w.reshape(1, -1))


def _ssd_direction(xbc, dt, cum, st_ref, direction, d_row):
    q = SSD_CHUNK
    hp = LANES // SSD_HEADDIM
    ii = lax.broadcasted_iota(jnp.int32, (q, q), 0)
    jj = lax.broadcasted_iota(jnp.int32, (q, q), 1)
    if direction == 0:
        tri = (jj <= ii)
        last = q - 1
    else:
        tri = (jj >= ii)
        last = 0
    cum_t = cum.T
    src_t = cum_t - jnp.log(dt.T)
    w_t = jnp.exp(cum_t[:, last:last + 1] - src_t)
    last_row = cum[last:last + 1, :]
    first_half = lax.broadcasted_iota(jnp.int32, (q, LANES), 1) < SSD_HEADDIM
    first_half_row = lax.broadcasted_iota(jnp.int32, (1, LANES), 1) < SSD_HEADDIM
    x_all = xbc[:, :D_INNER]
    ys = []
    for g in range(SSD_GROUPS):
        b_g = xbc[:, D_INNER + g * SSD_STATE:D_INNER + (g + 1) * SSD_STATE]
        c_g = xbc[:, D_INNER + (SSD_GROUPS + g) * SSD_STATE:D_INNER + (SSD_GROUPS + g + 1) * SSD_STATE]
        b_bf, c_bf = b_g.astype(BF16), c_g.astype(BF16)
        cb = lax.dot_general(c_bf, b_bf, (((1,), (1,)), ((), ())), preferred_element_type=F32)
        b_t = b_g.T
        heads_per_group = SSD_HEADS // SSD_GROUPS
        for pair in range(heads_per_group // hp):
            h0 = g * heads_per_group + pair * hp
            lo = h0 * SSD_HEADDIM
            x_pair = x_all[:, lo:lo + LANES]
            x_blk = jnp.concatenate([jnp.where(first_half, x_pair, 0.0).astype(BF16),
                                     jnp.where(first_half, 0.0, x_pair).astype(BF16)], axis=0)
            scores, b_w, cols, tot = [], [], [], []
            for h in (h0, h0 + 1):
                hl = direction * SSD_HEADS + h
                col = jnp.broadcast_to(cum[:, hl:hl + 1], (q, q))
                arg = jnp.where(tri, col - src_t[hl:hl + 1, :], -jnp.inf)
                scores.append((cb * jnp.exp(arg)).astype(BF16))
                b_w.append((b_t * w_t[hl:hl + 1, :]).astype(BF16))
                cols.append(col)
                tot.append(jnp.broadcast_to(last_row[:, hl:hl + 1], (1, LANES)))
            s_pair = st_ref[:, lo:lo + LANES]
            y = _dot(jnp.concatenate(scores, axis=1), x_blk)
            y = y + _dot(c_bf, s_pair.astype(BF16)) * jnp.exp(jnp.where(first_half, cols[0], cols[1]))
            if d_row is not None:
                y = y + x_pair * d_row[:, lo:lo + LANES]
            ys.append(y)
            decay = jnp.exp(jnp.where(first_half_row, tot[0], tot[1]))
            st_ref[:, lo:lo + LANES] = s_pair * decay + _dot(jnp.concatenate(b_w, axis=1), x_blk)
    return jnp.concatenate(ys, axis=-1)


def _ssd_kernel(*refs, has_h0, want_final):
    xf_ref, xb_ref, dtf_ref, dtb_ref, cumf_ref, cumb_ref, d_ref = refs[:7]
    h0_ref = refs[7] if has_h0 else None
    yf_ref, yb_ref = refs[7 + has_h0:9 + has_h0]
    hf_ref = refs[9 + has_h0] if want_final else None
    st_ref = refs[-1]
    c = pl.program_id(1)
    nc = pl.num_programs(1)

    hp = LANES // SSD_HEADDIM
    pairs = [(dr, p) for dr in range(2) for p in range(SSD_HEADS // hp)]

    @pl.when(c == 0)
    def _():
        if not has_h0:
            st_ref[...] = jnp.zeros_like(st_ref)
        for dr, p in pairs if has_h0 else ():
            st_ref[dr, :, LANES * p:LANES * (p + 1)] = h0_ref[0, dr, hp * p:hp * (p + 1)].reshape(LANES, SSD_STATE).T

    yf = _ssd_direction(xf_ref[0].astype(F32), dtf_ref[0], cumf_ref[0], st_ref.at[0], 0, d_ref[...])
    yf_ref[0] = yf.astype(yf_ref.dtype)
    yb = _ssd_direction(xb_ref[0].astype(F32), dtb_ref[0], cumb_ref[0], st_ref.at[1], 1, None)
    yb_ref[0] = yb.astype(yb_ref.dtype)

    if want_final:
        @pl.when(c == nc - 1)
        def _():
            for dr, p in pairs:
                hf_ref[0, dr, hp * p:hp * (p + 1)] = (
                    st_ref[dr, :, LANES * p:LANES * (p + 1)].T.reshape(hp, SSD_HEADDIM, SSD_STATE))


def _ssd_scan(xbc, dt, cum, d_exp, h0, want_final):
    bsz, seq, width = xbc.shape
    q = SSD_CHUNK
    nc = seq // q
    fwd = lambda w: pl.BlockSpec((1, q, w), lambda b, c: (b, c, 0))
    bwd = lambda w: pl.BlockSpec((1, q, w), lambda b, c: (b, nc - 1 - c, 0))
    st_shape = (bsz, 2, SSD_HEADS, SSD_HEADDIM, SSD_STATE)
    st_spec = pl.BlockSpec((1,) + st_shape[1:], lambda b, c: (b, 0, 0, 0, 0))
    has_h0 = h0 is not None
    y_shape = jax.ShapeDtypeStruct((bsz, seq, D_INNER), BF16)
    outs = pl.pallas_call(
        functools.partial(_ssd_kernel, has_h0=has_h0, want_final=want_final),
        grid=(bsz, nc),
        in_specs=[fwd(width), bwd(width), fwd(LANES), bwd(LANES), fwd(LANES), bwd(LANES),
                  pl.BlockSpec((1, D_INNER), lambda b, c: (0, 0))] + [st_spec] * has_h0,
        out_specs=[fwd(D_INNER), bwd(D_INNER)] + [st_spec] * want_final,
        out_shape=[y_shape, y_shape] + [jax.ShapeDtypeStruct(st_shape, F32)] * want_final,
        scratch_shapes=[pltpu.VMEM((2, SSD_STATE, D_INNER), F32)],
        compiler_params=_params("arbitrary", "arbitrary"),
        name="ssd_scan",
    )(xbc, xbc, dt, dt, cum, cum, d_exp, *([h0] if has_h0 else []))
    return (outs[0], outs[1], outs[2] if want_final else None)


def _odd_tail_kernel(x_ref, mod_ref, yf_ref, yb_ref, z_ref, nw_ref, wo_ref, fw_ref, o_ref, *, final):
    y = (yf_ref[0].astype(F32) + yb_ref[0].astype(F32)) * _silu(z_ref[0].astype(F32))
    y = _rms(y, nw_ref[...]).astype(BF16)
    xn = x_ref[0] + mod_ref[0][:, 2 * D_MODEL:] * _dot(y, wo_ref[...])
    o_ref[0] = _rms(xn, fw_ref[...]) if final else xn


def _odd_tail(x, mod, yf, yb, z, norm_w, w_out, final_w, final, tm=256):
    bsz, seq, _ = x.shape
    tok = lambda w: pl.BlockSpec((1, tm, w), lambda b, m: (b, m, 0))
    const = lambda r, c: pl.BlockSpec((r, c), lambda b, m: (0, 0))
    return pl.pallas_call(
        functools.partial(_odd_tail_kernel, final=final),
        grid=(bsz, seq // tm),
        in_specs=[tok(D_MODEL), _mod_spec(mod), tok(D_INNER), tok(D_INNER), tok(D_INNER),
                  const(1, D_INNER), const(D_INNER, D_MODEL), const(1, D_MODEL)],
        out_specs=tok(D_MODEL),
        out_shape=jax.ShapeDtypeStruct((bsz, seq, D_MODEL), F32),
        compiler_params=_params("arbitrary", "arbitrary"),
        name="odd_tail",
    )(x, mod, yf, yb, z, norm_w.reshape(1, -1), w_out, final_w.reshape(1, -1))


def _even_layer(x, mod, norm_w, w_in, w_out, s5, s5_d, glu_w, glu_b, short_w, short_b, dft, kspec, h0,
                final_w, final, s5_tq, conv_tc):
    tm = min(TOKEN_BLOCK, x.shape[1])
    outs = ((S5_WIDTH, F32, False), (S5_WIDTH, BF16, False),
            (HY_WIDTH, BF16, True), (HY_WIDTH, BF16, True), (HY_WIDTH, BF16, True))
    u, g_a, v, x1, x2g = _inproj(_inproj_even_kernel, x, mod, norm_w, w_in,
                                 (_phase_perm(tm), short_w, short_b.reshape(1, -1)), outs, tm=tm)
    ys, hf = _s5_scan(u, *s5, h0, s5_tq)
    fwd, inv = dft
    z1 = _hy_conv(v, x1, fwd, inv, kspec, 0, conv_tc)
    hy = _hy_conv(z1, x2g, fwd, inv, kspec, 1, conv_tc)
    xn = _even_tail(x, mod, ys, u, g_a, hy, s5_d, glu_w, glu_b, w_out, final_w, final,
                    tm=min(TAIL_BLOCK, x.shape[1]))
    return xn, hf


def _odd_layer(x, mod, norm_w, w_in, conv_w, conv_b, a_log, dt_bias, d_exp, ssd_norm_w, w_out, h0, final_w, final):
    tm = min(TOKEN_BLOCK, x.shape[1])
    z, xbc, dt, cum = _inproj(functools.partial(_inproj_odd_kernel, tm=tm), x, mod, norm_w, w_in,
                              (conv_w, conv_b.reshape(1, -1), dt_bias, a_log),
                              ((D_INNER, BF16, False), (SSD_CONV_DIM, BF16, False), (LANES, F32, False),
                               (LANES, F32, False)), tm=tm)
    yf, yb, hf = _ssd_scan(xbc, dt, cum, d_exp, h0, want_final=h0 is None)
    xn = _odd_tail(x, mod, yf, yb, z, ssd_norm_w, w_out, final_w, final)
    return xn, hf


def kernel(x_prompt, x_sample, state_s5_re, state_s5_im, state_ssd, c, c_ctx, mod_w, mod_b, norm_w, final_norm_w,
           ev_w_in, ev_w_out, s5_a_re, s5_a_im, s5_log_dt, s5_b_re, s5_b_im, s5_c_re, s5_c_im, s5_d, s5_glu_w,
           s5_glu_b, hy_short_w, hy_short_b, hy_f_w1, hy_f_b1, hy_f_freq1, hy_f_w2, hy_f_b2, hy_f_freq2, hy_f_w3,
           hy_bias, ssd_w_in, ssd_conv_w, ssd_conv_b, ssd_a_log, ssd_dt_bias, ssd_d, ssd_norm_w, ssd_w_out):
    bp, lp, _ = x_prompt.shape
    bs, ls, _ = x_sample.shape
    rows = 2 * SUBLANES
    cvec = jnp.zeros((rows, D_MODEL), F32).at[0].set(c_ctx.astype(F32)).at[1:1 + bs].set(c.astype(F32))
    mod = _modulation(cvec, mod_w, mod_b)
    dft_p, dft_s = _dft_matrices(lp // HY_PHASES), _dft_matrices(ls // HY_PHASES)
    xp, xs = x_prompt, x_sample
    new_re, new_im, new_ssd = [], [], []
    for i in range(DEPTH):
        j = i // 2
        final = i == DEPTH - 1
        mod_p = mod[i, 0:1].reshape(1, 1, 3 * D_MODEL)
        mod_s = mod[i, 1:1 + bs].reshape(bs, 1, 3 * D_MODEL)
        if i % 2 == 0:
            s5 = _s5_prepare(s5_a_re[j], s5_a_im[j], s5_log_dt[j], s5_b_re[j], s5_b_im[j], s5_c_re[j], s5_c_im[j])
            w_in = ev_w_in[j].astype(BF16)
            w_out = ev_w_out[j].astype(BF16)
            glu_w = s5_glu_w[j].astype(BF16)
            fargs = (hy_f_w1[j], hy_f_b1[j], hy_f_freq1[j], hy_f_w2[j], hy_f_b2[j], hy_f_freq2[j], hy_f_w3[j])
            ks_p = _hy_spectra(_polyphase_filters(_hy_filters(lp, *fargs)), dft_p[0], hy_bias[j])
            ks_s = _hy_spectra(_polyphase_filters(_hy_filters(ls, *fargs)), dft_s[0], hy_bias[j])
            h0_p = jnp.zeros((2, bp, S5_STATE_LANES), F32)
            h0_s = _s5_state_to_lanes(state_s5_re[:, j].astype(F32), state_s5_im[:, j].astype(F32))
            common = (s5_d[j], glu_w, s5_glu_b[j], hy_short_w[j], hy_short_b[j])
            xp, hf = _even_layer(xp, mod_p, norm_w[i], w_in, w_out, s5, *common, dft_p, ks_p, h0_p,
                                 final_norm_w, final, s5_tq=32, conv_tc=512)
            xs, _ = _even_layer(xs, mod_s, norm_w[i], w_in, w_out, s5, *common, dft_s, ks_s, h0_s,
                                final_norm_w, final, s5_tq=64, conv_tc=256)
            re, im = _s5_state_from_lanes(hf)
            new_re.append(re)
            new_im.append(im)
        else:
            pad = LANES - 2 * SSD_HEADS
            w_in = jnp.pad(ssd_w_in[j], ((0, 0), (0, pad))).astype(BF16)
            w_out = ssd_w_out[j].astype(BF16)
            dt_bias = jnp.pad(ssd_dt_bias[j].reshape(1, -1).astype(F32), ((0, 0), (0, pad)))
            a_log = jnp.pad(ssd_a_log[j].reshape(1, -1).astype(F32), ((0, 0), (0, pad)))
            d_exp = jnp.repeat(ssd_d[j].astype(F32), SSD_HEADDIM).reshape(1, D_INNER)
            common = (ssd_conv_w[j], ssd_conv_b[j], a_log, dt_bias, d_exp, ssd_norm_w[j], w_out)
            h0_s = state_ssd[:, j].astype(F32)
            xp, hf = _odd_layer(xp, mod_p, norm_w[i], w_in, *common, None, final_norm_w, final)
            xs, _ = _odd_layer(xs, mod_s, norm_w[i], w_in, *common, h0_s, final_norm_w, final)
            new_ssd.append(hf)
    return (xp, xs, jnp.stack(new_re, axis=1), jnp.stack(new_im, axis=1), jnp.stack(new_ssd, axis=1))
```

```python
import functools
import math

import jax
import jax.numpy as jnp
from jax import lax
from jax.experimental import pallas as pl
from jax.experimental.pallas import tpu as pltpu

F32 = jnp.float32
BF16 = jnp.bfloat16
HIGHEST = lax.Precision.HIGHEST

D_MODEL = 1024
DEPTH = 4
D_INNER = 2048
EPS = 1e-6
S5_WIDTH = 1024
S5_GROUP = 16
S5_GROUPS = 64
S5_STATE = 64
HY_WIDTH = 1024
HY_BANDS = 16
HY_HID = 64
HY_DECAY_TARGET = 1e-2
HY_FAST = 0.3
HY_SLOW = 1.5
HY_PHASES = 4
SSD_HEADDIM = 64
SSD_HEADS = 32
SSD_STATE = 128
SSD_GROUPS = 4
SSD_CHUNK = 128
SSD_CONV_DIM = D_INNER + 2 * SSD_GROUPS * SSD_STATE

LANES = 128
SUBLANES = 8
VMEM_LIMIT = 56 * 1024 * 1024

S5_LANE_CHUNKS = S5_WIDTH // LANES
S5_CHUNK_STATE = (LANES // S5_GROUP) * S5_STATE
S5_STATE_LANES = S5_LANE_CHUNKS * 2 * S5_CHUNK_STATE
S5_PERM_ROWS = 256


def _params(*sem):
    return pltpu.CompilerParams(dimension_semantics=sem, vmem_limit_bytes=VMEM_LIMIT)


def _silu(x):
    return x * jax.nn.sigmoid(x)


def _dot(a, b):
    return jnp.dot(a, b, preferred_element_type=F32)


def _dot_hi(a, b):
    return jnp.dot(a, b, preferred_element_type=F32, precision=HIGHEST)


def _dot_split(a, b):
    a_hi, b_hi = a.astype(BF16), b.astype(BF16)
    a_lo = (a - a_hi.astype(F32)).astype(BF16)
    b_lo = (b - b_hi.astype(F32)).astype(BF16)
    return _dot(a_hi, b_hi) + (_dot(a_hi, b_lo) + _dot(a_lo, b_hi))


def _mod_kernel(c_ref, w_ref, b_ref, o_ref):
    s = _silu(c_ref[...])
    o_ref[0] = _dot_hi(s, w_ref[0]) + b_ref[0]


def _modulation(cvec, mod_w, mod_b):
    rows = cvec.shape[0]
    tn = D_MODEL
    return pl.pallas_call(
        _mod_kernel,
        grid=(DEPTH, 3 * D_MODEL // tn),
        in_specs=[
            pl.BlockSpec((rows, D_MODEL), lambda i, n: (0, 0)),
            pl.BlockSpec((1, D_MODEL, tn), lambda i, n: (i, 0, n)),
            pl.BlockSpec((1, 1, tn), lambda i, n: (i, 0, n)),
        ],
        out_specs=pl.BlockSpec((1, rows, tn), lambda i, n: (i, 0, n)),
        out_shape=jax.ShapeDtypeStruct((DEPTH, rows, 3 * D_MODEL), F32),
        compiler_params=_params("arbitrary", "arbitrary"),
        name="modulation",
    )(cvec, mod_w, mod_b.reshape(DEPTH, 1, 3 * D_MODEL))


def _norm_mod(x, mod, nw):
    shift = mod[:, :D_MODEL]
    scale = mod[:, D_MODEL:2 * D_MODEL]
    ms = jnp.mean(x * x, axis=-1, keepdims=True)
    return (x * lax.rsqrt(ms + EPS) * nw * (1.0 + scale) + shift).astype(BF16)


def _conv3(x, prev_row, next_row, w, b):
    rows = x.shape[0]
    ridx = lax.broadcasted_iota(jnp.int32, x.shape, 0)
    xp = jnp.where(ridx == 0, prev_row, pltpu.roll(x, 1, 0))
    xn = jnp.where(ridx == rows - 1, next_row, pltpu.roll(x, rows - 1, 0))
    return w[0:1] * xp + w[1:2] * x + w[2:3] * xn + b


def _conv3_phased(x, prev_row, next_row, w, b):
    ph = x.shape[0] // HY_PHASES
    blk = [x[r * ph:(r + 1) * ph] for r in range(HY_PHASES)]
    ridx = lax.broadcasted_iota(jnp.int32, blk[0].shape, 0)
    before = [jnp.where(ridx == 0, prev_row, pltpu.roll(blk[-1], 1, 0))] + blk[:-1]
    after = blk[1:] + [jnp.where(ridx == ph - 1, next_row, pltpu.roll(blk[0], ph - 1, 0))]
    return jnp.concatenate([w[0:1] * before[r] + w[1:2] * blk[r] + w[2:3] * after[r] + b
                            for r in range(HY_PHASES)], axis=0)


CONV_COLS = 256
TOKEN_BLOCK = 512
TAIL_BLOCK = 512


def _proj_conv3(conv, hb, pv_ref, nx_ref, mod, nw, w_ref, lo, width, cw_ref, cb_ref):
    m = pl.program_id(1)
    nm = pl.num_programs(1)
    h_prev = _norm_mod(pv_ref[0], mod, nw)
    h_next = _norm_mod(nx_ref[0], mod, nw)
    has_prev, has_next = (m > 0).astype(F32), (m < nm - 1).astype(F32)
    for off in range(0, width, CONV_COLS):
        w = w_ref[:, lo + off:lo + off + CONV_COLS]
        prev = _dot(h_prev, w)[SUBLANES - 1:SUBLANES] * has_prev
        nxt = _dot(h_next, w)[0:1] * has_next
        yield off, conv(_dot(hb, w), prev, nxt, cw_ref[:, off:off + CONV_COLS], cb_ref[:, off:off + CONV_COLS])


def _inproj_even_kernel(x_ref, pv_ref, nx_ref, mod_ref, nw_ref, w_ref, perm_ref, cw_ref, cb_ref,
                        u_ref, ga_ref, v_ref, x1_ref, x2_ref):
    mod, nw = mod_ref[0], nw_ref[...]
    hb = _norm_mod(x_ref[0], mod, nw)
    lo_vx, hi_vx = 2 * S5_WIDTH, 2 * S5_WIDTH + 3 * HY_WIDTH
    u_ref[0] = _dot(hb, w_ref[:, :S5_WIDTH])
    ga_ref[0] = _dot(hb, w_ref[:, S5_WIDTH:lo_vx]).astype(ga_ref.dtype)
    hbp = _dot(perm_ref[...], hb).astype(BF16)
    ph = hb.shape[0] // HY_PHASES
    outs = (v_ref, x1_ref, x2_ref)
    for off, y in _proj_conv3(_conv3_phased, hbp, pv_ref, nx_ref, mod, nw, w_ref, lo_vx, 3 * HY_WIDTH,
                              cw_ref, cb_ref):
        which, col = off // HY_WIDTH, off % HY_WIDTH
        if which == 2:
            y = y * _silu(_dot(hbp, w_ref[:, hi_vx + col:hi_vx + col + CONV_COLS]))
        o_ref = outs[which]
        o_ref[0, :, :, col:col + CONV_COLS] = y.reshape(HY_PHASES, ph, CONV_COLS).astype(o_ref.dtype)


def _softplus(x):
    return jnp.maximum(x, 0.0) + jnp.log(1.0 + jnp.exp(-jnp.abs(x)))


def _inproj_odd_kernel(x_ref, pv_ref, nx_ref, mod_ref, nw_ref, w_ref, cw_ref, cb_ref, bias_ref, alog_ref,
                       z_ref, xbc_ref, dt_ref, cum_ref, *, tm):
    mod, nw = mod_ref[0], nw_ref[...]
    hb = _norm_mod(x_ref[0], mod, nw)
    lo_dt = D_INNER + SSD_CONV_DIM
    z_ref[0] = _dot(hb, w_ref[:, :D_INNER]).astype(z_ref.dtype)
    for off, y in _proj_conv3(_conv3, hb, pv_ref, nx_ref, mod, nw, w_ref, D_INNER, SSD_CONV_DIM, cw_ref, cb_ref):
        xbc_ref[0, :, off:off + CONV_COLS] = _silu(y).astype(xbc_ref.dtype)
    dt = _softplus(_dot(hb, w_ref[:, lo_dt:]) + bias_ref[...])
    dt_ref[0] = dt
    da = dt * (-jnp.exp(alog_ref[...]))
    q = SSD_CHUNK
    ii = lax.broadcasted_iota(jnp.int32, (q, q), 0)
    jj = lax.broadcasted_iota(jnp.int32, (q, q), 1)
    low, up = (jj <= ii).astype(F32), (jj >= ii).astype(F32)
    fwd_lane = lax.broadcasted_iota(jnp.int32, (q, LANES), 1) < SSD_HEADS
    for c in range(tm // q):
        dac = da[c * q:(c + 1) * q]
        cum_ref[0, c * q:(c + 1) * q, :] = jnp.where(fwd_lane, _dot_hi(low, dac), _dot_hi(up, dac))


def _mod_spec(mod):
    per_batch = mod.shape[0] > 1
    return pl.BlockSpec((1, 1, mod.shape[-1]), (lambda b, m: (b, 0, 0)) if per_batch else (lambda b, m: (0, 0, 0)))


def _halo_specs(width, tl, seq):
    r = tl // SUBLANES
    last = seq // SUBLANES - 1
    return (pl.BlockSpec((1, SUBLANES, width), lambda b, m: (b, jnp.maximum(m * r - 1, 0), 0)),
            pl.BlockSpec((1, SUBLANES, width), lambda b, m: (b, jnp.minimum((m + 1) * r, last), 0)))


def _phase_perm(tm):
    ph = tm // HY_PHASES
    i = jnp.arange(tm, dtype=jnp.int32)
    src = HY_PHASES * (i % ph) + i // ph
    return (src[:, None] == i[None, :]).astype(BF16)


def _phased_spec(tm, width):
    return pl.BlockSpec((1, HY_PHASES, tm // HY_PHASES, width), lambda b, m: (b, 0, m, 0))


def _inproj(kernel_fn, x, mod, norm_w, w_bf16, consts, outs, tm=256):
    bsz, seq, _ = x.shape
    pv, nx = _halo_specs(D_MODEL, tm, seq)
    const = lambda a: pl.BlockSpec(a.shape, lambda b, m: (0, 0))
    out_specs, out_shape = [], []
    for width, dtype, phased in outs:
        if phased:
            out_specs.append(_phased_spec(tm, width))
            out_shape.append(jax.ShapeDtypeStruct((bsz, HY_PHASES, seq // HY_PHASES, width), dtype))
        else:
            out_specs.append(pl.BlockSpec((1, tm, width), lambda b, m: (b, m, 0)))
            out_shape.append(jax.ShapeDtypeStruct((bsz, seq, width), dtype))
    return pl.pallas_call(
        kernel_fn,
        grid=(bsz, seq // tm),
        in_specs=[pl.BlockSpec((1, tm, D_MODEL), lambda b, m: (b, m, 0)), pv, nx, _mod_spec(mod),
                  pl.BlockSpec((1, D_MODEL), lambda b, m: (0, 0)),
                  pl.BlockSpec(w_bf16.shape, lambda b, m: (0, 0), pipeline_mode=pl.Buffered(1))]
                 + [const(a) for a in consts],
        out_specs=out_specs,
        out_shape=out_shape,
        compiler_params=_params("arbitrary", "arbitrary"),
        name="inproj",
    )(x, x, x, mod, norm_w.reshape(1, D_MODEL), w_bf16, *consts)


def _s5_kernel(u_ref, pin_ref, pout_ref, bw_ref, cw_ref, lr_ref, li_ref, h0_ref, ys_ref, hf_ref, bu_ref, hst_ref,
               *, bsz, tq, jgroup):
    d = pl.program_id(0)
    blk = pl.program_id(1)
    nblk = pl.num_programs(1)
    cs = S5_CHUNK_STATE
    nsub = cs // LANES
    prow = pin_ref.shape[1]
    tsub = prow // bsz
    ns = tq // tsub

    @pl.when(blk == 0)
    def _():
        hst_ref[...] = h0_ref[0]

    def pick(blocks, s):
        return blocks[s] if ns == 1 else jnp.where(d == 0, blocks[s], blocks[ns - 1 - s])

    u_nat = [u_ref[:, s * tsub:(s + 1) * tsub, :].reshape(prow, S5_WIDTH) for s in range(ns)]
    u_tm = jnp.concatenate([_dot(pin_ref[0], pick(u_nat, s).astype(BF16)).astype(BF16) for s in range(ns)], axis=0)
    for j in range(S5_LANE_CHUNKS):
        bu_ref[:, 2 * cs * j:2 * cs * (j + 1)] = _dot(u_tm[:, LANES * j:LANES * (j + 1)], bw_ref[0, j])

    for j0 in range(0, S5_LANE_CHUNKS, jgroup):
        cols = [(2 * cs * j + LANES * c, 2 * cs * j + cs + LANES * c, j, c)
                for j in range(j0, j0 + jgroup) for c in range(nsub)]
        lam = [(jnp.broadcast_to(lr_ref[0, j, :, LANES * c:LANES * (c + 1)], (bsz, LANES)),
                jnp.broadcast_to(li_ref[0, j, :, LANES * c:LANES * (c + 1)], (bsz, LANES))) for _, _, j, c in cols]
        state = [(hst_ref[:, re_lo:re_lo + LANES], hst_ref[:, im_lo:im_lo + LANES]) for re_lo, im_lo, _, _ in cols]
        for step in range(tq):
            r0 = step * bsz
            for i, (re_lo, im_lo, _, _) in enumerate(cols):
                hr, hi = state[i]
                lam_re, lam_im = lam[i]
                nr = lam_re * hr - lam_im * hi + bu_ref[r0:r0 + bsz, re_lo:re_lo + LANES]
                ni = lam_re * hi + lam_im * hr + bu_ref[r0:r0 + bsz, im_lo:im_lo + LANES]
                bu_ref[r0:r0 + bsz, re_lo:re_lo + LANES] = nr
                bu_ref[r0:r0 + bsz, im_lo:im_lo + LANES] = ni
                state[i] = (nr, ni)
        for i, (re_lo, im_lo, _, _) in enumerate(cols):
            hst_ref[:, re_lo:re_lo + LANES] = state[i][0]
            hst_ref[:, im_lo:im_lo + LANES] = state[i][1]

    y_tm = jnp.concatenate([_dot(bu_ref[:, 2 * cs * j:2 * cs * (j + 1)].astype(BF16), cw_ref[0, j])
                            for j in range(S5_LANE_CHUNKS)], axis=-1)
    y_grp = [y_tm[s * prow:(s + 1) * prow] for s in range(ns)]
    for s in range(ns):
        y = pick(y_grp, s)
        y_hi = y.astype(BF16)
        y_lo = (y - y_hi.astype(F32)).astype(BF16)
        y_bm = _dot(pout_ref[0], y_hi) + _dot(pout_ref[0], y_lo)
        ys_ref[0, :, s * tsub:(s + 1) * tsub, :] = y_bm.reshape(bsz, tsub, S5_WIDTH).astype(ys_ref.dtype)

    @pl.when(blk == nblk - 1)
    def _():
        hf_ref[0] = hst_ref[...]


def _s5_scan(u, bw, cw, lr, li, h0, tq):
    bsz, seq, _ = u.shape
    nblk = seq // tq
    rows = bsz * tq
    prow = min(rows, S5_PERM_ROWS)
    tsub = prow // bsz
    r = jnp.arange(prow, dtype=jnp.int32)
    step, b = r // bsz, r % bsz
    src = jnp.stack([b * tsub + step, b * tsub + (tsub - 1 - step)])
    p_in = (src[:, :, None] == r[None, None, :]).astype(BF16)
    p_out = jnp.transpose(p_in, (0, 2, 1))

    def tblk(d, i):
        return i + d * (nblk - 1 - 2 * i)

    nj, cs = S5_LANE_CHUNKS, S5_CHUNK_STATE
    jgroup = 2 if bsz <= SUBLANES else 1
    return pl.pallas_call(
        functools.partial(_s5_kernel, bsz=bsz, tq=tq, jgroup=jgroup),
        grid=(2, nblk),
        in_specs=[
            pl.BlockSpec((bsz, tq, S5_WIDTH), lambda d, i: (0, tblk(d, i), 0)),
            pl.BlockSpec((1, prow, prow), lambda d, i: (d, 0, 0)),
            pl.BlockSpec((1, prow, prow), lambda d, i: (d, 0, 0)),
            pl.BlockSpec((1, nj, LANES, 2 * cs), lambda d, i: (d, 0, 0, 0)),
            pl.BlockSpec((1, nj, 2 * cs, LANES), lambda d, i: (d, 0, 0, 0)),
            pl.BlockSpec((1, nj, 1, cs), lambda d, i: (d, 0, 0, 0)),
            pl.BlockSpec((1, nj, 1, cs), lambda d, i: (d, 0, 0, 0)),
            pl.BlockSpec((1, bsz, S5_STATE_LANES), lambda d, i: (d, 0, 0)),
        ],
        out_specs=[
            pl.BlockSpec((1, bsz, tq, S5_WIDTH), lambda d, i: (d, 0, tblk(d, i), 0)),
            pl.BlockSpec((1, bsz, S5_STATE_LANES), lambda d, i: (d, 0, 0)),
        ],
        out_shape=[
            jax.ShapeDtypeStruct((2, bsz, seq, S5_WIDTH), BF16),
            jax.ShapeDtypeStruct((2, bsz, S5_STATE_LANES), F32),
        ],
        scratch_shapes=[
            pltpu.VMEM((rows, S5_STATE_LANES), F32),
            pltpu.VMEM((bsz, S5_STATE_LANES), F32),
        ],
        compiler_params=_params("arbitrary", "arbitrary"),
        name="s5_scan",
    )(u, p_in, p_out, bw, cw, lr, li, h0)


def _s5_prepare(a_re, a_im, log_dt, b_re, b_im, c_re, c_im):
    dt = jnp.exp(log_dt.astype(F32))[..., None]
    a_re, a_im = a_re.astype(F32), a_im.astype(F32)
    mag = jnp.exp(a_re * dt)
    lbr, lbi = mag * jnp.cos(a_im * dt), mag * jnp.sin(a_im * dt)
    den = a_re * a_re + a_im * a_im
    cr = ((lbr - 1.0) * a_re + lbi * a_im) / den
    ci = (lbi * a_re - (lbr - 1.0) * a_im) / den
    bbr = cr[..., None] * b_re - ci[..., None] * b_im
    bbi = cr[..., None] * b_im + ci[..., None] * b_re
    nj, gpc = S5_LANE_CHUNKS, LANES // S5_GROUP
    eye = jnp.eye(gpc, dtype=F32)

    def pack_b(m):
        m = m.reshape(2, nj, gpc, S5_STATE, S5_GROUP)
        return jnp.einsum('djgph,gk->djghkp', m, eye).reshape(2, nj, LANES, S5_CHUNK_STATE)

    def pack_c(m):
        m = m.reshape(2, nj, gpc, S5_GROUP, S5_STATE)
        return jnp.einsum('djghp,gk->djgpkh', m, eye).reshape(2, nj, S5_CHUNK_STATE, LANES)

    bw = jnp.concatenate([pack_b(bbr), pack_b(bbi)], axis=-1).astype(BF16)
    cw = jnp.concatenate([pack_c(c_re.astype(F32)), pack_c(-c_im.astype(F32))], axis=-2).astype(BF16)
    lr = lbr.reshape(2, nj, 1, S5_CHUNK_STATE)
    li = lbi.reshape(2, nj, 1, S5_CHUNK_STATE)
    return bw, cw, lr, li


def _s5_state_to_lanes(re, im):
    bsz = re.shape[0]
    st = jnp.stack([re.reshape(bsz, 2, S5_LANE_CHUNKS, S5_CHUNK_STATE),
                    im.reshape(bsz, 2, S5_LANE_CHUNKS, S5_CHUNK_STATE)], axis=3)
    return jnp.transpose(st.reshape(bsz, 2, S5_STATE_LANES), (1, 0, 2))


def _s5_state_from_lanes(hf):
    bsz = hf.shape[1]
    st = jnp.transpose(hf, (1, 0, 2)).reshape(bsz, 2, S5_LANE_CHUNKS, 2, S5_CHUNK_STATE)
    re = st[:, :, :, 0].reshape(bsz, 2, S5_GROUPS, S5_STATE)
    im = st[:, :, :, 1].reshape(bsz, 2, S5_GROUPS, S5_STATE)
    return re, im


def _hy_filter_kernel(bands_ref, deltas_ref, w1t_ref, w1c_ref, w1s_ref, b1_ref, f1_ref, w2_ref, b2_ref, f2_ref,
                      w3_ref, o_ref, *, seq, tl):
    r = pl.program_id(0)
    m = pl.program_id(1)
    t = (HY_PHASES * (lax.broadcasted_iota(jnp.int32, (tl, 1), 0) + m * tl) + r).astype(F32)
    t_norm = t / (seq - 1)
    ang = (2.0 * math.pi / seq) * t * bands_ref[...]
    pre = t_norm * w1t_ref[...] + _dot_hi(jnp.cos(ang), w1c_ref[...]) + _dot_hi(jnp.sin(ang), w1s_ref[...])
    f = jnp.sin(f1_ref[...] * (pre + b1_ref[...]))
    f = jnp.sin(f2_ref[...] * (_dot_hi(f, w2_ref[...]) + b2_ref[...]))
    f = _dot_split(f, w3_ref[...])
    window = jnp.exp(-t_norm * deltas_ref[...])
    for o in range(2):
        for direction in range(2):
            lo = (2 * o + direction) * HY_WIDTH
            o_ref[o, direction, 0] = f[:, lo:lo + HY_WIDTH] * window


def _hy_filters(seq, w1, b1, freq1, w2, b2, freq2, w3):
    n4 = seq // HY_PHASES
    tl = min(256, n4)
    bands = jnp.linspace(1e-4, HY_BANDS - 1, HY_BANDS, dtype=F32).reshape(1, HY_BANDS)
    deltas = jnp.abs(jnp.linspace(math.log(HY_DECAY_TARGET) / HY_SLOW, math.log(HY_DECAY_TARGET) / HY_FAST,
                                  HY_WIDTH, dtype=F32)).reshape(1, HY_WIDTH)
    small = [bands, deltas, w1[0:1], w1[1:1 + HY_BANDS], w1[1 + HY_BANDS:], b1.reshape(1, -1), freq1.reshape(1, -1),
             w2, b2.reshape(1, -1), freq2.reshape(1, -1), w3]
    return pl.pallas_call(
        functools.partial(_hy_filter_kernel, seq=seq, tl=tl),
        grid=(HY_PHASES, n4 // tl),
        in_specs=[pl.BlockSpec(a.shape, lambda r, m: (0, 0)) for a in small],
        out_specs=pl.BlockSpec((2, 2, 1, tl, HY_WIDTH), lambda r, m: (0, 0, r, m, 0)),
        out_shape=jax.ShapeDtypeStruct((2, 2, HY_PHASES, n4, HY_WIDTH), F32),
        compiler_params=_params("arbitrary", "arbitrary"),
        name="hy_filters",
    )(*small)


def _polyphase_filters(hf):
    fwd, bwd = hf[:, 0], hf[:, 1]
    zero = jnp.zeros_like(fwd[:, 0, :1])

    def lags(h, first, start):
        r, j0 = start % HY_PHASES, start // HY_PHASES
        tail = h[:, r, j0:] if j0 else h[:, r, :-1]
        return jnp.concatenate([first, tail], axis=1)

    out = []
    for q in range(1 - HY_PHASES, HY_PHASES):
        pos = fwd[:, q] if q >= 0 else lags(fwd, bwd[:, -q, :1], HY_PHASES + q)
        neg = lags(bwd, zero, HY_PHASES - q)
        out.append(jnp.stack([pos + neg, pos - neg], axis=1))
    return jnp.stack(out, axis=1).astype(BF16)


def _dft_matrices(seq):
    n = 2 * seq
    k = jnp.arange(seq, dtype=jnp.int32)
    ang = ((k[:, None] * k[None, :]) % n).astype(F32) * (2.0 * math.pi / n)
    cos, sin = jnp.cos(ang), jnp.sin(ang)
    alt = jnp.where(k % 2 == 0, 1.0, -1.0).astype(F32)
    fwd = jnp.concatenate([cos, jnp.where(k[:, None] == 0, alt[None, :], -sin)], axis=0)
    scale = jnp.where(k == 0, 1.0 / n, 2.0 / n).astype(F32)
    inv = jnp.concatenate([cos * scale[None, :], jnp.where(k[None, :] == 0, alt[:, None], -sin) * scale[None, :]],
                          axis=1)
    return fwd.astype(BF16), inv.astype(BF16)


def _hy_spec_kernel(fwd_ref, h_ref, bias_ref, o_ref, *, half):
    q = pl.program_id(1)
    f = fwd_ref[...]
    s_sum = _dot(f, h_ref[0, 0, 0])
    s_dif = _dot(f, h_ref[0, 0, 1])
    ridx = lax.broadcasted_iota(jnp.int32, s_sum.shape, 0)
    real_row = ridx <= half
    lag0 = jnp.where(real_row & (q == HY_PHASES - 1), bias_ref[0], 0.0)
    o_ref[0, 0] = jnp.where(real_row, s_sum, s_dif) + lag0


def _hy_spectra(hs, fwd, bias, tc=512):
    _, nq, _, n4, width = hs.shape
    return pl.pallas_call(
        functools.partial(_hy_spec_kernel, half=n4),
        grid=(2, nq, width // tc),
        in_specs=[
            pl.BlockSpec((2 * n4, n4), lambda o, q, c: (0, 0)),
            pl.BlockSpec((1, 1, 2, n4, tc), lambda o, q, c: (o, q, 0, 0, c)),
            pl.BlockSpec((1, 1, tc), lambda o, q, c: (o, 0, c)),
        ],
        out_specs=pl.BlockSpec((1, 1, 2 * n4, tc), lambda o, q, c: (o, q, 0, c)),
        out_shape=jax.ShapeDtypeStruct((2, nq, 2 * n4, width), F32),
        compiler_params=_params("arbitrary", "arbitrary", "arbitrary"),
        name="hy_spectra",
    )(fwd, hs, bias.reshape(2, 1, width))


CONV_LANES = 256


def _hy_conv_kernel(v_ref, g_ref, fwd_ref, inv_ref, ks_ref, o_ref, s_ref, p_ref, *, half):
    f = fwd_ref[...]
    for r in range(HY_PHASES):
        s_ref[r] = _dot(f, v_ref[0, r])
    nq = 2 * HY_PHASES - 1
    row0 = lax.broadcasted_iota(jnp.int32, (SUBLANES, CONV_LANES), 0) == 0
    for lo in range(0, half, SUBLANES):
        for c0 in range(0, v_ref.shape[-1], CONV_LANES):
            rows_re, rows_im, cols = slice(lo, lo + SUBLANES), slice(half + lo, half + lo + SUBLANES), \
                slice(c0, c0 + CONV_LANES)
            s_re = [s_ref[r, rows_re, cols] for r in range(HY_PHASES)]
            s_im = [s_ref[r, rows_im, cols] for r in range(HY_PHASES)]
            k_re = [ks_ref[0, q, rows_re, cols] for q in range(nq)]
            k_im = [ks_ref[0, q, rows_im, cols] for q in range(nq)]
            if lo == 0:
                s_ny, k_ny = [s[0:1] for s in s_im], [k[0:1] for k in k_im]
                s_im = [jnp.where(row0, 0.0, s) for s in s_im]
                k_im = [jnp.where(row0, 0.0, k) for k in k_im]
            for r in range(HY_PHASES):
                qs = [r - rp + HY_PHASES - 1 for rp in range(HY_PHASES)]
                p_re = functools.reduce(jnp.add, [s_re[rp] * k_re[q] - s_im[rp] * k_im[q] for rp, q in enumerate(qs)])
                p_im = functools.reduce(jnp.add, [s_re[rp] * k_im[q] + s_im[rp] * k_re[q] for rp, q in enumerate(qs)])
                if lo == 0:
                    p_ny = functools.reduce(jnp.add, [s_ny[rp] * k_ny[q] for rp, q in enumerate(qs)])
                    p_im = jnp.where(row0, p_ny, p_im)
                p_ref[r, rows_re, cols] = p_re
                p_ref[r, rows_im, cols] = p_im
    for r in range(HY_PHASES):
        o_ref[0, r] = (g_ref[0, r] * _dot(inv_ref[...], p_ref[r].astype(BF16))).astype(o_ref.dtype)


def _hy_conv(v, gate, fwd, inv, kspec, order, tc):
    bsz, _, n4, width = v.shape
    nq = 2 * HY_PHASES - 1
    blk = pl.BlockSpec((1, HY_PHASES, n4, tc), lambda c, b: (b, 0, 0, c))
    return pl.pallas_call(
        functools.partial(_hy_conv_kernel, half=n4),
        grid=(width // tc, bsz),
        in_specs=[
            blk, blk,
            pl.BlockSpec((2 * n4, n4), lambda c, b: (0, 0)),
            pl.BlockSpec((n4, 2 * n4), lambda c, b: (0, 0)),
            pl.BlockSpec((1, nq, 2 * n4, tc), lambda c, b: (order, 0, 0, c)),
        ],
        out_specs=blk,
        out_shape=jax.ShapeDtypeStruct(v.shape, BF16),
        scratch_shapes=[pltpu.VMEM((HY_PHASES, 2 * n4, tc), F32), pltpu.VMEM((HY_PHASES, 2 * n4, tc), F32)],
        compiler_params=_params("arbitrary", "arbitrary"),
        name="hy_conv",
    )(v, gate, fwd, inv, kspec)


def _gelu_tanh(x):
    return 0.5 * x * (1.0 + jnp.tanh(math.sqrt(2.0 / math.pi) * (x + 0.044715 * (x * x * x))))


def _rms(x, w):
    return x * lax.rsqrt(jnp.mean(x * x, axis=-1, keepdims=True) + EPS) * w


def _even_tail_kernel(x_ref, mod_ref, yf_ref, yb_ref, u_ref, ga_ref, hy_ref, unperm_ref, d_ref, gw_ref, gb_ref,
                      wo_ref, fw_ref, o_ref, *, final):
    y = yf_ref[0, 0].astype(F32) + yb_ref[0, 0].astype(F32) + u_ref[0] * d_ref[...]
    y = _gelu_tanh(y)
    y = y * jax.nn.sigmoid(_dot(y.astype(BF16), gw_ref[...]) + gb_ref[...])
    ya = (y * _silu(ga_ref[0].astype(F32))).astype(BF16)
    hy = _dot(unperm_ref[...], hy_ref[0].reshape(ya.shape[0], HY_WIDTH)).astype(BF16)
    out = _dot(ya, wo_ref[:S5_WIDTH, :]) + _dot(hy, wo_ref[S5_WIDTH:, :])
    xn = x_ref[0] + mod_ref[0][:, 2 * D_MODEL:] * out
    o_ref[0] = _rms(xn, fw_ref[...]) if final else xn


def _even_tail(x, mod, ys, u, g_a, hy, s5_d, glu_w, glu_b, w_out, final_w, final, tm=256):
    bsz, seq, _ = x.shape
    tok = lambda w: pl.BlockSpec((1, tm, w), lambda b, m: (b, m, 0))
    const = lambda r, c: pl.BlockSpec((r, c), lambda b, m: (0, 0))
    return pl.pallas_call(
        functools.partial(_even_tail_kernel, final=final),
        grid=(bsz, seq // tm),
        in_specs=[
            tok(D_MODEL), _mod_spec(mod),
            pl.BlockSpec((1, 1, tm, S5_WIDTH), lambda b, m: (0, b, m, 0)),
            pl.BlockSpec((1, 1, tm, S5_WIDTH), lambda b, m: (1, b, m, 0)),
            tok(S5_WIDTH), tok(S5_WIDTH), _phased_spec(tm, HY_WIDTH), const(tm, tm),
            const(1, S5_WIDTH), const(S5_WIDTH, S5_WIDTH), const(1, S5_WIDTH), const(D_INNER, D_MODEL),
            const(1, D_MODEL),
        ],
        out_specs=tok(D_MODEL),
        out_shape=jax.ShapeDtypeStruct((bsz, seq, D_MODEL), F32),
        compiler_params=_params("arbitrary", "arbitrary"),
        name="even_tail",
    )(x, mod, ys, ys, u, g_a, hy, _phase_perm(tm).T, s5_d.reshape(1, -1), glu_w, glu_b.reshape(1, -1), w_out,
      final_w.reshape(1, -1))


def _ssd_direction(xbc, dt, cum, st_ref, direction, d_row):
    q = SSD_CHUNK
    hp = LANES // SSD_HEADDIM
    ii = lax.broadcasted_iota(jnp.int32, (q, q), 0)
    jj = lax.broadcasted_iota(jnp.int32, (q, q), 1)
    if direction == 0:
        tri = (jj <= ii)
        last = q - 1
    else:
        tri = (jj >= ii)
        last = 0
    cum_t = cum.T
    src_t = cum_t - jnp.log(dt.T)
    w_t = jnp.exp(cum_t[:, last:last + 1] - src_t)
    last_row = cum[last:last + 1, :]
    first_half = lax.broadcasted_iota(jnp.int32, (q, LANES), 1) < SSD_HEADDIM
    first_half_row = lax.broadcasted_iota(jnp.int32, (1, LANES), 1) < SSD_HEADDIM
    x_all = xbc[:, :D_INNER]
    ys = []
    for g in range(SSD_GROUPS):
        b_g = xbc[:, D_INNER + g * SSD_STATE:D_INNER + (g + 1) * SSD_STATE]
        c_g = xbc[:, D_INNER + (SSD_GROUPS + g) * SSD_STATE:D_INNER + (SSD_GROUPS + g + 1) * SSD_STATE]
        b_bf, c_bf = b_g.astype(BF16), c_g.astype(BF16)
        cb = lax.dot_general(c_bf, b_bf, (((1,), (1,)), ((), ())), preferred_element_type=F32)
        b_t = b_g.T
        heads_per_group = SSD_HEADS // SSD_GROUPS
        for pair in range(heads_per_group // hp):
            h0 = g * heads_per_group + pair * hp
            lo = h0 * SSD_HEADDIM
            x_pair = x_all[:, lo:lo + LANES]
            x_blk = jnp.concatenate([jnp.where(first_half, x_pair, 0.0).astype(BF16),
                                     jnp.where(first_half, 0.0, x_pair).astype(BF16)], axis=0)
            scores, b_w, cols, tot = [], [], [], []
            for h in (h0, h0 + 1):
                hl = direction * SSD_HEADS + h
                col = jnp.broadcast_to(cum[:, hl:hl + 1], (q, q))
                arg = jnp.where(tri, col - src_t[hl:hl + 1, :], -jnp.inf)
                scores.append((cb * jnp.exp(arg)).astype(BF16))
                b_w.append((b_t * w_t[hl:hl + 1, :]).astype(BF16))
                cols.append(col)
                tot.append(jnp.broadcast_to(last_row[:, hl:hl + 1], (1, LANES)))
            s_pair = st_ref[:, lo:lo + LANES]
            y = _dot(jnp.concatenate(scores, axis=1), x_blk)
            y = y + _dot(c_bf, s_pair.astype(BF16)) * jnp.exp(jnp.where(first_half, cols[0], cols[1]))
            if d_row is not None:
                y = y + x_pair * d_row[:, lo:lo + LANES]
            ys.append(y)
            decay = jnp.exp(jnp.where(first_half_row, tot[0], tot[1]))
            st_ref[:, lo:lo + LANES] = s_pair * decay + _dot(jnp.concatenate(b_w, axis=1), x_blk)
    return jnp.concatenate(ys, axis=-1)


def _ssd_kernel(*refs, has_h0, want_final):
    xf_ref, xb_ref, dtf_ref, dtb_ref, cumf_ref, cumb_ref, d_ref = refs[:7]
    h0_ref = refs[7] if has_h0 else None
    yf_ref, yb_ref = refs[7 + has_h0:9 + has_h0]
    hf_ref = refs[9 + has_h0] if want_final else None
    st_ref = refs[-1]
    c = pl.program_id(1)
    nc = pl.num_programs(1)

    hp = LANES // SSD_HEADDIM
    pairs = [(dr, p) for dr in range(2) for p in range(SSD_HEADS // hp)]

    @pl.when(c == 0)
    def _():
        if not has_h0:
            st_ref[...] = jnp.zeros_like(st_ref)
        for dr, p in pairs if has_h0 else ():
            st_ref[dr, :, LANES * p:LANES * (p + 1)] = h0_ref[0, dr, hp * p:hp * (p + 1)].reshape(LANES, SSD_STATE).T

    yf = _ssd_direction(xf_ref[0].astype(F32), dtf_ref[0], cumf_ref[0], st_ref.at[0], 0, d_ref[...])
    yf_ref[0] = yf.astype(yf_ref.dtype)
    yb = _ssd_direction(xb_ref[0].astype(F32), dtb_ref[0], cumb_ref[0], st_ref.at[1], 1, None)
    yb_ref[0] = yb.astype(yb_ref.dtype)

    if want_final:
        @pl.when(c == nc - 1)
        def _():
            for dr, p in pairs:
                hf_ref[0, dr, hp * p:hp * (p + 1)] = (
                    st_ref[dr, :, LANES * p:LANES * (p + 1)].T.reshape(hp, SSD_HEADDIM, SSD_STATE))


def _ssd_scan(xbc, dt, cum, d_exp, h0, want_final):
    bsz, seq, width = xbc.shape
    q = SSD_CHUNK
    nc = seq // q
    fwd = lambda w: pl.BlockSpec((1, q, w), lambda b, c: (b, c, 0))
    bwd = lambda w: pl.BlockSpec((1, q, w), lambda b, c: (b, nc - 1 - c, 0))
    st_shape = (bsz, 2, SSD_HEADS, SSD_HEADDIM, SSD_STATE)
    st_spec = pl.BlockSpec((1,) + st_shape[1:], lambda b, c: (b, 0, 0, 0, 0))
    has_h0 = h0 is not None
    y_shape = jax.ShapeDtypeStruct((bsz, seq, D_INNER), BF16)
    outs = pl.pallas_call(
        functools.partial(_ssd_kernel, has_h0=has_h0, want_final=want_final),
        grid=(bsz, nc),
        in_specs=[fwd(width), bwd(width), fwd(LANES), bwd(LANES), fwd(LANES), bwd(LANES),
                  pl.BlockSpec((1, D_INNER), lambda b, c: (0, 0))] + [st_spec] * has_h0,
        out_specs=[fwd(D_INNER), bwd(D_INNER)] + [st_spec] * want_final,
        out_shape=[y_shape, y_shape] + [jax.ShapeDtypeStruct(st_shape, F32)] * want_final,
        scratch_shapes=[pltpu.VMEM((2, SSD_STATE, D_INNER), F32)],
        compiler_params=_params("arbitrary", "arbitrary"),
        name="ssd_scan",
    )(xbc, xbc, dt, dt, cum, cum, d_exp, *([h0] if has_h0 else []))
    return (outs[0], outs[1], outs[2] if want_final else None)


def _odd_tail_kernel(x_ref, mod_ref, yf_ref, yb_ref, z_ref, nw_ref, wo_ref, fw_ref, o_ref, *, final):
    y = (yf_ref[0].astype(F32) + yb_ref[0].astype(F32)) * _silu(z_ref[0].astype(F32))
    y = _rms(y, nw_ref[...]).astype(BF16)
    xn = x_ref[0] + mod_ref[0][:, 2 * D_MODEL:] * _dot(y, wo_ref[...])
    o_ref[0] = _rms(xn, fw_ref[...]) if final else xn


def _odd_tail(x, mod, yf, yb, z, norm_w, w_out, final_w, final, tm=256):
    bsz, seq, _ = x.shape
    tok = lambda w: pl.BlockSpec((1, tm, w), lambda b, m: (b, m, 0))
    const = lambda r, c: pl.BlockSpec((r, c), lambda b, m: (0, 0))
    return pl.pallas_call(
        functools.partial(_odd_tail_kernel, final=final),
        grid=(bsz, seq // tm),
        in_specs=[tok(D_MODEL), _mod_spec(mod), tok(D_INNER), tok(D_INNER), tok(D_INNER),
                  const(1, D_INNER), const(D_INNER, D_MODEL), const(1, D_MODEL)],
        out_specs=tok(D_MODEL),
        out_shape=jax.ShapeDtypeStruct((bsz, seq, D_MODEL), F32),
        compiler_params=_params("arbitrary", "arbitrary"),
        name="odd_tail",
    )(x, mod, yf, yb, z, norm_w.reshape(1, -1), w_out, final_w.reshape(1, -1))


def _even_layer(x, mod, norm_w, w_in, w_out, s5, s5_d, glu_w, glu_b, short_w, short_b, dft, kspec, h0,
                final_w, final, s5_tq, conv_tc):
    tm = min(TOKEN_BLOCK, x.shape[1])
    outs = ((S5_WIDTH, F32, False), (S5_WIDTH, BF16, False),
            (HY_WIDTH, BF16, True), (HY_WIDTH, BF16, True), (HY_WIDTH, BF16, True))
    u, g_a, v, x1, x2g = _inproj(_inproj_even_kernel, x, mod, norm_w, w_in,
                                 (_phase_perm(tm), short_w, short_b.reshape(1, -1)), outs, tm=tm)
    ys, hf = _s5_scan(u, *s5, h0, s5_tq)
    fwd, inv = dft
    z1 = _hy_conv(v, x1, fwd, inv, kspec, 0, conv_tc)
    hy = _hy_conv(z1, x2g, fwd, inv, kspec, 1, conv_tc)
    xn = _even_tail(x, mod, ys, u, g_a, hy, s5_d, glu_w, glu_b, w_out, final_w, final,
                    tm=min(TAIL_BLOCK, x.shape[1]))
    return xn, hf


def _odd_layer(x, mod, norm_w, w_in, conv_w, conv_b, a_log, dt_bias, d_exp, ssd_norm_w, w_out, h0, final_w, final):
    tm = min(TOKEN_BLOCK, x.shape[1])
    z, xbc, dt, cum = _inproj(functools.partial(_inproj_odd_kernel, tm=tm), x, mod, norm_w, w_in,
                              (conv_w, conv_b.reshape(1, -1), dt_bias, a_log),
                              ((D_INNER, BF16, False), (SSD_CONV_DIM, BF16, False), (LANES, F32, False),
                               (LANES, F32, False)), tm=tm)
    yf, yb, hf = _ssd_scan(xbc, dt, cum, d_exp, h0, want_final=h0 is None)
    xn = _odd_tail(x, mod, yf, yb, z, ssd_norm_w, w_out, final_w, final)
    return xn, hf


def kernel(x_prompt, x_sample, state_s5_re, state_s5_im, state_ssd, c, c_ctx, mod_w, mod_b, norm_w, final_norm_w,
           ev_w_in, ev_w_out, s5_a_re, s5_a_im, s5_log_dt, s5_b_re, s5_b_im, s5_c_re, s5_c_im, s5_d, s5_glu_w,
           s5_glu_b, hy_short_w, hy_short_b, hy_f_w1, hy_f_b1, hy_f_freq1, hy_f_w2, hy_f_b2, hy_f_freq2, hy_f_w3,
           hy_bias, ssd_w_in, ssd_conv_w, ssd_conv_b, ssd_a_log, ssd_dt_bias, ssd_d, ssd_norm_w, ssd_w_out):
    bp, lp, _ = x_prompt.shape
    bs, ls, _ = x_sample.shape
    rows = 2 * SUBLANES
    cvec = jnp.zeros((rows, D_MODEL), F32).at[0].set(c_ctx.astype(F32)).at[1:1 + bs].set(c.astype(F32))
    mod = _modulation(cvec, mod_w, mod_b)
    dft_p, dft_s = _dft_matrices(lp // HY_PHASES), _dft_matrices(ls // HY_PHASES)
    xp, xs = x_prompt, x_sample
    new_re, new_im, new_ssd = [], [], []
    for i in range(DEPTH):
        j = i // 2
        final = i == DEPTH - 1
        mod_p = mod[i, 0:1].reshape(1, 1, 3 * D_MODEL)
        mod_s = mod[i, 1:1 + bs].reshape(bs, 1, 3 * D_MODEL)
        if i % 2 == 0:
            s5 = _s5_prepare(s5_a_re[j], s5_a_im[j], s5_log_dt[j], s5_b_re[j], s5_b_im[j], s5_c_re[j], s5_c_im[j])
            w_in = ev_w_in[j].astype(BF16)
            w_out = ev_w_out[j].astype(BF16)
            glu_w = s5_glu_w[j].astype(BF16)
            fargs = (hy_f_w1[j], hy_f_b1[j], hy_f_freq1[j], hy_f_w2[j], hy_f_b2[j], hy_f_freq2[j], hy_f_w3[j])
            ks_p = _hy_spectra(_polyphase_filters(_hy_filters(lp, *fargs)), dft_p[0], hy_bias[j])
            ks_s = _hy_spectra(_polyphase_filters(_hy_filters(ls, *fargs)), dft_s[0], hy_bias[j])
            h0_p = jnp.zeros((2, bp, S5_STATE_LANES), F32)
            h0_s = _s5_state_to_lanes(state_s5_re[:, j].astype(F32), state_s5_im[:, j].astype(F32))
            common = (s5_d[j], glu_w, s5_glu_b[j], hy_short_w[j], hy_short_b[j])
            xp, hf = _even_layer(xp, mod_p, norm_w[i], w_in, w_out, s5, *common, dft_p, ks_p, h0_p,
                                 final_norm_w, final, s5_tq=32, conv_tc=512)
            xs, _ = _even_layer(xs, mod_s, norm_w[i], w_in, w_out, s5, *common, dft_s, ks_s, h0_s,
                                final_norm_w, final, s5_tq=64, conv_tc=256)
            re, im = _s5_state_from_lanes(hf)
            new_re.append(re)
            new_im.append(im)
        else:
            pad = LANES - 2 * SSD_HEADS
            w_in = jnp.pad(ssd_w_in[j], ((0, 0), (0, pad))).astype(BF16)
            w_out = ssd_w_out[j].astype(BF16)
            dt_bias = jnp.pad(ssd_dt_bias[j].reshape(1, -1).astype(F32), ((0, 0), (0, pad)))
            a_log = jnp.pad(ssd_a_log[j].reshape(1, -1).astype(F32), ((0, 0), (0, pad)))
            d_exp = jnp.repeat(ssd_d[j].astype(F32), SSD_HEADDIM).reshape(1, D_INNER)
            common = (ssd_conv_w[j], ssd_conv_b[j], a_log, dt_bias, d_exp, ssd_norm_w[j], w_out)
            h0_s = state_ssd[:, j].astype(F32)
            xp, hf = _odd_layer(xp, mod_p, norm_w[i], w_in, *common, None, final_norm_w, final)
            xs, _ = _odd_layer(xs, mod_s, norm_w[i], w_in, *common, h0_s, final_norm_w, final)
            new_ssd.append(hf)
    return (xp, xs, jnp.stack(new_re, axis=1), jnp.stack(new_im, axis=1), jnp.stack(new_ssd, axis=1))
```

```python
import functools
import math

import jax
import jax.numpy as jnp
from jax import lax
from jax.experimental import pallas as pl
from jax.experimental.pallas import tpu as pltpu

F32 = jnp.float32
BF16 = jnp.bfloat16
HIGHEST = lax.Precision.HIGHEST

D_MODEL = 1024
DEPTH = 4
D_INNER = 2048
EPS = 1e-6
S5_WIDTH = 1024
S5_GROUP = 16
S5_GROUPS = 64
S5_STATE = 64
HY_WIDTH = 1024
HY_BANDS = 16
HY_HID = 64
HY_DECAY_TARGET = 1e-2
HY_FAST = 0.3
HY_SLOW = 1.5
HY_PHASES = 4
SSD_HEADDIM = 64
SSD_HEADS = 32
SSD_STATE = 128
SSD_GROUPS = 4
SSD_CHUNK = 128
SSD_CONV_DIM = D_INNER + 2 * SSD_GROUPS * SSD_STATE

LANES = 128
SUBLANES = 8
VMEM_LIMIT = 56 * 1024 * 1024

S5_LANE_CHUNKS = S5_WIDTH // LANES
S5_CHUNK_STATE = (LANES // S5_GROUP) * S5_STATE
S5_STATE_LANES = S5_LANE_CHUNKS * 2 * S5_CHUNK_STATE
S5_PERM_ROWS = 256


def _params(*sem):
    return pltpu.CompilerParams(dimension_semantics=sem, vmem_limit_bytes=VMEM_LIMIT)


def _silu(x):
    return x * jax.nn.sigmoid(x)


def _dot(a, b):
    return jnp.dot(a, b, preferred_element_type=F32)


def _dot_hi(a, b):
    return jnp.dot(a, b, preferred_element_type=F32, precision=HIGHEST)


def _dot_split(a, b):
    a_hi, b_hi = a.astype(BF16), b.astype(BF16)
    a_lo = (a - a_hi.astype(F32)).astype(BF16)
    b_lo = (b - b_hi.astype(F32)).astype(BF16)
    return _dot(a_hi, b_hi) + (_dot(a_hi, b_lo) + _dot(a_lo, b_hi))


def _mod_kernel(c_ref, w_ref, b_ref, o_ref):
    s = _silu(c_ref[...])
    o_ref[0] = _dot_hi(s, w_ref[0]) + b_ref[0]


def _modulation(cvec, mod_w, mod_b):
    rows = cvec.shape[0]
    tn = D_MODEL
    return pl.pallas_call(
        _mod_kernel,
        grid=(DEPTH, 3 * D_MODEL // tn),
        in_specs=[
            pl.BlockSpec((rows, D_MODEL), lambda i, n: (0, 0)),
            pl.BlockSpec((1, D_MODEL, tn), lambda i, n: (i, 0, n)),
            pl.BlockSpec((1, 1, tn), lambda i, n: (i, 0, n)),
        ],
        out_specs=pl.BlockSpec((1, rows, tn), lambda i, n: (i, 0, n)),
        out_shape=jax.ShapeDtypeStruct((DEPTH, rows, 3 * D_MODEL), F32),
        compiler_params=_params("arbitrary", "arbitrary"),
        name="modulation",
    )(cvec, mod_w, mod_b.reshape(DEPTH, 1, 3 * D_MODEL))


def _norm_mod(x, mod, nw):
    shift = mod[:, :D_MODEL]
    scale = mod[:, D_MODEL:2 * D_MODEL]
    ms = jnp.mean(x * x, axis=-1, keepdims=True)
    return (x * lax.rsqrt(ms + EPS) * nw * (1.0 + scale) + shift).astype(BF16)


def _conv3(x, prev_row, next_row, w, b):
    rows = x.shape[0]
    ridx = lax.broadcasted_iota(jnp.int32, x.shape, 0)
    xp = jnp.where(ridx == 0, prev_row, pltpu.roll(x, 1, 0))
    xn = jnp.where(ridx == rows - 1, next_row, pltpu.roll(x, rows - 1, 0))
    return w[0:1] * xp + w[1:2] * x + w[2:3] * xn + b


def _conv3_phased(x, prev_row, next_row, w, b):
    ph = x.shape[0] // HY_PHASES
    blk = [x[r * ph:(r + 1) * ph] for r in range(HY_PHASES)]
    ridx = lax.broadcasted_iota(jnp.int32, blk[0].shape, 0)
    before = [jnp.where(ridx == 0, prev_row, pltpu.roll(blk[-1], 1, 0))] + blk[:-1]
    after = blk[1:] + [jnp.where(ridx == ph - 1, next_row, pltpu.roll(blk[0], ph - 1, 0))]
    return jnp.concatenate([w[0:1] * before[r] + w[1:2] * blk[r] + w[2:3] * after[r] + b
                            for r in range(HY_PHASES)], axis=0)


CONV_COLS = 256
TOKEN_BLOCK = 1024
TAIL_BLOCK = 512


def _proj_conv3(conv, hb, pv_ref, nx_ref, mod, nw, w_ref, lo, width, cw_ref, cb_ref):
    m = pl.program_id(1)
    nm = pl.num_programs(1)
    h_prev = _norm_mod(pv_ref[0], mod, nw)
    h_next = _norm_mod(nx_ref[0], mod, nw)
    has_prev, has_next = (m > 0).astype(F32), (m < nm - 1).astype(F32)
    for off in range(0, width, CONV_COLS):
        w = w_ref[:, lo + off:lo + off + CONV_COLS]
        prev = _dot(h_prev, w)[SUBLANES - 1:SUBLANES] * has_prev
        nxt = _dot(h_next, w)[0:1] * has_next
        yield off, conv(_dot(hb, w), prev, nxt, cw_ref[:, off:off + CONV_COLS], cb_ref[:, off:off + CONV_COLS])


def _inproj_even_kernel(x_ref, pv_ref, nx_ref, mod_ref, nw_ref, w_ref, perm_ref, cw_ref, cb_ref,
                        u_ref, ga_ref, v_ref, x1_ref, x2_ref):
    mod, nw = mod_ref[0], nw_ref[...]
    hb = _norm_mod(x_ref[0], mod, nw)
    lo_vx, hi_vx = 2 * S5_WIDTH, 2 * S5_WIDTH + 3 * HY_WIDTH
    u_ref[0] = _dot(hb, w_ref[:, :S5_WIDTH])
    ga_ref[0] = _dot(hb, w_ref[:, S5_WIDTH:lo_vx]).astype(ga_ref.dtype)
    hbp = _dot(perm_ref[...], hb).astype(BF16)
    ph = hb.shape[0] // HY_PHASES
    outs = (v_ref, x1_ref, x2_ref)
    for off, y in _proj_conv3(_conv3_phased, hbp, pv_ref, nx_ref, mod, nw, w_ref, lo_vx, 3 * HY_WIDTH,
                              cw_ref, cb_ref):
        which, col = off // HY_WIDTH, off % HY_WIDTH
        if which == 2:
            y = y * _silu(_dot(hbp, w_ref[:, hi_vx + col:hi_vx + col + CONV_COLS]))
        o_ref = outs[which]
        o_ref[0, :, :, col:col + CONV_COLS] = y.reshape(HY_PHASES, ph, CONV_COLS).astype(o_ref.dtype)


def _softplus(x):
    return jnp.maximum(x, 0.0) + jnp.log(1.0 + jnp.exp(-jnp.abs(x)))


def _inproj_odd_kernel(x_ref, pv_ref, nx_ref, mod_ref, nw_ref, w_ref, cw_ref, cb_ref, bias_ref, alog_ref,
                       z_ref, xbc_ref, dt_ref, cum_ref, *, tm):
    mod, nw = mod_ref[0], nw_ref[...]
    hb = _norm_mod(x_ref[0], mod, nw)
    lo_dt = D_INNER + SSD_CONV_DIM
    z_ref[0] = _dot(hb, w_ref[:, :D_INNER]).astype(z_ref.dtype)
    for off, y in _proj_conv3(_conv3, hb, pv_ref, nx_ref, mod, nw, w_ref, D_INNER, SSD_CONV_DIM, cw_ref, cb_ref):
        xbc_ref[0, :, off:off + CONV_COLS] = _silu(y).astype(xbc_ref.dtype)
    dt = _softplus(_dot(hb, w_ref[:, lo_dt:]) + bias_ref[...])
    dt_ref[0] = dt
    da = dt * (-jnp.exp(alog_ref[...]))
    q = SSD_CHUNK
    ii = lax.broadcasted_iota(jnp.int32, (q, q), 0)
    jj = lax.broadcasted_iota(jnp.int32, (q, q), 1)
    low, up = (jj <= ii).astype(F32), (jj >= ii).astype(F32)
    fwd_lane = lax.broadcasted_iota(jnp.int32, (q, LANES), 1) < SSD_HEADS
    for c in range(tm // q):
        dac = da[c * q:(c + 1) * q]
        cum_ref[0, c * q:(c + 1) * q, :] = jnp.where(fwd_lane, _dot_hi(low, dac), _dot_hi(up, dac))


def _mod_spec(mod):
    per_batch = mod.shape[0] > 1
    return pl.BlockSpec((1, 1, mod.shape[-1]), (lambda b, m: (b, 0, 0)) if per_batch else (lambda b, m: (0, 0, 0)))


def _halo_specs(width, tl, seq):
    r = tl // SUBLANES
    last = seq // SUBLANES - 1
    return (pl.BlockSpec((1, SUBLANES, width), lambda b, m: (b, jnp.maximum(m * r - 1, 0), 0)),
            pl.BlockSpec((1, SUBLANES, width), lambda b, m: (b, jnp.minimum((m + 1) * r, last), 0)))


def _phase_perm(tm):
    ph = tm // HY_PHASES
    i = jnp.arange(tm, dtype=jnp.int32)
    src = HY_PHASES * (i % ph) + i // ph
    return (src[:, None] == i[None, :]).astype(BF16)


def _phased_spec(tm, width):
    return pl.BlockSpec((1, HY_PHASES, tm // HY_PHASES, width), lambda b, m: (b, 0, m, 0))


def _inproj(kernel_fn, x, mod, norm_w, w_bf16, consts, outs, tm=256):
    bsz, seq, _ = x.shape
    pv, nx = _halo_specs(D_MODEL, tm, seq)
    const = lambda a: pl.BlockSpec(a.shape, lambda b, m: (0, 0))
    out_specs, out_shape = [], []
    for width, dtype, phased in outs:
        if phased:
            out_specs.append(_phased_spec(tm, width))
            out_shape.append(jax.ShapeDtypeStruct((bsz, HY_PHASES, seq // HY_PHASES, width), dtype))
        else:
            out_specs.append(pl.BlockSpec((1, tm, width), lambda b, m: (b, m, 0)))
            out_shape.append(jax.ShapeDtypeStruct((bsz, seq, width), dtype))
    return pl.pallas_call(
        kernel_fn,
        grid=(bsz, seq // tm),
        in_specs=[pl.BlockSpec((1, tm, D_MODEL), lambda b, m: (b, m, 0)), pv, nx, _mod_spec(mod),
                  pl.BlockSpec((1, D_MODEL), lambda b, m: (0, 0)),
                  pl.BlockSpec(w_bf16.shape, lambda b, m: (0, 0), pipeline_mode=pl.Buffered(1))]
                 + [const(a) for a in consts],
        out_specs=out_specs,
        out_shape=out_shape,
        compiler_params=_params("arbitrary", "arbitrary"),
        name="inproj",
    )(x, x, x, mod, norm_w.reshape(1, D_MODEL), w_bf16, *consts)


def _s5_kernel(u_ref, pin_ref, pout_ref, bw_ref, cw_ref, lr_ref, li_ref, h0_ref, ys_ref, hf_ref, bu_ref, hst_ref,
               *, bsz, tq, jgroup):
    d = pl.program_id(0)
    blk = pl.program_id(1)
    nblk = pl.num_programs(1)
    cs = S5_CHUNK_STATE
    nsub = cs // LANES
    prow = pin_ref.shape[1]
    tsub = prow // bsz
    ns = tq // tsub

    @pl.when(blk == 0)
    def _():
        hst_ref[...] = h0_ref[0]

    def pick(blocks, s):
        return blocks[s] if ns == 1 else jnp.where(d == 0, blocks[s], blocks[ns - 1 - s])

    u_nat = [u_ref[:, s * tsub:(s + 1) * tsub, :].reshape(prow, S5_WIDTH) for s in range(ns)]
    u_tm = jnp.concatenate([_dot(pin_ref[0], pick(u_nat, s).astype(BF16)).astype(BF16) for s in range(ns)], axis=0)
    for j in range(S5_LANE_CHUNKS):
        bu_ref[:, 2 * cs * j:2 * cs * (j + 1)] = _dot(u_tm[:, LANES * j:LANES * (j + 1)], bw_ref[0, j])

    for j0 in range(0, S5_LANE_CHUNKS, jgroup):
        cols = [(2 * cs * j + LANES * c, 2 * cs * j + cs + LANES * c, j, c)
                for j in range(j0, j0 + jgroup) for c in range(nsub)]
        lam = [(jnp.broadcast_to(lr_ref[0, j, :, LANES * c:LANES * (c + 1)], (bsz, LANES)),
                jnp.broadcast_to(li_ref[0, j, :, LANES * c:LANES * (c + 1)], (bsz, LANES))) for _, _, j, c in cols]
        state = [(hst_ref[:, re_lo:re_lo + LANES], hst_ref[:, im_lo:im_lo + LANES]) for re_lo, im_lo, _, _ in cols]
        for step in range(tq):
            r0 = step * bsz
            for i, (re_lo, im_lo, _, _) in enumerate(cols):
                hr, hi = state[i]
                lam_re, lam_im = lam[i]
                nr = lam_re * hr - lam_im * hi + bu_ref[r0:r0 + bsz, re_lo:re_lo + LANES]
                ni = lam_re * hi + lam_im * hr + bu_ref[r0:r0 + bsz, im_lo:im_lo + LANES]
                bu_ref[r0:r0 + bsz, re_lo:re_lo + LANES] = nr
                bu_ref[r0:r0 + bsz, im_lo:im_lo + LANES] = ni
                state[i] = (nr, ni)
        for i, (re_lo, im_lo, _, _) in enumerate(cols):
            hst_ref[:, re_lo:re_lo + LANES] = state[i][0]
            hst_ref[:, im_lo:im_lo + LANES] = state[i][1]

    y_tm = jnp.concatenate([_dot(bu_ref[:, 2 * cs * j:2 * cs * (j + 1)].astype(BF16), cw_ref[0, j])
                            for j in range(S5_LANE_CHUNKS)], axis=-1)
    y_grp = [y_tm[s * prow:(s + 1) * prow] for s in range(ns)]
    for s in range(ns):
        y = pick(y_grp, s)
        y_hi = y.astype(BF16)
        y_lo = (y - y_hi.astype(F32)).astype(BF16)
        y_bm = _dot(pout_ref[0], y_hi) + _dot(pout_ref[0], y_lo)
        ys_ref[0, :, s * tsub:(s + 1) * tsub, :] = y_bm.reshape(bsz, tsub, S5_WIDTH).astype(ys_ref.dtype)

    @pl.when(blk == nblk - 1)
    def _():
        hf_ref[0] = hst_ref[...]


def _s5_scan(u, bw, cw, lr, li, h0, tq):
    bsz, seq, _ = u.shape
    nblk = seq // tq
    rows = bsz * tq
    prow = min(rows, S5_PERM_ROWS)
    tsub = prow // bsz
    r = jnp.arange(prow, dtype=jnp.int32)
    step, b = r // bsz, r % bsz
    src = jnp.stack([b * tsub + step, b * tsub + (tsub - 1 - step)])
    p_in = (src[:, :, None] == r[None, None, :]).astype(BF16)
    p_out = jnp.transpose(p_in, (0, 2, 1))

    def tblk(d, i):
        return i + d * (nblk - 1 - 2 * i)

    nj, cs = S5_LANE_CHUNKS, S5_CHUNK_STATE
    jgroup = 2 if bsz <= SUBLANES else 1
    return pl.pallas_call(
        functools.partial(_s5_kernel, bsz=bsz, tq=tq, jgroup=jgroup),
        grid=(2, nblk),
        in_specs=[
            pl.BlockSpec((bsz, tq, S5_WIDTH), lambda d, i: (0, tblk(d, i), 0)),
            pl.BlockSpec((1, prow, prow), lambda d, i: (d, 0, 0)),
            pl.BlockSpec((1, prow, prow), lambda d, i: (d, 0, 0)),
            pl.BlockSpec((1, nj, LANES, 2 * cs), lambda d, i: (d, 0, 0, 0)),
            pl.BlockSpec((1, nj, 2 * cs, LANES), lambda d, i: (d, 0, 0, 0)),
            pl.BlockSpec((1, nj, 1, cs), lambda d, i: (d, 0, 0, 0)),
            pl.BlockSpec((1, nj, 1, cs), lambda d, i: (d, 0, 0, 0)),
            pl.BlockSpec((1, bsz, S5_STATE_LANES), lambda d, i: (d, 0, 0)),
        ],
        out_specs=[
            pl.BlockSpec((1, bsz, tq, S5_WIDTH), lambda d, i: (d, 0, tblk(d, i), 0)),
            pl.BlockSpec((1, bsz, S5_STATE_LANES), lambda d, i: (d, 0, 0)),
        ],
        out_shape=[
            jax.ShapeDtypeStruct((2, bsz, seq, S5_WIDTH), BF16),
            jax.ShapeDtypeStruct((2, bsz, S5_STATE_LANES), F32),
        ],
        scratch_shapes=[
            pltpu.VMEM((rows, S5_STATE_LANES), F32),
            pltpu.VMEM((bsz, S5_STATE_LANES), F32),
        ],
        compiler_params=_params("arbitrary", "arbitrary"),
        name="s5_scan",
    )(u, p_in, p_out, bw, cw, lr, li, h0)


def _s5_prepare(a_re, a_im, log_dt, b_re, b_im, c_re, c_im):
    dt = jnp.exp(log_dt.astype(F32))[..., None]
    a_re, a_im = a_re.astype(F32), a_im.astype(F32)
    mag = jnp.exp(a_re * dt)
    lbr, lbi = mag * jnp.cos(a_im * dt), mag * jnp.sin(a_im * dt)
    den = a_re * a_re + a_im * a_im
    cr = ((lbr - 1.0) * a_re + lbi * a_im) / den
    ci = (lbi * a_re - (lbr - 1.0) * a_im) / den
    bbr = cr[..., None] * b_re - ci[..., None] * b_im
    bbi = cr[..., None] * b_im + ci[..., None] * b_re
    nj, gpc = S5_LANE_CHUNKS, LANES // S5_GROUP
    eye = jnp.eye(gpc, dtype=F32)

    def pack_b(m):
        m = m.reshape(2, nj, gpc, S5_STATE, S5_GROUP)
        return jnp.einsum('djgph,gk->djghkp', m, eye).reshape(2, nj, LANES, S5_CHUNK_STATE)

    def pack_c(m):
        m = m.reshape(2, nj, gpc, S5_GROUP, S5_STATE)
        return jnp.einsum('djghp,gk->djgpkh', m, eye).reshape(2, nj, S5_CHUNK_STATE, LANES)

    bw = jnp.concatenate([pack_b(bbr), pack_b(bbi)], axis=-1).astype(BF16)
    cw = jnp.concatenate([pack_c(c_re.astype(F32)), pack_c(-c_im.astype(F32))], axis=-2).astype(BF16)
    lr = lbr.reshape(2, nj, 1, S5_CHUNK_STATE)
    li = lbi.reshape(2, nj, 1, S5_CHUNK_STATE)
    return bw, cw, lr, li


def _s5_state_to_lanes(re, im):
    bsz = re.shape[0]
    st = jnp.stack([re.reshape(bsz, 2, S5_LANE_CHUNKS, S5_CHUNK_STATE),
                    im.reshape(bsz, 2, S5_LANE_CHUNKS, S5_CHUNK_STATE)], axis=3)
    return jnp.transpose(st.reshape(bsz, 2, S5_STATE_LANES), (1, 0, 2))


def _s5_state_from_lanes(hf):
    bsz = hf.shape[1]
    st = jnp.transpose(hf, (1, 0, 2)).reshape(bsz, 2, S5_LANE_CHUNKS, 2, S5_CHUNK_STATE)
    re = st[:, :, :, 0].reshape(bsz, 2, S5_GROUPS, S5_STATE)
    im = st[:, :, :, 1].reshape(bsz, 2, S5_GROUPS, S5_STATE)
    return re, im


def _hy_filter_kernel(bands_ref, deltas_ref, w1t_ref, w1c_ref, w1s_ref, b1_ref, f1_ref, w2_ref, b2_ref, f2_ref,
                      w3_ref, o_ref, *, seq, tl):
    r = pl.program_id(0)
    m = pl.program_id(1)
    t = (HY_PHASES * (lax.broadcasted_iota(jnp.int32, (tl, 1), 0) + m * tl) + r).astype(F32)
    t_norm = t / (seq - 1)
    ang = (2.0 * math.pi / seq) * t * bands_ref[...]
    pre = t_norm * w1t_ref[...] + _dot_hi(jnp.cos(ang), w1c_ref[...]) + _dot_hi(jnp.sin(ang), w1s_ref[...])
    f = jnp.sin(f1_ref[...] * (pre + b1_ref[...]))
    f = jnp.sin(f2_ref[...] * (_dot_hi(f, w2_ref[...]) + b2_ref[...]))
    f = _dot_split(f, w3_ref[...])
    window = jnp.exp(-t_norm * deltas_ref[...])
    for o in range(2):
        for direction in range(2):
            lo = (2 * o + direction) * HY_WIDTH
            o_ref[o, direction, 0] = f[:, lo:lo + HY_WIDTH] * window


def _hy_filters(seq, w1, b1, freq1, w2, b2, freq2, w3):
    n4 = seq // HY_PHASES
    tl = min(256, n4)
    bands = jnp.linspace(1e-4, HY_BANDS - 1, HY_BANDS, dtype=F32).reshape(1, HY_BANDS)
    deltas = jnp.abs(jnp.linspace(math.log(HY_DECAY_TARGET) / HY_SLOW, math.log(HY_DECAY_TARGET) / HY_FAST,
                                  HY_WIDTH, dtype=F32)).reshape(1, HY_WIDTH)
    small = [bands, deltas, w1[0:1], w1[1:1 + HY_BANDS], w1[1 + HY_BANDS:], b1.reshape(1, -1), freq1.reshape(1, -1),
             w2, b2.reshape(1, -1), freq2.reshape(1, -1), w3]
    return pl.pallas_call(
        functools.partial(_hy_filter_kernel, seq=seq, tl=tl),
        grid=(HY_PHASES, n4 // tl),
        in_specs=[pl.BlockSpec(a.shape, lambda r, m: (0, 0)) for a in small],
        out_specs=pl.BlockSpec((2, 2, 1, tl, HY_WIDTH), lambda r, m: (0, 0, r, m, 0)),
        out_shape=jax.ShapeDtypeStruct((2, 2, HY_PHASES, n4, HY_WIDTH), F32),
        compiler_params=_params("arbitrary", "arbitrary"),
        name="hy_filters",
    )(*small)


def _polyphase_filters(hf):
    fwd, bwd = hf[:, 0], hf[:, 1]
    zero = jnp.zeros_like(fwd[:, 0, :1])

    def lags(h, first, start):
        r, j0 = start % HY_PHASES, start // HY_PHASES
        tail = h[:, r, j0:] if j0 else h[:, r, :-1]
        return jnp.concatenate([first, tail], axis=1)

    out = []
    for q in range(1 - HY_PHASES, HY_PHASES):
        pos = fwd[:, q] if q >= 0 else lags(fwd, bwd[:, -q, :1], HY_PHASES + q)
        neg = lags(bwd, zero, HY_PHASES - q)
        out.append(jnp.stack([pos + neg, pos - neg], axis=1))
    return jnp.stack(out, axis=1).astype(BF16)


def _dft_matrices(seq):
    n = 2 * seq
    k = jnp.arange(seq, dtype=jnp.int32)
    ang = ((k[:, None] * k[None, :]) % n).astype(F32) * (2.0 * math.pi / n)
    cos, sin = jnp.cos(ang), jnp.sin(ang)
    alt = jnp.where(k % 2 == 0, 1.0, -1.0).astype(F32)
    fwd = jnp.concatenate([cos, jnp.where(k[:, None] == 0, alt[None, :], -sin)], axis=0)
    scale = jnp.where(k == 0, 1.0 / n, 2.0 / n).astype(F32)
    inv = jnp.concatenate([cos * scale[None, :], jnp.where(k[None, :] == 0, alt[:, None], -sin) * scale[None, :]],
                          axis=1)
    return fwd.astype(BF16), inv.astype(BF16)


def _hy_spec_kernel(fwd_ref, h_ref, bias_ref, o_ref, *, half):
    q = pl.program_id(1)
    f = fwd_ref[...]
    s_sum = _dot(f, h_ref[0, 0, 0])
    s_dif = _dot(f, h_ref[0, 0, 1])
    ridx = lax.broadcasted_iota(jnp.int32, s_sum.shape, 0)
    real_row = ridx <= half
    lag0 = jnp.where(real_row & (q == HY_PHASES - 1), bias_ref[0], 0.0)
    o_ref[0, 0] = jnp.where(real_row, s_sum, s_dif) + lag0


def _hy_spectra(hs, fwd, bias, tc=512):
    _, nq, _, n4, width = hs.shape
    return pl.pallas_call(
        functools.partial(_hy_spec_kernel, half=n4),
        grid=(2, nq, width // tc),
        in_specs=[
            pl.BlockSpec((2 * n4, n4), lambda o, q, c: (0, 0)),
            pl.BlockSpec((1, 1, 2, n4, tc), lambda o, q, c: (o, q, 0, 0, c)),
            pl.BlockSpec((1, 1, tc), lambda o, q, c: (o, 0, c)),
        ],
        out_specs=pl.BlockSpec((1, 1, 2 * n4, tc), lambda o, q, c: (o, q, 0, c)),
        out_shape=jax.ShapeDtypeStruct((2, nq, 2 * n4, width), F32),
        compiler_params=_params("arbitrary", "arbitrary", "arbitrary"),
        name="hy_spectra",
    )(fwd, hs, bias.reshape(2, 1, width))


CONV_LANES = 256


def _hy_conv_kernel(v_ref, g_ref, fwd_ref, inv_ref, ks_ref, o_ref, s_ref, p_ref, *, half):
    f = fwd_ref[...]
    for r in range(HY_PHASES):
        s_ref[r] = _dot(f, v_ref[0, r])
    nq = 2 * HY_PHASES - 1
    row0 = lax.broadcasted_iota(jnp.int32, (SUBLANES, CONV_LANES), 0) == 0
    for lo in range(0, half, SUBLANES):
        for c0 in range(0, v_ref.shape[-1], CONV_LANES):
            rows_re, rows_im, cols = slice(lo, lo + SUBLANES), slice(half + lo, half + lo + SUBLANES), \
                slice(c0, c0 + CONV_LANES)
            s_re = [s_ref[r, rows_re, cols] for r in range(HY_PHASES)]
            s_im = [s_ref[r, rows_im, cols] for r in range(HY_PHASES)]
            k_re = [ks_ref[0, q, rows_re, cols] for q in range(nq)]
            k_im = [ks_ref[0, q, rows_im, cols] for q in range(nq)]
            if lo == 0:
                s_ny, k_ny = [s[0:1] for s in s_im], [k[0:1] for k in k_im]
                s_im = [jnp.where(row0, 0.0, s) for s in s_im]
                k_im = [jnp.where(row0, 0.0, k) for k in k_im]
            for r in range(HY_PHASES):
                qs = [r - rp + HY_PHASES - 1 for rp in range(HY_PHASES)]
                p_re = functools.reduce(jnp.add, [s_re[rp] * k_re[q] - s_im[rp] * k_im[q] for rp, q in enumerate(qs)])
                p_im = functools.reduce(jnp.add, [s_re[rp] * k_im[q] + s_im[rp] * k_re[q] for rp, q in enumerate(qs)])
                if lo == 0:
                    p_ny = functools.reduce(jnp.add, [s_ny[rp] * k_ny[q] for rp, q in enumerate(qs)])
                    p_im = jnp.where(row0, p_ny, p_im)
                p_ref[r, rows_re, cols] = p_re
                p_ref[r, rows_im, cols] = p_im
    for r in range(HY_PHASES):
        o_ref[0, r] = (g_ref[0, r] * _dot(inv_ref[...], p_ref[r].astype(BF16))).astype(o_ref.dtype)


def _hy_conv(v, gate, fwd, inv, kspec, order, tc):
    bsz, _, n4, width = v.shape
    nq = 2 * HY_PHASES - 1
    blk = pl.BlockSpec((1, HY_PHASES, n4, tc), lambda c, b: (b, 0, 0, c))
    return pl.pallas_call(
        functools.partial(_hy_conv_kernel, half=n4),
        grid=(width // tc, bsz),
        in_specs=[
            blk, blk,
            pl.BlockSpec((2 * n4, n4), lambda c, b: (0, 0)),
            pl.BlockSpec((n4, 2 * n4), lambda c, b: (0, 0)),
            pl.BlockSpec((1, nq, 2 * n4, tc), lambda c, b: (order, 0, 0, c)),
        ],
        out_specs=blk,
        out_shape=jax.ShapeDtypeStruct(v.shape, BF16),
        scratch_shapes=[pltpu.VMEM((HY_PHASES, 2 * n4, tc), F32), pltpu.VMEM((HY_PHASES, 2 * n4, tc), F32)],
        compiler_params=_params("arbitrary", "arbitrary"),
        name="hy_conv",
    )(v, gate, fwd, inv, kspec)


def _gelu_tanh(x):
    return 0.5 * x * (1.0 + jnp.tanh(math.sqrt(2.0 / math.pi) * (x + 0.044715 * (x * x * x))))


def _rms(x, w):
    return x * lax.rsqrt(jnp.mean(x * x, axis=-1, keepdims=True) + EPS) * w


def _even_tail_kernel(x_ref, mod_ref, yf_ref, yb_ref, u_ref, ga_ref, hy_ref, unperm_ref, d_ref, gw_ref, gb_ref,
                      wo_ref, fw_ref, o_ref, *, final):
    y = yf_ref[0, 0].astype(F32) + yb_ref[0, 0].astype(F32) + u_ref[0] * d_ref[...]
    y = _gelu_tanh(y)
    y = y * jax.nn.sigmoid(_dot(y.astype(BF16), gw_ref[...]) + gb_ref[...])
    ya = (y * _silu(ga_ref[0].astype(F32))).astype(BF16)
    hy = _dot(unperm_ref[...], hy_ref[0].reshape(ya.shape[0], HY_WIDTH)).astype(BF16)
    out = _dot(ya, wo_ref[:S5_WIDTH, :]) + _dot(hy, wo_ref[S5_WIDTH:, :])
    xn = x_ref[0] + mod_ref[0][:, 2 * D_MODEL:] * out
    o_ref[0] = _rms(xn, fw_ref[...]) if final else xn


def _even_tail(x, mod, ys, u, g_a, hy, s5_d, glu_w, glu_b, w_out, final_w, final, tm=256):
    bsz, seq, _ = x.shape
    tok = lambda w: pl.BlockSpec((1, tm, w), lambda b, m: (b, m, 0))
    const = lambda r, c: pl.BlockSpec((r, c), lambda b, m: (0, 0))
    return pl.pallas_call(
        functools.partial(_even_tail_kernel, final=final),
        grid=(bsz, seq // tm),
        in_specs=[
            tok(D_MODEL), _mod_spec(mod),
            pl.BlockSpec((1, 1, tm, S5_WIDTH), lambda b, m: (0, b, m, 0)),
            pl.BlockSpec((1, 1, tm, S5_WIDTH), lambda b, m: (1, b, m, 0)),
            tok(S5_WIDTH), tok(S5_WIDTH), _phased_spec(tm, HY_WIDTH), const(tm, tm),
            const(1, S5_WIDTH), const(S5_WIDTH, S5_WIDTH), const(1, S5_WIDTH), const(D_INNER, D_MODEL),
            const(1, D_MODEL),
        ],
        out_specs=tok(D_MODEL),
        out_shape=jax.ShapeDtypeStruct((bsz, seq, D_MODEL), F32),
        compiler_params=_params("arbitrary", "arbitrary"),
        name="even_tail",
    )(x, mod, ys, ys, u, g_a, hy, _phase_perm(tm).T, s5_d.reshape(1, -1), glu_w, glu_b.reshape(1, -1), w_out,
      final_w.reshape(1, -1))


def _ssd_direction(xbc, dt, cum, st_ref, direction, d_row):
    q = SSD_CHUNK
    hp = LANES // SSD_HEADDIM
    ii = lax.broadcasted_iota(jnp.int32, (q, q), 0)
    jj = lax.broadcasted_iota(jnp.int32, (q, q), 1)
    if direction == 0:
        tri = (jj <= ii)
        last = q - 1
    else:
        tri = (jj >= ii)
        last = 0
    cum_t = cum.T
    src_t = cum_t - jnp.log(dt.T)
    w_t = jnp.exp(cum_t[:, last:last + 1] - src_t)
    last_row = cum[last:last + 1, :]
    first_half = lax.broadcasted_iota(jnp.int32, (q, LANES), 1) < SSD_HEADDIM
    first_half_row = lax.broadcasted_iota(jnp.int32, (1, LANES), 1) < SSD_HEADDIM
    x_all = xbc[:, :D_INNER]
    ys = []
    for g in range(SSD_GROUPS):
        b_g = xbc[:, D_INNER + g * SSD_STATE:D_INNER + (g + 1) * SSD_STATE]
        c_g = xbc[:, D_INNER + (SSD_GROUPS + g) * SSD_STATE:D_INNER + (SSD_GROUPS + g + 1) * SSD_STATE]
        b_bf, c_bf = b_g.astype(BF16), c_g.astype(BF16)
        cb = lax.dot_general(c_bf, b_bf, (((1,), (1,)), ((), ())), preferred_element_type=F32)
        b_t = b_g.T
        heads_per_group = SSD_HEADS // SSD_GROUPS
        for pair in range(heads_per_group // hp):
            h0 = g * heads_per_group + pair * hp
            lo = h0 * SSD_HEADDIM
            x_pair = x_all[:, lo:lo + LANES]
            x_blk = jnp.concatenate([jnp.where(first_half, x_pair, 0.0).astype(BF16),
                                     jnp.where(first_half, 0.0, x_pair).astype(BF16)], axis=0)
            scores, b_w, cols, tot = [], [], [], []
            for h in (h0, h0 + 1):
                hl = direction * SSD_HEADS + h
                col = jnp.broadcast_to(cum[:, hl:hl + 1], (q, q))
                arg = jnp.where(tri, col - src_t[hl:hl + 1, :], -jnp.inf)
                scores.append((cb * jnp.exp(arg)).astype(BF16))
                b_w.append((b_t * w_t[hl:hl + 1, :]).astype(BF16))
                cols.append(col)
                tot.append(jnp.broadcast_to(last_row[:, hl:hl + 1], (1, LANES)))
            s_pair = st_ref[:, lo:lo + LANES]
            y = _dot(jnp.concatenate(scores, axis=1), x_blk)
            y = y + _dot(c_bf, s_pair.astype(BF16)) * jnp.exp(jnp.where(first_half, cols[0], cols[1]))
            if d_row is not None:
                y = y + x_pair * d_row[:, lo:lo + LANES]
            ys.append(y)
            decay = jnp.exp(jnp.where(first_half_row, tot[0], tot[1]))
            st_ref[:, lo:lo + LANES] = s_pair * decay + _dot(jnp.concatenate(b_w, axis=1), x_blk)
    return jnp.concatenate(ys, axis=-1)


def _ssd_kernel(*refs, has_h0, want_final):
    xf_ref, xb_ref, dtf_ref, dtb_ref, cumf_ref, cumb_ref, d_ref = refs[:7]
    h0_ref = refs[7] if has_h0 else None
    yf_ref, yb_ref = refs[7 + has_h0:9 + has_h0]
    hf_ref = refs[9 + has_h0] if want_final else None
    st_ref = refs[-1]
    c = pl.program_id(1)
    nc = pl.num_programs(1)

    hp = LANES // SSD_HEADDIM
    pairs = [(dr, p) for dr in range(2) for p in range(SSD_HEADS // hp)]

    @pl.when(c == 0)
    def _():
        if not has_h0:
            st_ref[...] = jnp.zeros_like(st_ref)
        for dr, p in pairs if has_h0 else ():
            st_ref[dr, :, LANES * p:LANES * (p + 1)] = h0_ref[0, dr, hp * p:hp * (p + 1)].reshape(LANES, SSD_STATE).T

    yf = _ssd_direction(xf_ref[0].astype(F32), dtf_ref[0], cumf_ref[0], st_ref.at[0], 0, d_ref[...])
    yf_ref[0] = yf.astype(yf_ref.dtype)
    yb = _ssd_direction(xb_ref[0].astype(F32), dtb_ref[0], cumb_ref[0], st_ref.at[1], 1, None)
    yb_ref[0] = yb.astype(yb_ref.dtype)

    if want_final:
        @pl.when(c == nc - 1)
        def _():
            for dr, p in pairs:
                hf_ref[0, dr, hp * p:hp * (p + 1)] = (
                    st_ref[dr, :, LANES * p:LANES * (p + 1)].T.reshape(hp, SSD_HEADDIM, SSD_STATE))


def _ssd_scan(xbc, dt, cum, d_exp, h0, want_final):
    bsz, seq, width = xbc.shape
    q = SSD_CHUNK
    nc = seq // q
    fwd = lambda w: pl.BlockSpec((1, q, w), lambda b, c: (b, c, 0))
    bwd = lambda w: pl.BlockSpec((1, q, w), lambda b, c: (b, nc - 1 - c, 0))
    st_shape = (bsz, 2, SSD_HEADS, SSD_HEADDIM, SSD_STATE)
    st_spec = pl.BlockSpec((1,) + st_shape[1:], lambda b, c: (b, 0, 0, 0, 0))
    has_h0 = h0 is not None
    y_shape = jax.ShapeDtypeStruct((bsz, seq, D_INNER), BF16)
    outs = pl.pallas_call(
        functools.partial(_ssd_kernel, has_h0=has_h0, want_final=want_final),
        grid=(bsz, nc),
        in_specs=[fwd(width), bwd(width), fwd(LANES), bwd(LANES), fwd(LANES), bwd(LANES),
                  pl.BlockSpec((1, D_INNER), lambda b, c: (0, 0))] + [st_spec] * has_h0,
        out_specs=[fwd(D_INNER), bwd(D_INNER)] + [st_spec] * want_final,
        out_shape=[y_shape, y_shape] + [jax.ShapeDtypeStruct(st_shape, F32)] * want_final,
        scratch_shapes=[pltpu.VMEM((2, SSD_STATE, D_INNER), F32)],
        compiler_params=_params("arbitrary", "arbitrary"),
        name="ssd_scan",
    )(xbc, xbc, dt, dt, cum, cum, d_exp, *([h0] if has_h0 else []))
    return (outs[0], outs[1], outs[2] if want_final else None)


def _odd_tail_kernel(x_ref, mod_ref, yf_ref, yb_ref, z_ref, nw_ref, wo_ref, fw_ref, o_ref, *, final):
    y = (yf_ref[0].astype(F32) + yb_ref[0].astype(F32)) * _silu(z_ref[0].astype(F32))
    y = _rms(y, nw_ref[...]).astype(BF16)
    xn = x_ref[0] + mod_ref[0][:, 2 * D_MODEL:] * _dot(y, wo_ref[...])
    o_ref[0] = _rms(xn, fw_ref[...]) if final else xn


def _odd_tail(x, mod, yf, yb, z, norm_w, w_out, final_w, final, tm=256):
    bsz, seq, _ = x.shape
    tok = lambda w: pl.BlockSpec((1, tm, w), lambda b, m: (b, m, 0))
    const = lambda r, c: pl.BlockSpec((r, c), lambda b, m: (0, 0))
    return pl.pallas_call(
        functools.partial(_odd_tail_kernel, final=final),
        grid=(bsz, seq // tm),
        in_specs=[tok(D_MODEL), _mod_spec(mod), tok(D_INNER), tok(D_INNER), tok(D_INNER),
                  const(1, D_INNER), const(D_INNER, D_MODEL), const(1, D_MODEL)],
        out_specs=tok(D_MODEL),
        out_shape=jax.ShapeDtypeStruct((bsz, seq, D_MODEL), F32),
        compiler_params=_params("arbitrary", "arbitrary"),
        name="odd_tail",
    )(x, mod, yf, yb, z, norm_w.reshape(1, -1), w_out, final_w.reshape(1, -1))


def _even_layer(x, mod, norm_w, w_in, w_out, s5, s5_d, glu_w, glu_b, short_w, short_b, dft, kspec, h0,
                final_w, final, s5_tq, conv_tc):
    tm = min(TOKEN_BLOCK, x.shape[1])
    outs = ((S5_WIDTH, F32, False), (S5_WIDTH, BF16, False),
            (HY_WIDTH, BF16, True), (HY_WIDTH, BF16, True), (HY_WIDTH, BF16, True))
    u, g_a, v, x1, x2g = _inproj(_inproj_even_kernel, x, mod, norm_w, w_in,
                                 (_phase_perm(tm), short_w, short_b.reshape(1, -1)), outs, tm=tm)
    ys, hf = _s5_scan(u, *s5, h0, s5_tq)
    fwd, inv = dft
    z1 = _hy_conv(v, x1, fwd, inv, kspec, 0, conv_tc)
    hy = _hy_conv(z1, x2g, fwd, inv, kspec, 1, conv_tc)
    xn = _even_tail(x, mod, ys, u, g_a, hy, s5_d, glu_w, glu_b, w_out, final_w, final,
                    tm=min(TAIL_BLOCK, x.shape[1]))
    return xn, hf


def _odd_layer(x, mod, norm_w, w_in, conv_w, conv_b, a_log, dt_bias, d_exp, ssd_norm_w, w_out, h0, final_w, final):
    tm = min(TOKEN_BLOCK, x.shape[1])
    z, xbc, dt, cum = _inproj(functools.partial(_inproj_odd_kernel, tm=tm), x, mod, norm_w, w_in,
                              (conv_w, conv_b.reshape(1, -1), dt_bias, a_log),
                              ((D_INNER, BF16, False), (SSD_CONV_DIM, BF16, False), (LANES, F32, False),
                               (LANES, F32, False)), tm=tm)
    yf, yb, hf = _ssd_scan(xbc, dt, cum, d_exp, h0, want_final=h0 is None)
    xn = _odd_tail(x, mod, yf, yb, z, ssd_norm_w, w_out, final_w, final)
    return xn, hf


def kernel(x_prompt, x_sample, state_s5_re, state_s5_im, state_ssd, c, c_ctx, mod_w, mod_b, norm_w, final_norm_w,
           ev_w_in, ev_w_out, s5_a_re, s5_a_im, s5_log_dt, s5_b_re, s5_b_im, s5_c_re, s5_c_im, s5_d, s5_glu_w,
           s5_glu_b, hy_short_w, hy_short_b, hy_f_w1, hy_f_b1, hy_f_freq1, hy_f_w2, hy_f_b2, hy_f_freq2, hy_f_w3,
           hy_bias, ssd_w_in, ssd_conv_w, ssd_conv_b, ssd_a_log, ssd_dt_bias, ssd_d, ssd_norm_w, ssd_w_out):
    bp, lp, _ = x_prompt.shape
    bs, ls, _ = x_sample.shape
    rows = 2 * SUBLANES
    cvec = jnp.zeros((rows, D_MODEL), F32).at[0].set(c_ctx.astype(F32)).at[1:1 + bs].set(c.astype(F32))
    mod = _modulation(cvec, mod_w, mod_b)
    dft_p, dft_s = _dft_matrices(lp // HY_PHASES), _dft_matrices(ls // HY_PHASES)
    xp, xs = x_prompt, x_sample
    new_re, new_im, new_ssd = [], [], []
    for i in range(DEPTH):
        j = i // 2
        final = i == DEPTH - 1
        mod_p = mod[i, 0:1].reshape(1, 1, 3 * D_MODEL)
        mod_s = mod[i, 1:1 + bs].reshape(bs, 1, 3 * D_MODEL)
        if i % 2 == 0:
            s5 = _s5_prepare(s5_a_re[j], s5_a_im[j], s5_log_dt[j], s5_b_re[j], s5_b_im[j], s5_c_re[j], s5_c_im[j])
            w_in = ev_w_in[j].astype(BF16)
            w_out = ev_w_out[j].astype(BF16)
            glu_w = s5_glu_w[j].astype(BF16)
            fargs = (hy_f_w1[j], hy_f_b1[j], hy_f_freq1[j], hy_f_w2[j], hy_f_b2[j], hy_f_freq2[j], hy_f_w3[j])
            ks_p = _hy_spectra(_polyphase_filters(_hy_filters(lp, *fargs)), dft_p[0], hy_bias[j])
            ks_s = _hy_spectra(_polyphase_filters(_hy_filters(ls, *fargs)), dft_s[0], hy_bias[j])
            h0_p = jnp.zeros((2, bp, S5_STATE_LANES), F32)
            h0_s = _s5_state_to_lanes(state_s5_re[:, j].astype(F32), state_s5_im[:, j].astype(F32))
            common = (s5_d[j], glu_w, s5_glu_b[j], hy_short_w[j], hy_short_b[j])
            xp, hf = _even_layer(xp, mod_p, norm_w[i], w_in, w_out, s5, *common, dft_p, ks_p, h0_p,
                                 final_norm_w, final, s5_tq=32, conv_tc=512)
            xs, _ = _even_layer(xs, mod_s, norm_w[i], w_in, w_out, s5, *common, dft_s, ks_s, h0_s,
                                final_norm_w, final, s5_tq=64, conv_tc=256)
            re, im = _s5_state_from_lanes(hf)
            new_re.append(re)
            new_im.append(im)
        else:
            pad = LANES - 2 * SSD_HEADS
            w_in = jnp.pad(ssd_w_in[j], ((0, 0), (0, pad))).astype(BF16)
            w_out = ssd_w_out[j].astype(BF16)
            dt_bias = jnp.pad(ssd_dt_bias[j].reshape(1, -1).astype(F32), ((0, 0), (0, pad)))
            a_log = jnp.pad(ssd_a_log[j].reshape(1, -1).astype(F32), ((0, 0), (0, pad)))
            d_exp = jnp.repeat(ssd_d[j].astype(F32), SSD_HEADDIM).reshape(1, D_INNER)
            common = (ssd_conv_w[j], ssd_conv_b[j], a_log, dt_bias, d_exp, ssd_norm_w[j], w_out)
            h0_s = state_ssd[:, j].astype(F32)
            xp, hf = _odd_layer(xp, mod_p, norm_w[i], w_in, *common, None, final_norm_w, final)
            xs, _ = _odd_layer(xs, mod_s, norm_w[i], w_in, *common, h0_s, final_norm_w, final)
            new_ssd.append(hf)
    return (xp, xs, jnp.stack(new_re, axis=1), jnp.stack(new_im, axis=1), jnp.stack(new_ssd, axis=1))
```
